```python
import jax, jax.numpy as jnp
from jax import lax
import numpy as np

D_MODEL = 1024
BATCH = 8
SEQ = 4096
DEPTH = 4

HEAD_DIM = 64
MIX_WIDTH = D_MODEL
ATT_Q_HEADS = (MIX_WIDTH // 2) // HEAD_DIM
ATT_KV_HEADS = 2
ATT_GROUP = ATT_Q_HEADS // ATT_KV_HEADS
ATT_WIDTH = ATT_Q_HEADS * HEAD_DIM
ATT_KV_WIDTH = ATT_KV_HEADS * HEAD_DIM
WINDOW = 128
BLOCK = 128
RWKV_HEADS = (MIX_WIDTH // 4) // HEAD_DIM
RWKV_WIDTH = RWKV_HEADS * HEAD_DIM
DECAY_LORA = 64
ICLR_LORA = 64
GATE_LORA = 128
RWKV_GN_EPS = 64e-5
CONV_WIDTH = MIX_WIDTH - ATT_WIDTH - RWKV_WIDTH
CONV_K = 3
ATT_PROJ_WIDTH = ATT_WIDTH + 2 * ATT_KV_WIDTH
RWKV_SIZES = (RWKV_WIDTH, RWKV_WIDTH, RWKV_WIDTH, DECAY_LORA, ICLR_LORA, GATE_LORA)
RWKV_PROJ_WIDTH = 3 * RWKV_WIDTH + DECAY_LORA + ICLR_LORA + GATE_LORA
CONV_PROJ_WIDTH = 3 * CONV_WIDTH
PROJ_WIDTH = ATT_PROJ_WIDTH + RWKV_PROJ_WIDTH + CONV_PROJ_WIDTH
D_FF = 2816
N_MOD = 9
EPS = 1e-6

kernel_name = "hymba_style_hybrid_swa_rwkv7_shortconv_macaron_adaln"


def rms_norm(x, gain):
    xf = x.astype(jnp.float32)
    y = xf * lax.rsqrt(jnp.mean(xf * xf, axis=-1, keepdims=True) + EPS)
    return (y * gain.astype(jnp.float32)).astype(x.dtype)


def swiglu(x, w_in, w_out):
    gate, up = jnp.split(x @ w_in, 2, axis=-1)
    return (jax.nn.silu(gate) * up) @ w_out


def token_shift(p):
    return jnp.pad(p, ((0, 0), (1, 0), (0, 0)))[:, :-1]


def sliding_window_sink_attention(qkv, q_gain, k_gain, sinks):
    B, T, _ = qkv.shape
    nb = T // BLOCK
    q, k, v = jnp.split(qkv, [ATT_WIDTH, ATT_WIDTH + ATT_KV_WIDTH], axis=-1)
    q = rms_norm(q.reshape(B, T, ATT_Q_HEADS, HEAD_DIM), q_gain)
    k = rms_norm(k.reshape(B, T, ATT_KV_HEADS, HEAD_DIM), k_gain)
    v = v.reshape(B, T, ATT_KV_HEADS, HEAD_DIM)
    qb = q.reshape(B, nb, BLOCK, ATT_KV_HEADS, ATT_GROUP, HEAD_DIM)

    def band(z):
        zb = z.reshape(B, nb, BLOCK, ATT_KV_HEADS, HEAD_DIM)
        prev = jnp.pad(zb, ((0, 0), (1, 0), (0, 0), (0, 0), (0, 0)))[:, :-1]
        return jnp.concatenate([prev, zb], axis=2)

    kb, vb = band(k), band(v)
    scores = jnp.einsum('bnqhgd,bnkhd->bnhgqk', qb, kb).astype(jnp.float32) * (HEAD_DIM ** -0.5)
    blk = jnp.arange(nb)[:, None, None] * BLOCK
    q_pos = blk + jnp.arange(BLOCK)[None, :, None]
    k_pos = blk - BLOCK + jnp.arange(2 * BLOCK)[None, None, :]
    diff = q_pos - k_pos
    mask = (k_pos >= 0) & (diff >= 0) & (diff < WINDOW)
    scores = jnp.where(mask[None, :, None, None], scores, -jnp.inf)
    sink = sinks.astype(jnp.float32).reshape(ATT_KV_HEADS, ATT_GROUP)[None, None, :, :, None, None]
    sink = jnp.broadcast_to(sink, scores.shape[:-1] + (1,))
    probs = jax.nn.softmax(jnp.concatenate([scores, sink], axis=-1), axis=-1)[..., :-1]
    out = jnp.einsum('bnhgqk,bnkhd->bnqhgd', probs.astype(v.dtype), vb)
    return out.reshape(B, T, ATT_WIDTH)


def rwkv7_time_mix(p, mu, w0, w_w2, a0, w_a2, w_g2, k_k, k_a, r_k, gn_w, gn_b):
    B, T, _ = p.shape
    f32 = jnp.float32
    p = p + mu * (token_shift(p) - p)
    r, k, v, wd, ad, gd = jnp.split(p, np.cumsum(RWKV_SIZES)[:-1].tolist(), axis=-1)
    r, k, v = r.astype(f32), k.astype(f32), v.astype(f32)
    w_log = -jax.nn.softplus(-(w0 + jnp.tanh(wd) @ w_w2).astype(f32)) - 0.5
    decay = jnp.exp(-jnp.exp(w_log))
    a = jax.nn.sigmoid((a0 + ad @ w_a2).astype(f32))
    g = (jax.nn.sigmoid(gd) @ w_g2).astype(f32)
    kk = (k * k_k).reshape(B, T, RWKV_HEADS, HEAD_DIM)
    kk = kk / jnp.maximum(jnp.linalg.norm(kk, axis=-1, keepdims=True), 1e-12)
    k = k * (1.0 + (a - 1.0) * k_a)
    hs = lambda z: z.reshape(B, T, RWKV_HEADS, HEAD_DIM)
    r, k, v, a, decay = hs(r), hs(k), hs(v), hs(a), hs(decay)

    def step(S, inp):
        r_t, w_t, k_t, v_t, kk_t, a_t = inp
        s_kk = jnp.einsum('bhvk,bhk->bhv', S, kk_t)
        S = (S * w_t[:, :, None, :] - s_kk[..., None] * (kk_t * a_t)[:, :, None, :]
             + v_t[..., None] * k_t[:, :, None, :])
        return S, jnp.einsum('bhvk,bhk->bhv', S, r_t)

    xs = tuple(jnp.swapaxes(z, 0, 1) for z in (r, decay, k, v, kk, a))
    S0 = jnp.zeros((B, RWKV_HEADS, HEAD_DIM, HEAD_DIM), f32)
    _, y = lax.scan(step, S0, xs)
    y = jnp.swapaxes(y, 0, 1)
    mean = jnp.mean(y, axis=-1, keepdims=True)
    var = jnp.mean(jnp.square(y - mean), axis=-1, keepdims=True)
    y = ((y - mean) * lax.rsqrt(var + RWKV_GN_EPS)).reshape(B, T, RWKV_WIDTH) * gn_w + gn_b
    bonus = jnp.sum(r * k * r_k, axis=-1, keepdims=True) * v
    y = (y + bonus.reshape(B, T, RWKV_WIDTH)) * g
    return y.astype(p.dtype)


def short_conv_mix(p, conv_w):
    b_gate, c_gate, h = jnp.split(p, [CONV_WIDTH, 2 * CONV_WIDTH], axis=-1)
    u = c_gate * h
    y = lax.conv_general_dilated(u, conv_w[:, None, :].astype(u.dtype), window_strides=(1,),
                                 padding=[(CONV_K - 1, 0)], dimension_numbers=('NWC', 'WIO', 'NWC'),
                                 feature_group_count=CONV_WIDTH)
    return b_gate * y


def setup_inputs(seed: int = 0) -> dict:
    key = jax.random.key(seed)
    ks = iter(jax.random.split(key, 32))
    nrm = lambda shape, s: jax.random.normal(next(ks), shape, jnp.float32) * s
    L, D = DEPTH, D_MODEL
    return {
        'x': nrm((BATCH, SEQ, D), 1.0),
        'c': nrm((BATCH, D), 1.0),
        'w_ada': nrm((L, D, N_MOD * D), 0.1 * D ** -0.5),
        'b_ada': nrm((L, N_MOD * D), 0.01),
        'g_ffn1': 1.0 + nrm((L, D), 0.05),
        'w_ffn1_in': nrm((L, D, 2 * D_FF), D ** -0.5),
        'w_ffn1_out': nrm((L, D_FF, D), D_FF ** -0.5),
        'g_mix': 1.0 + nrm((L, D), 0.05),
        'w_mix_in': nrm((L, D, PROJ_WIDTH), D ** -0.5),
        'w_mix_out': nrm((L, MIX_WIDTH, D), MIX_WIDTH ** -0.5),
        'att_q_gain': 1.0 + nrm((L, HEAD_DIM), 0.05),
        'att_k_gain': 1.0 + nrm((L, HEAD_DIM), 0.05),
        'att_sinks': nrm((L, ATT_Q_HEADS), 1.0),
        'rwkv_mu': jax.random.uniform(next(ks), (L, RWKV_PROJ_WIDTH), jnp.float32),
        'rwkv_w0': jax.random.uniform(next(ks), (L, RWKV_WIDTH), jnp.float32, -6.0, -1.0),
        'rwkv_w_w2': nrm((L, DECAY_LORA, RWKV_WIDTH), 0.1 * DECAY_LORA ** -0.5),
        'rwkv_a0': nrm((L, RWKV_WIDTH), 0.1),
        'rwkv_a_w2': nrm((L, ICLR_LORA, RWKV_WIDTH), 0.1 * ICLR_LORA ** -0.5),
        'rwkv_g_w2': nrm((L, GATE_LORA, RWKV_WIDTH), GATE_LORA ** -0.5),
        'rwkv_k_k': 1.0 + nrm((L, RWKV_WIDTH), 0.1),
        'rwkv_k_a': 1.0 + nrm((L, RWKV_WIDTH), 0.1),
        'rwkv_r_k': nrm((L, RWKV_HEADS, HEAD_DIM), 0.1),
        'rwkv_gn_w': 1.0 + nrm((L, RWKV_WIDTH), 0.05),
        'rwkv_gn_b': nrm((L, RWKV_WIDTH), 0.01),
        'conv_w': nrm((L, CONV_K, CONV_WIDTH), CONV_K ** -0.5),
        'g_ffn2': 1.0 + nrm((L, D), 0.05),
        'w_ffn2_in': nrm((L, D, 2 * D_FF), D ** -0.5),
        'w_ffn2_out': nrm((L, D_FF, D), D_FF ** -0.5),
    }


def reference(x, c, w_ada, b_ada, g_ffn1, w_ffn1_in, w_ffn1_out, g_mix, w_mix_in, w_mix_out,
              att_q_gain, att_k_gain, att_sinks, rwkv_mu, rwkv_w0, rwkv_w_w2, rwkv_a0, rwkv_a_w2,
              rwkv_g_w2, rwkv_k_k, rwkv_k_a, rwkv_r_k, rwkv_gn_w, rwkv_gn_b, conv_w,
              g_ffn2, w_ffn2_in, w_ffn2_out):
    h = x
    B = x.shape[0]
    c_act = jax.nn.silu(c)
    for l in range(DEPTH):
        mod = (c_act @ w_ada[l] + b_ada[l]).reshape(B, N_MOD, D_MODEL)
        sh1, sc1, gt1, sh2, sc2, gt2, sh3, sc3, gt3 = [mod[:, i, None, :] for i in range(N_MOD)]
        hn = rms_norm(h, g_ffn1[l]) * (1.0 + sc1) + sh1
        h = h + 0.5 * (1.0 + gt1) * swiglu(hn, w_ffn1_in[l], w_ffn1_out[l])
        hn = rms_norm(h, g_mix[l]) * (1.0 + sc2) + sh2
        proj = hn @ w_mix_in[l]
        p_att, p_rwkv, p_conv = jnp.split(proj, [ATT_PROJ_WIDTH, ATT_PROJ_WIDTH + RWKV_PROJ_WIDTH], axis=-1)
        y_att = sliding_window_sink_attention(p_att, att_q_gain[l], att_k_gain[l], att_sinks[l])
        y_rwkv = rwkv7_time_mix(p_rwkv, rwkv_mu[l], rwkv_w0[l], rwkv_w_w2[l], rwkv_a0[l], rwkv_a_w2[l],
                                rwkv_g_w2[l], rwkv_k_k[l], rwkv_k_a[l], rwkv_r_k[l], rwkv_gn_w[l], rwkv_gn_b[l])
        y_conv = short_conv_mix(p_conv, conv_w[l])
        mixed = jnp.concatenate([y_att, y_rwkv, y_conv], axis=-1) @ w_mix_out[l]
        h = h + (1.0 + gt2) * mixed
        hn = rms_norm(h, g_ffn2[l]) * (1.0 + sc3) + sh3
        h = h + 0.5 * (1.0 + gt3) * swiglu(hn, w_ffn2_in[l], w_ffn2_out[l])
    return h
```

```python
import functools
import math

import jax
import jax.numpy as jnp
from jax import lax
from jax.experimental import pallas as pl
from jax.experimental.pallas import tpu as pltpu

F32 = jnp.float32
BF16 = jnp.bfloat16

HEAD_DIM = 64
ATT_Q_HEADS = 8
ATT_KV_HEADS = 2
ATT_WIDTH = ATT_Q_HEADS * HEAD_DIM
ATT_KV_WIDTH = ATT_KV_HEADS * HEAD_DIM
ATT_PROJ_WIDTH = ATT_WIDTH + 2 * ATT_KV_WIDTH
ATT_BLOCK = 128
RWKV_HEADS = 4
RWKV_WIDTH = RWKV_HEADS * HEAD_DIM
RWKV_LORA = 128
RWKV_PROJ_WIDTH = 3 * RWKV_WIDTH + 2 * RWKV_LORA
RWKV_GN_EPS = 64e-5
RWKV_CHUNK = 64
CONV_WIDTH = 256
CONV_K = 3
N_MOD = 9
EPS = 1e-6
NEG_BIG = -1e30
EXP_M05 = math.exp(-0.5)

TOKEN_TILE = 512
VMEM_LIMIT = 56 * 1024 * 1024


def _dot(a, b):
    return jnp.dot(a, b, preferred_element_type=F32)


def _dot_nt(a, b):
    return lax.dot_general(a, b, (((1,), (1,)), ((), ())), preferred_element_type=F32)


def _dot_tn(a, b):
    return lax.dot_general(a, b, (((0,), (0,)), ((), ())), preferred_element_type=F32)


def _split3(x):
    hi = x.astype(BF16)
    r1 = x - hi.astype(F32)
    mid = r1.astype(BF16)
    lo = (r1 - mid.astype(F32)).astype(BF16)
    return hi, mid, lo


def _dot_exact_rhs(x, m):
    hi, mid, lo = _split3(x)
    return _dot(hi, m) + _dot(mid, m) + _dot(lo, m)


def _dot_exact_lhs(m, x):
    hi, mid, lo = _split3(x)
    return _dot(m, hi) + _dot(m, mid) + _dot(m, lo)


def _block_ones(n, blk):
    r = lax.broadcasted_iota(jnp.int32, (n, n), 0) // blk
    c = lax.broadcasted_iota(jnp.int32, (n, n), 1) // blk
    return jnp.where(r == c, 1.0, 0.0).astype(BF16)


def _const_spec(shape):
    nd = len(shape)
    return pl.BlockSpec(shape, lambda *_: (0,) * nd, pipeline_mode=pl.Buffered(1))


def _layer_spec(shape, layer):
    nd = len(shape)
    return pl.BlockSpec((None,) + tuple(shape), lambda *_: (layer,) + (0,) * nd,
                        pipeline_mode=pl.Buffered(1))


def _params(*sem):
    return pltpu.CompilerParams(dimension_semantics=sem, vmem_limit_bytes=VMEM_LIMIT)


def _modulated_norm(x, gain, shift, scale):
    ms = jnp.mean(x * x, axis=-1, keepdims=True)
    return (x * lax.rsqrt(ms + EPS) * gain) * (1.0 + scale) + shift


def _mod_kernel(c_ref, w_ref, b_ref, o_ref):
    c = c_ref[...]
    act = (c * jax.nn.sigmoid(c)).astype(BF16)
    o_ref[...] = _dot(act, w_ref[...].astype(BF16)) + b_ref[...]


def _modulation(c, w_ada, b_ada):
    n_layers, d, n = w_ada.shape
    b = c.shape[0]
    tn = d
    return pl.pallas_call(
        _mod_kernel,
        grid=(n_layers, n // tn),
        in_specs=[
            pl.BlockSpec((b, d), lambda l, j: (0, 0)),
            pl.BlockSpec((None, d, tn), lambda l, j: (l, 0, j)),
            pl.BlockSpec((None, 1, tn), lambda l, j: (l, 0, j)),
        ],
        out_specs=pl.BlockSpec((None, b, tn), lambda l, j: (l, 0, j)),
        out_shape=jax.ShapeDtypeStruct((n_layers, b, n), F32),
        compiler_params=_params("arbitrary", "arbitrary"),
        name="adaln_mod",
    )(c, w_ada, b_ada.reshape(n_layers, 1, n))


def _ffn_kernel(mod_ref, g_ref, x_ref, wg_ref, wu_ref, wo_ref, o_ref, *, n_chunks):
    x = x_ref[...]
    hn = _modulated_norm(x, g_ref[...], mod_ref[0:1, :], mod_ref[1:2, :]).astype(BF16)
    d_ff = wg_ref.shape[1]
    tf = d_ff // n_chunks
    acc = None
    for j in range(n_chunks):
        sl = slice(j * tf, (j + 1) * tf)
        gate = _dot(hn, wg_ref[:, sl])
        up = _dot(hn, wu_ref[:, sl])
        act = (gate * jax.nn.sigmoid(gate) * up).astype(BF16)
        part = _dot(act, wo_ref[sl, :])
        acc = part if acc is None else acc + part
    o_ref[...] = x + (0.5 * (1.0 + mod_ref[2:3, :])) * acc


def _ffn(h, mod3, gain, w_in, w_out, layer):
    b, t, d = h.shape
    d_ff = w_out.shape[1]
    tm = min(TOKEN_TILE, t)
    n_chunks = 2 if d_ff % 256 == 0 else 1
    kern = functools.partial(_ffn_kernel, n_chunks=n_chunks)
    w_gate_spec = pl.BlockSpec((None, d, d_ff), lambda i, j: (layer, 0, 0), pipeline_mode=pl.Buffered(1))
    w_up_spec = pl.BlockSpec((None, d, d_ff), lambda i, j: (layer, 0, 1), pipeline_mode=pl.Buffered(1))
    return pl.pallas_call(
        kern,
        grid=(b, t // tm),
        in_specs=[
            pl.BlockSpec((None, 3, d), lambda i, j: (i, 0, 0)),
            _layer_spec((1, d), layer),
            pl.BlockSpec((None, tm, d), lambda i, j: (i, j, 0)),
            w_gate_spec,
            w_up_spec,
            _layer_spec((d_ff, d), layer),
        ],
        out_specs=pl.BlockSpec((None, tm, d), lambda i, j: (i, j, 0)),
        out_shape=jax.ShapeDtypeStruct((b, t, d), F32),
        compiler_params=_params("arbitrary", "arbitrary"),
        name="macaron_ffn",
    )(mod3, gain, h, w_in, w_in, w_out)


def _mixin_kernel(mod_ref, g_ref, x_ref, watt_ref, wrwkv_ref, wconv_ref, cw_ref,
                  patt_ref, prwkv_ref, yconv_ref, carry_ref):
    @pl.when(pl.program_id(1) == 0)
    def _():
        carry_ref[...] = jnp.zeros_like(carry_ref)

    x = x_ref[...]
    hn = _modulated_norm(x, g_ref[...], mod_ref[0:1, :], mod_ref[1:2, :]).astype(BF16)
    patt_ref[...] = _dot(hn, watt_ref[...]).astype(patt_ref.dtype)
    prwkv_ref[...] = _dot(hn, wrwkv_ref[...])
    pc = _dot(hn, wconv_ref[...])
    cwid = yconv_ref.shape[-1]
    b_gate = pc[:, 0:cwid]
    u = pc[:, cwid:2 * cwid] * pc[:, 2 * cwid:3 * cwid]
    tm = u.shape[0]
    row = lax.broadcasted_iota(jnp.int32, (tm, 1), 0)
    prev1 = carry_ref[1:2, :]
    prev2 = carry_ref[0:1, :]
    u1 = jnp.where(row == 0, prev1, pltpu.roll(u, 1, 0))
    u2 = jnp.where(row == 0, prev2, jnp.where(row == 1, prev1, pltpu.roll(u, 2, 0)))
    y = cw_ref[0:1, :] * u2 + cw_ref[1:2, :] * u1 + cw_ref[2:3, :] * u
    yconv_ref[...] = (b_gate * y).astype(yconv_ref.dtype)
    carry_ref[0:2, :] = u[tm - 2:tm, :]


def _mixin(h, mod3, gain, w_att, w_rwkv, w_conv, conv_w, layer):
    b, t, d = h.shape
    tm = min(TOKEN_TILE, t)
    tok = lambda width: pl.BlockSpec((None, tm, width), lambda i, j: (i, j, 0))
    return pl.pallas_call(
        _mixin_kernel,
        grid=(b, t // tm),
        in_specs=[
            pl.BlockSpec((None, 3, d), lambda i, j: (i, 0, 0)),
            _layer_spec((1, d), layer),
            tok(d),
            _layer_spec((d, ATT_PROJ_WIDTH), layer),
            _layer_spec((d, RWKV_PROJ_WIDTH), layer),
            _layer_spec((d, 3 * CONV_WIDTH), layer),
            _layer_spec((CONV_K, CONV_WIDTH), layer),
        ],
        out_specs=[tok(ATT_PROJ_WIDTH), tok(RWKV_PROJ_WIDTH), tok(CONV_WIDTH)],
        out_shape=[
            jax.ShapeDtypeStruct((b, t, ATT_PROJ_WIDTH), BF16),
            jax.ShapeDtypeStruct((b, t, RWKV_PROJ_WIDTH), F32),
            jax.ShapeDtypeStruct((b, t, CONV_WIDTH), BF16),
        ],
        scratch_shapes=[pltpu.VMEM((8, CONV_WIDTH), F32)],
        compiler_params=_params("arbitrary", "arbitrary"),
        name="mix_in_proj",
    )(mod3, gain, h, w_att, w_rwkv, w_conv, conv_w)


def _head_rms(x, ones_blk, gain):
    ss = _dot((x * x).astype(BF16), ones_blk)
    return x * lax.rsqrt(ss * (1.0 / HEAD_DIM) + EPS) * gain


def _attn_kernel(sink_ref, qg_ref, kg_ref, q_ref, kvc_ref, kvp_ref, o_ref):
    n = pl.program_id(1)
    blk = ATT_BLOCK
    lane128 = lax.broadcasted_iota(jnp.int32, (1, 2 * HEAD_DIM), 1)
    low = lane128 < HEAD_DIM

    q = q_ref[...].astype(F32)
    qn = _head_rms(q, _block_ones(ATT_WIDTH, HEAD_DIM), qg_ref[...]).astype(BF16)
    kv = jnp.concatenate([kvp_ref[...], kvc_ref[...]], axis=0).astype(F32)
    k = kv[:, 0:ATT_KV_WIDTH]
    v = kv[:, ATT_KV_WIDTH:2 * ATT_KV_WIDTH]
    kn = _head_rms(k, _block_ones(ATT_KV_WIDTH, HEAD_DIM), kg_ref[...])
    kr = pltpu.roll(kn, HEAD_DIM, 1)
    vr = pltpu.roll(v, HEAD_DIM, 1)
    zero = jnp.zeros_like(kn)
    k_low = [jnp.where(low, kn, zero), jnp.where(low, kr, zero)]
    k_high = [jnp.where(low, zero, kr), jnp.where(low, zero, kn)]
    v_low = [jnp.where(low, v, zero), jnp.where(low, vr, zero)]
    v_high = [jnp.where(low, zero, vr), jnp.where(low, zero, v)]

    rows = 2 * blk
    ri = lax.broadcasted_iota(jnp.int32, (rows, 4 * blk), 0) & (blk - 1)
    cj = lax.broadcasted_iota(jnp.int32, (rows, 4 * blk), 1) & (2 * blk - 1)
    visible = (cj > ri) & (cj <= ri + blk) & ((cj >= blk) | (n > 0))
    top = lax.broadcasted_iota(jnp.int32, (rows, 1), 0) < blk
    half_r = lax.broadcasted_iota(jnp.int32, (4 * blk, 2 * HEAD_DIM), 0) // (2 * blk)
    half_c = lax.broadcasted_iota(jnp.int32, (4 * blk, 2 * HEAD_DIM), 1) // HEAD_DIM
    den_sel = jnp.where(half_r == half_c, 1.0, 0.0).astype(BF16)

    for g in range(ATT_KV_HEADS):
        qg = jnp.concatenate([qn[:, 256 * g:256 * g + 128], qn[:, 256 * g + 128:256 * g + 256]], axis=0)
        kcat = jnp.concatenate([k_low[g], k_high[g]], axis=0).astype(BF16)
        vcat = jnp.concatenate([v_low[g], v_high[g]], axis=0).astype(BF16)
        s = _dot_nt(qg, kcat) * (HEAD_DIM ** -0.5)
        s = jnp.where(visible, s, NEG_BIG)
        h0 = 4 * g
        sink_e = jnp.where(top, sink_ref[h0], sink_ref[h0 + 2])
        sink_o = jnp.where(top, sink_ref[h0 + 1], sink_ref[h0 + 3])
        s_e = s[:, 0:2 * blk]
        s_o = s[:, 2 * blk:4 * blk]
        m_e = jnp.maximum(jnp.max(s_e, axis=-1, keepdims=True), sink_e)
        m_o = jnp.maximum(jnp.max(s_o, axis=-1, keepdims=True), sink_o)
        p = jnp.concatenate([jnp.exp(s_e - m_e), jnp.exp(s_o - m_o)], axis=1).astype(BF16)
        num = _dot(p, vcat)
        den = _dot(p, den_sel) + jnp.where(low, jnp.exp(sink_e - m_e), jnp.exp(sink_o - m_o))
        out = (num / den).astype(o_ref.dtype)
        o_ref[:, 256 * g:256 * g + 128] = out[0:blk]
        o_ref[:, 256 * g + 128:256 * g + 256] = out[blk:rows]


def _attention(p_att, q_gain, k_gain, sinks):
    b, t, _ = p_att.shape
    blk = ATT_BLOCK
    scale_gain = lambda g, reps: jnp.tile(g.astype(F32), reps).reshape(1, reps * HEAD_DIM)
    return pl.pallas_call(
        _attn_kernel,
        grid=(b, t // blk),
        in_specs=[
            pl.BlockSpec(memory_space=pltpu.SMEM),
            _const_spec((1, ATT_WIDTH)),
            _const_spec((1, ATT_KV_WIDTH)),
            pl.BlockSpec((None, blk, ATT_WIDTH), lambda i, n: (i, n, 0)),
            pl.BlockSpec((None, blk, 2 * ATT_KV_WIDTH), lambda i, n: (i, n, ATT_WIDTH // (2 * ATT_KV_WIDTH))),
            pl.BlockSpec((None, blk, 2 * ATT_KV_WIDTH),
                         lambda i, n: (i, jnp.maximum(n - 1, 0), ATT_WIDTH // (2 * ATT_KV_WIDTH))),
        ],
        out_specs=pl.BlockSpec((None, blk, ATT_WIDTH), lambda i, n: (i, n, 0)),
        out_shape=jax.ShapeDtypeStruct((b, t, ATT_WIDTH), BF16),
        compiler_params=_params("arbitrary", "arbitrary"),
        name="swa_sink_attention",
    )(sinks.astype(F32), scale_gain(q_gain, ATT_Q_HEADS), scale_gain(k_gain, ATT_KV_HEADS),
      p_att, p_att, p_att)


def _rwkv_kernel(p_ref, mu_ref, w0_ref, ww2_ref, a0_ref, wa2_ref, wg2_ref, kk_ref, ka_ref, rk_ref,
                 gnw_ref, gnb_ref, o_ref, prev_ref, state_ref):
    c_len = p_ref.shape[0]
    width = RWKV_WIDTH
    n_heads = RWKV_HEADS
    rows = n_heads * c_len

    @pl.when(pl.program_id(1) == 0)
    def _():
        prev_ref[...] = jnp.zeros_like(prev_ref)
        state_ref[...] = jnp.zeros_like(state_ref)

    p = p_ref[...]
    trow = lax.broadcasted_iota(jnp.int32, (c_len, 1), 0)
    p_prev = jnp.where(trow == 0, prev_ref[0:1, :], pltpu.roll(p, 1, 0))
    prev_ref[0:1, :] = p[c_len - 1:c_len, :]
    xs = p + mu_ref[...] * (p_prev - p)
    r = xs[:, 0:width]
    k = xs[:, width:2 * width]
    v = xs[:, 2 * width:3 * width]
    lora = xs[:, 3 * width:3 * width + RWKV_LORA]
    gate_in = xs[:, 3 * width + RWKV_LORA:]

    dw = _dot(jnp.tanh(lora).astype(BF16), ww2_ref[...])
    da = _dot(lora.astype(BF16), wa2_ref[...])
    g = _dot(jax.nn.sigmoid(gate_in).astype(BF16), wg2_ref[...])
    lw = -EXP_M05 * jax.nn.sigmoid(w0_ref[...] + dw)
    a = jax.nn.sigmoid(a0_ref[...] + da)

    ones_head = _block_ones(width, HEAD_DIM)
    head_sum = lambda z: _dot_exact_rhs(z, ones_head)
    kk_raw = k * kk_ref[...]
    kk = kk_raw * lax.rsqrt(jnp.maximum(head_sum(kk_raw * kk_raw), 1e-24))
    kmod = k * (1.0 + (a - 1.0) * ka_ref[...])
    b = kk * a

    tri_r = lax.broadcasted_iota(jnp.int32, (c_len, c_len), 0)
    tri_c = lax.broadcasted_iota(jnp.int32, (c_len, c_len), 1)
    tri = jnp.where(tri_c <= tri_r, 1.0, 0.0).astype(BF16)
    lc = _dot_exact_lhs(tri, lw)
    ltot = lc[c_len - 1:c_len, :]
    w_incl = jnp.exp(lc)
    w_excl = jnp.exp(lc - lw)
    w_inv = jnp.exp(-lc)
    w_end = jnp.exp(ltot - lc)

    lane_head = lax.broadcasted_iota(jnp.int32, (1, width), 1) // HEAD_DIM

    def stack(z):
        zb = z.astype(BF16)
        zero = jnp.zeros_like(zb)
        return jnp.concatenate([jnp.where(lane_head == h, zb, zero) for h in range(n_heads)], axis=0)

    def unstack(z):
        out = z[0:c_len]
        for h in range(1, n_heads):
            out = out + z[h * c_len:(h + 1) * c_len]
        return out

    at_s = stack(kk * w_excl)
    rt = r * w_incl
    rt_s = stack(rt)
    bt_s = stack(b * w_inv)
    kt_s = stack(kmod * w_inv)
    v_s = stack(v)
    bh_s = stack(b * w_end)
    kh_s = stack(kmod * w_end)

    gram = _dot_nt(jnp.concatenate([at_s, rt_s], axis=0), jnp.concatenate([bt_s, kt_s], axis=0))
    ti = lax.broadcasted_iota(jnp.int32, (rows, rows), 0) & (c_len - 1)
    si = lax.broadcasted_iota(jnp.int32, (rows, rows), 1) & (c_len - 1)
    strict = si < ti
    incl = si <= ti
    a_ab = jnp.where(strict, gram[0:rows, 0:rows], 0.0)
    a_ak = jnp.where(strict, gram[0:rows, rows:2 * rows], 0.0).astype(BF16)
    a_rb = jnp.where(incl, gram[rows:2 * rows, 0:rows], 0.0).astype(BF16)
    a_rk = jnp.where(incl, gram[rows:2 * rows, rows:2 * rows], 0.0).astype(BF16)

    eye = lax.broadcasted_iota(jnp.int32, (rows, rows), 0) == lax.broadcasted_iota(jnp.int32, (rows, rows), 1)
    t_inv = jnp.where(eye, 1.0, 0.0) - a_ab
    pw = a_ab.astype(BF16)
    n_sq = int(math.log2(c_len)) - 1
    for i in range(n_sq):
        pw_f = _dot(pw, pw)
        pw = pw_f.astype(BF16)
        t_inv = t_inv + _dot(t_inv.astype(BF16), pw)
    t_b = t_inv.astype(BF16)

    y1 = _dot(a_ak, v_s).astype(BF16)
    at2 = _dot(t_b, at_s)
    u2 = _dot(t_b, y1)
    at2_b = at2.astype(BF16)
    u2_b = u2.astype(BF16)
    r2 = unstack(jnp.concatenate([jnp.where(lane_head == h, rt, 0.0) for h in range(n_heads)], axis=0)
                 - _dot(a_rb, at2_b))
    o2 = unstack(_dot(a_rk, v_s) - _dot(a_rb, u2_b))
    diag_w = jnp.where(lax.broadcasted_iota(jnp.int32, (width, width), 0)
                       == lax.broadcasted_iota(jnp.int32, (width, width), 1), jnp.exp(ltot), 0.0)
    p_t = diag_w - _dot_tn(bh_s, at2_b)
    q_t = _dot_tn(kh_s, v_s) - _dot_tn(bh_s, u2_b)

    s0 = state_ref[...]
    s0_b = s0.astype(BF16)
    y = _dot(r2.astype(BF16), s0_b) + o2
    state_ref[...] = _dot(p_t.astype(BF16), s0_b) + q_t

    mean = head_sum(y) * (1.0 / HEAD_DIM)
    dev = y - mean
    var = head_sum(dev * dev) * (1.0 / HEAD_DIM)
    yn = dev * lax.rsqrt(var + RWKV_GN_EPS) * gnw_ref[...] + gnb_ref[...]
    bonus = head_sum(r * kmod * rk_ref[...]) * v
    o_ref[...] = ((yn + bonus) * g).astype(o_ref.dtype)


def _rwkv(p_rwkv, mu, w0, ww2_pad, a0, wa2_pad, wg2, k_k, k_a, r_k, gn_w, gn_b, layer):
    b, t, pw = p_rwkv.shape
    c_len = min(RWKV_CHUNK, t)
    vec = lambda n: _layer_spec((1, n), layer)
    return pl.pallas_call(
        _rwkv_kernel,
        grid=(b, t // c_len),
        in_specs=[
            pl.BlockSpec((None, c_len, pw), lambda i, c: (i, c, 0)),
            vec(pw), vec(RWKV_WIDTH),
            _layer_spec((RWKV_LORA, RWKV_WIDTH), layer),
            vec(RWKV_WIDTH),
            _layer_spec((RWKV_LORA, RWKV_WIDTH), layer),
            _layer_spec((RWKV_LORA, RWKV_WIDTH), layer),
            vec(RWKV_WIDTH), vec(RWKV_WIDTH), vec(RWKV_WIDTH), vec(RWKV_WIDTH), vec(RWKV_WIDTH),
        ],
        out_specs=pl.BlockSpec((None, c_len, RWKV_WIDTH), lambda i, c: (i, c, 0)),
        out_shape=jax.ShapeDtypeStruct((b, t, RWKV_WIDTH), BF16),
        scratch_shapes=[pltpu.VMEM((8, pw), F32), pltpu.VMEM((RWKV_WIDTH, RWKV_WIDTH), F32)],
        compiler_params=_params("arbitrary", "arbitrary"),
        name="rwkv7_chunked",
    )(p_rwkv, mu, w0, ww2_pad, a0, wa2_pad, wg2, k_k, k_a, r_k, gn_w, gn_b)


def _mixout_kernel(mod_ref, h_ref, ya_ref, yr_ref, yc_ref, w_ref, o_ref):
    wa = ATT_WIDTH
    wr = wa + RWKV_WIDTH
    mixed = (_dot(ya_ref[...], w_ref[0:wa, :]) + _dot(yr_ref[...], w_ref[wa:wr, :])
             + _dot(yc_ref[...], w_ref[wr:, :]))
    o_ref[...] = h_ref[...] + (1.0 + mod_ref[2:3, :]) * mixed


def _mixout(h, mod3, y_att, y_rwkv, y_conv, w_out, layer):
    b, t, d = h.shape
    tm = min(TOKEN_TILE, t)
    tok = lambda width: pl.BlockSpec((None, tm, width), lambda i, j: (i, j, 0))
    return pl.pallas_call(
        _mixout_kernel,
        grid=(b, t // tm),
        in_specs=[
            pl.BlockSpec((None, 3, d), lambda i, j: (i, 0, 0)),
            tok(d), tok(ATT_WIDTH), tok(RWKV_WIDTH), tok(CONV_WIDTH),
            _layer_spec((d, d), layer),
        ],
        out_specs=tok(d),
        out_shape=jax.ShapeDtypeStruct((b, t, d), F32),
        compiler_params=_params("arbitrary", "arbitrary"),
        name="mix_out_proj",
    )(mod3, h, y_att, y_rwkv, y_conv, w_out)


def kernel(x, c, w_ada, b_ada, g_ffn1, w_ffn1_in, w_ffn1_out, g_mix, w_mix_in, w_mix_out, att_q_gain, att_k_gain, att_sinks, rwkv_mu, rwkv_w0, rwkv_w_w2, rwkv_a0, rwkv_a_w2, rwkv_g_w2, rwkv_k_k, rwkv_k_a, rwkv_r_k, rwkv_gn_w, rwkv_gn_b, conv_w, g_ffn2, w_ffn2_in, w_ffn2_out):
    n_layers, d = g_ffn1.shape
    bsz = x.shape[0]
    row3 = lambda z: z.astype(F32).reshape(n_layers, 1, -1)
    bf = lambda z: z.astype(BF16)

    mod = _modulation(c, w_ada, b_ada).reshape(n_layers, bsz, N_MOD // 3, 3, d)

    a_end = ATT_PROJ_WIDTH
    r_end = a_end + RWKV_PROJ_WIDTH
    w_att, w_rwkv, w_conv = bf(w_mix_in[:, :, :a_end]), bf(w_mix_in[:, :, a_end:r_end]), bf(w_mix_in[:, :, r_end:])
    w_mix_out_b = bf(w_mix_out)
    w1_in, w1_out, w2_in, w2_out = bf(w_ffn1_in), bf(w_ffn1_out), bf(w_ffn2_in), bf(w_ffn2_out)
    half = RWKV_LORA // 2
    zeros = jnp.zeros((n_layers, half, RWKV_WIDTH), F32)
    ww2_pad = bf(jnp.concatenate([rwkv_w_w2, zeros], axis=1))
    wa2_pad = bf(jnp.concatenate([zeros, rwkv_a_w2], axis=1))
    wg2 = bf(rwkv_g_w2)
    g1, gm, g2 = row3(g_ffn1), row3(g_mix), row3(g_ffn2)
    mu, w0, a0 = row3(rwkv_mu), row3(rwkv_w0), row3(rwkv_a0)
    k_k, k_a, r_k = row3(rwkv_k_k), row3(rwkv_k_a), row3(rwkv_r_k)
    gn_w, gn_b = row3(rwkv_gn_w), row3(rwkv_gn_b)

    h = x
    for l in range(n_layers):
        h = _ffn(h, mod[l, :, 0], g1, w1_in, w1_out, l)
        p_att, p_rwkv, y_conv = _mixin(h, mod[l, :, 1], gm, w_att, w_rwkv, w_conv, conv_w, l)
        y_att = _attention(p_att, att_q_gain[l], att_k_gain[l], att_sinks[l])
        y_rwkv = _rwkv(p_rwkv, mu, w0, ww2_pad, a0, wa2_pad, wg2, k_k, k_a, r_k, gn_w, gn_b, l)
        h = _mixout(h, mod[l, :, 1], y_att, y_rwkv, y_conv, w_mix_out_b, l)
        h = _ffn(h, mod[l, :, 2], g2, w2_in, w2_out, l)
    return h
```

```python
import functools
import math

import jax
import jax.numpy as jnp
from jax import lax
from jax.experimental import pallas as pl
from jax.experimental.pallas import tpu as pltpu

F32 = jnp.float32
BF16 = jnp.bfloat16

HEAD_DIM = 64
ATT_Q_HEADS = 8
ATT_KV_HEADS = 2
ATT_WIDTH = ATT_Q_HEADS * HEAD_DIM
ATT_KV_WIDTH = ATT_KV_HEADS * HEAD_DIM
ATT_PROJ_WIDTH = ATT_WIDTH + 2 * ATT_KV_WIDTH
ATT_BLOCK = 128
RWKV_HEADS = 4
RWKV_WIDTH = RWKV_HEADS * HEAD_DIM
RWKV_LORA = 128
RWKV_PROJ_WIDTH = 3 * RWKV_WIDTH + 2 * RWKV_LORA
RWKV_GN_EPS = 64e-5
RWKV_CHUNK = 64
RWKV_TILE = 256
CONV_WIDTH = 256
CONV_K = 3
N_MOD = 9
EPS = 1e-6
NEG_BIG = -1e30
EXP_M05 = math.exp(-0.5)

TOKEN_TILE = 512
VMEM_LIMIT = 56 * 1024 * 1024


def _dot(a, b):
    return jnp.dot(a, b, preferred_element_type=F32)


def _dot_nt(a, b):
    return lax.dot_general(a, b, (((1,), (1,)), ((), ())), preferred_element_type=F32)


def _dot_tn(a, b):
    return lax.dot_general(a, b, (((0,), (0,)), ((), ())), preferred_element_type=F32)


def _split3(x):
    hi = x.astype(BF16)
    r1 = x - hi.astype(F32)
    mid = r1.astype(BF16)
    lo = (r1 - mid.astype(F32)).astype(BF16)
    return hi, mid, lo


def _dot_exact_rhs(x, m):
    hi, mid, lo = _split3(x)
    return _dot(hi, m) + _dot(mid, m) + _dot(lo, m)


def _dot_exact_lhs(m, x):
    hi, mid, lo = _split3(x)
    return _dot(m, hi) + _dot(m, mid) + _dot(m, lo)


def _block_ones(n, blk):
    r = lax.broadcasted_iota(jnp.int32, (n, n), 0) // blk
    c = lax.broadcasted_iota(jnp.int32, (n, n), 1) // blk
    return jnp.where(r == c, 1.0, 0.0).astype(BF16)


def _const_spec(shape):
    nd = len(shape)
    return pl.BlockSpec(shape, lambda *_: (0,) * nd, pipeline_mode=pl.Buffered(1))


def _layer_spec(shape, layer):
    nd = len(shape)
    return pl.BlockSpec((None,) + tuple(shape), lambda *_: (layer,) + (0,) * nd,
                        pipeline_mode=pl.Buffered(1))


def _params(*sem):
    return pltpu.CompilerParams(dimension_semantics=sem, vmem_limit_bytes=VMEM_LIMIT)


def _modulated_norm(x, gain, shift, scale):
    ms = jnp.mean(x * x, axis=-1, keepdims=True)
    return (x * lax.rsqrt(ms + EPS) * gain) * (1.0 + scale) + shift


def _mod_kernel(c_ref, w_ref, b_ref, o_ref):
    c = c_ref[...]
    act = (c * jax.nn.sigmoid(c)).astype(BF16)
    o_ref[...] = _dot(act, w_ref[...].astype(BF16)) + b_ref[...]


def _modulation(c, w_ada, b_ada):
    n_layers, d, n = w_ada.shape
    b = c.shape[0]
    tn = d
    return pl.pallas_call(
        _mod_kernel,
        grid=(n_layers, n // tn),
        in_specs=[
            pl.BlockSpec((b, d), lambda l, j: (0, 0)),
            pl.BlockSpec((None, d, tn), lambda l, j: (l, 0, j)),
            pl.BlockSpec((None, 1, tn), lambda l, j: (l, 0, j)),
        ],
        out_specs=pl.BlockSpec((None, b, tn), lambda l, j: (l, 0, j)),
        out_shape=jax.ShapeDtypeStruct((n_layers, b, n), F32),
        compiler_params=_params("arbitrary", "arbitrary"),
        name="adaln_mod",
    )(c, w_ada, b_ada.reshape(n_layers, 1, n))


def _ffn_kernel(mod_ref, g_ref, x_ref, wg_ref, wu_ref, wo_ref, o_ref, *, n_chunks):
    x = x_ref[...]
    hn = _modulated_norm(x, g_ref[...], mod_ref[0:1, :], mod_ref[1:2, :]).astype(BF16)
    d_ff = wg_ref.shape[1]
    tf = d_ff // n_chunks
    acc = None
    for j in range(n_chunks):
        sl = slice(j * tf, (j + 1) * tf)
        gate = _dot(hn, wg_ref[:, sl])
        up = _dot(hn, wu_ref[:, sl])
        act = (gate * jax.nn.sigmoid(gate) * up).astype(BF16)
        part = _dot(act, wo_ref[sl, :])
        acc = part if acc is None else acc + part
    o_ref[...] = x + (0.5 * (1.0 + mod_ref[2:3, :])) * acc


def _ffn(h, mod3, gain, w_in, w_out, layer):
    b, t, d = h.shape
    d_ff = w_out.shape[1]
    tm = min(TOKEN_TILE, t)
    n_chunks = 2 if d_ff % 256 == 0 else 1
    kern = functools.partial(_ffn_kernel, n_chunks=n_chunks)
    w_gate_spec = pl.BlockSpec((None, d, d_ff), lambda i, j: (layer, 0, 0), pipeline_mode=pl.Buffered(1))
    w_up_spec = pl.BlockSpec((None, d, d_ff), lambda i, j: (layer, 0, 1), pipeline_mode=pl.Buffered(1))
    return pl.pallas_call(
        kern,
        grid=(b, t // tm),
        in_specs=[
            pl.BlockSpec((None, 3, d), lambda i, j: (i, 0, 0)),
            _layer_spec((1, d), layer),
            pl.BlockSpec((None, tm, d), lambda i, j: (i, j, 0)),
            w_gate_spec,
            w_up_spec,
            _layer_spec((d_ff, d), layer),
        ],
        out_specs=pl.BlockSpec((None, tm, d), lambda i, j: (i, j, 0)),
        out_shape=jax.ShapeDtypeStruct((b, t, d), F32),
        compiler_params=_params("arbitrary", "arbitrary"),
        name="macaron_ffn",
    )(mod3, gain, h, w_in, w_in, w_out)


def _mixin_kernel(mod_ref, g_ref, x_ref, watt_ref, wrwkv_ref, wconv_ref, cw_ref,
                  patt_ref, prwkv_ref, yconv_ref, carry_ref):
    @pl.when(pl.program_id(1) == 0)
    def _():
        carry_ref[...] = jnp.zeros_like(carry_ref)

    x = x_ref[...]
    hn = _modulated_norm(x, g_ref[...], mod_ref[0:1, :], mod_ref[1:2, :]).astype(BF16)
    patt_ref[...] = _dot(hn, watt_ref[...]).astype(patt_ref.dtype)
    prwkv_ref[...] = _dot(hn, wrwkv_ref[...])
    pc = _dot(hn, wconv_ref[...])
    cwid = yconv_ref.shape[-1]
    b_gate = pc[:, 0:cwid]
    u = pc[:, cwid:2 * cwid] * pc[:, 2 * cwid:3 * cwid]
    tm = u.shape[0]
    row = lax.broadcasted_iota(jnp.int32, (tm, 1), 0)
    prev1 = carry_ref[1:2, :]
    prev2 = carry_ref[0:1, :]
    u1 = jnp.where(row == 0, prev1, pltpu.roll(u, 1, 0))
    u2 = jnp.where(row == 0, prev2, jnp.where(row == 1, prev1, pltpu.roll(u, 2, 0)))
    y = cw_ref[0:1, :] * u2 + cw_ref[1:2, :] * u1 + cw_ref[2:3, :] * u
    yconv_ref[...] = (b_gate * y).astype(yconv_ref.dtype)
    carry_ref[0:2, :] = u[tm - 2:tm, :]


def _mixin(h, mod3, gain, w_att, w_rwkv, w_conv, conv_w, layer):
    b, t, d = h.shape
    tm = min(TOKEN_TILE, t)
    tok = lambda width: pl.BlockSpec((None, tm, width), lambda i, j: (i, j, 0))
    return pl.pallas_call(
        _mixin_kernel,
        grid=(b, t // tm),
        in_specs=[
            pl.BlockSpec((None, 3, d), lambda i, j: (i, 0, 0)),
            _layer_spec((1, d), layer),
            tok(d),
            _layer_spec((d, ATT_PROJ_WIDTH), layer),
            _layer_spec((d, RWKV_PROJ_WIDTH), layer),
            _layer_spec((d, 3 * CONV_WIDTH), layer),
            _layer_spec((CONV_K, CONV_WIDTH), layer),
        ],
        out_specs=[tok(ATT_PROJ_WIDTH), tok(RWKV_PROJ_WIDTH), tok(CONV_WIDTH)],
        out_shape=[
            jax.ShapeDtypeStruct((b, t, ATT_PROJ_WIDTH), BF16),
            jax.ShapeDtypeStruct((b, t, RWKV_PROJ_WIDTH), F32),
            jax.ShapeDtypeStruct((b, t, CONV_WIDTH), BF16),
        ],
        scratch_shapes=[pltpu.VMEM((8, CONV_WIDTH), F32)],
        compiler_params=_params("arbitrary", "arbitrary"),
        name="mix_in_proj",
    )(mod3, gain, h, w_att, w_rwkv, w_conv, conv_w)


def _head_rms(x, ones_blk, gain):
    ss = _dot((x * x).astype(BF16), ones_blk)
    return x * lax.rsqrt(ss * (1.0 / HEAD_DIM) + EPS) * gain


def _attn_kernel(sink_ref, qg_ref, kg_ref, q_ref, kvc_ref, kvp_ref, o_ref):
    n = pl.program_id(1)
    blk = ATT_BLOCK
    lane128 = lax.broadcasted_iota(jnp.int32, (1, 2 * HEAD_DIM), 1)
    low = lane128 < HEAD_DIM

    q = q_ref[...].astype(F32)
    qn = _head_rms(q, _block_ones(ATT_WIDTH, HEAD_DIM), qg_ref[...]).astype(BF16)
    kv = jnp.concatenate([kvp_ref[...], kvc_ref[...]], axis=0).astype(F32)
    k = kv[:, 0:ATT_KV_WIDTH]
    v = kv[:, ATT_KV_WIDTH:2 * ATT_KV_WIDTH]
    kn = _head_rms(k, _block_ones(ATT_KV_WIDTH, HEAD_DIM), kg_ref[...])
    kr = pltpu.roll(kn, HEAD_DIM, 1)
    vr = pltpu.roll(v, HEAD_DIM, 1)
    zero = jnp.zeros_like(kn)
    k_low = [jnp.where(low, kn, zero), jnp.where(low, kr, zero)]
    k_high = [jnp.where(low, zero, kr), jnp.where(low, zero, kn)]
    v_low = [jnp.where(low, v, zero), jnp.where(low, vr, zero)]
    v_high = [jnp.where(low, zero, vr), jnp.where(low, zero, v)]

    rows = 2 * blk
    ri = lax.broadcasted_iota(jnp.int32, (rows, 4 * blk), 0) & (blk - 1)
    cj = lax.broadcasted_iota(jnp.int32, (rows, 4 * blk), 1) & (2 * blk - 1)
    visible = (cj > ri) & (cj <= ri + blk) & ((cj >= blk) | (n > 0))
    top = lax.broadcasted_iota(jnp.int32, (rows, 1), 0) < blk
    half_r = lax.broadcasted_iota(jnp.int32, (4 * blk, 2 * HEAD_DIM), 0) // (2 * blk)
    half_c = lax.broadcasted_iota(jnp.int32, (4 * blk, 2 * HEAD_DIM), 1) // HEAD_DIM
    den_sel = jnp.where(half_r == half_c, 1.0, 0.0).astype(BF16)

    for g in range(ATT_KV_HEADS):
        qg = jnp.concatenate([qn[:, 256 * g:256 * g + 128], qn[:, 256 * g + 128:256 * g + 256]], axis=0)
        kcat = jnp.concatenate([k_low[g], k_high[g]], axis=0).astype(BF16)
        vcat = jnp.concatenate([v_low[g], v_high[g]], axis=0).astype(BF16)
        s = _dot_nt(qg, kcat) * (HEAD_DIM ** -0.5)
        s = jnp.where(visible, s, NEG_BIG)
        h0 = 4 * g
        sink_e = jnp.where(top, sink_ref[h0], sink_ref[h0 + 2])
        sink_o = jnp.where(top, sink_ref[h0 + 1], sink_ref[h0 + 3])
        s_e = s[:, 0:2 * blk]
        s_o = s[:, 2 * blk:4 * blk]
        m_e = jnp.maximum(jnp.max(s_e, axis=-1, keepdims=True), sink_e)
        m_o = jnp.maximum(jnp.max(s_o, axis=-1, keepdims=True), sink_o)
        p = jnp.concatenate([jnp.exp(s_e - m_e), jnp.exp(s_o - m_o)], axis=1).astype(BF16)
        num = _dot(p, vcat)
        den = _dot(p, den_sel) + jnp.where(low, jnp.exp(sink_e - m_e), jnp.exp(sink_o - m_o))
        out = (num / den).astype(o_ref.dtype)
        o_ref[:, 256 * g:256 * g + 128] = out[0:blk]
        o_ref[:, 256 * g + 128:256 * g + 256] = out[blk:rows]


def _attention(p_att, q_gain, k_gain, sinks):
    b, t, _ = p_att.shape
    blk = ATT_BLOCK
    scale_gain = lambda g, reps: jnp.tile(g.astype(F32), reps).reshape(1, reps * HEAD_DIM)
    return pl.pallas_call(
        _attn_kernel,
        grid=(b, t // blk),
        in_specs=[
            pl.BlockSpec(memory_space=pltpu.SMEM),
            _const_spec((1, ATT_WIDTH)),
            _const_spec((1, ATT_KV_WIDTH)),
            pl.BlockSpec((None, blk, ATT_WIDTH), lambda i, n: (i, n, 0)),
            pl.BlockSpec((None, blk, 2 * ATT_KV_WIDTH), lambda i, n: (i, n, ATT_WIDTH // (2 * ATT_KV_WIDTH))),
            pl.BlockSpec((None, blk, 2 * ATT_KV_WIDTH),
                         lambda i, n: (i, jnp.maximum(n - 1, 0), ATT_WIDTH // (2 * ATT_KV_WIDTH))),
        ],
        out_specs=pl.BlockSpec((None, blk, ATT_WIDTH), lambda i, n: (i, n, 0)),
        out_shape=jax.ShapeDtypeStruct((b, t, ATT_WIDTH), BF16),
        compiler_params=_params("arbitrary", "arbitrary"),
        name="swa_sink_attention",
    )(sinks.astype(F32), scale_gain(q_gain, ATT_Q_HEADS), scale_gain(k_gain, ATT_KV_HEADS),
      p_att, p_att, p_att)


def _rwkv_kernel(p_ref, mu_ref, w0_ref, ww2_ref, a0_ref, wa2_ref, wg2_ref, kk_ref, ka_ref, rk_ref,
                 gnw_ref, gnb_ref, o_ref, prev_ref, state_ref, *, c_len):
    tt = p_ref.shape[0]
    width = RWKV_WIDTH
    n_heads = RWKV_HEADS
    n_ch = tt // c_len
    bf = lambda z: z.astype(BF16)
    each = lambda f, *cols: [f(*args) for args in zip(*cols)]

    @pl.when(pl.program_id(1) == 0)
    def _():
        prev_ref[...] = jnp.zeros_like(prev_ref)
        state_ref[...] = jnp.zeros_like(state_ref)

    p = p_ref[...]
    trow = lax.broadcasted_iota(jnp.int32, (tt, 1), 0)
    p_prev = jnp.where(trow == 0, prev_ref[0:1, :], pltpu.roll(p, 1, 0))
    prev_ref[0:1, :] = p[tt - 1:tt, :]
    xs = p + mu_ref[...] * (p_prev - p)
    r = xs[:, 0:width]
    k = xs[:, width:2 * width]
    v = xs[:, 2 * width:3 * width]
    lora = xs[:, 3 * width:3 * width + RWKV_LORA]
    gate_in = xs[:, 3 * width + RWKV_LORA:]

    dw = _dot(bf(jnp.tanh(lora)), ww2_ref[...])
    da = _dot(bf(lora), wa2_ref[...])
    g = _dot(bf(jax.nn.sigmoid(gate_in)), wg2_ref[...])
    lw = -EXP_M05 * jax.nn.sigmoid(w0_ref[...] + dw)
    a = jax.nn.sigmoid(a0_ref[...] + da)

    ones_head = _block_ones(width, HEAD_DIM)
    ones_head2 = jnp.concatenate([ones_head, ones_head], axis=0)

    def head_sum(z):
        hi = bf(z)
        lo = bf(z - hi.astype(F32))
        return _dot(jnp.concatenate([hi, lo], axis=1), ones_head2)

    kk_raw = k * kk_ref[...]
    kk = kk_raw * lax.rsqrt(jnp.maximum(head_sum(kk_raw * kk_raw), 1e-24))
    kmod = k * (1.0 + (a - 1.0) * ka_ref[...])
    b = kk * a

    tri_r = lax.broadcasted_iota(jnp.int32, (tt, tt), 0)
    tri_c = lax.broadcasted_iota(jnp.int32, (tt, tt), 1)
    tri = jnp.where((tri_c <= tri_r) & (tri_c // c_len == tri_r // c_len), 1.0, 0.0).astype(BF16)
    lw_hi, lw_mid, lw_lo = _split3(lw)
    lc = _dot(jnp.concatenate([tri, tri, tri], axis=1), jnp.concatenate([lw_hi, lw_mid, lw_lo], axis=0))

    at = kk * jnp.exp(lc - lw)
    rt = r * jnp.exp(lc)
    w_inv = jnp.exp(-lc)
    bt = b * w_inv
    kt = kmod * w_inv

    lane_head = lax.broadcasted_iota(jnp.int32, (1, width), 1) // HEAD_DIM

    def stack(z):
        zero = jnp.zeros_like(z)
        return jnp.concatenate([jnp.where(lane_head == h, z, zero) for h in range(n_heads)], axis=0)

    mi = lax.broadcasted_iota(jnp.int32, (c_len, width), 0)
    mj = lax.broadcasted_iota(jnp.int32, (c_len, width), 1) & (c_len - 1)
    strict = mj < mi
    incl = mj <= mi
    eye = mj == mi
    eye_b = jnp.where(eye, 1.0, 0.0).astype(BF16)

    sl = [slice(j * c_len, (j + 1) * c_len) for j in range(n_ch)]
    cut = lambda z: [z[s] for s in sl]
    lc_c = cut(lc)
    ltot = [z[c_len - 1:c_len, :] for z in lc_c]
    w_end = each(lambda lt, lcj: jnp.exp(lt - lcj), ltot, lc_c)
    at_b, rt_c, v_b = cut(bf(at)), cut(rt), cut(bf(v))
    rt_b = each(bf, rt_c)
    bt_bd = each(stack, cut(bf(bt)))
    kt_bd = each(stack, cut(bf(kt)))
    v_bd = each(stack, v_b)
    at_bd = each(stack, at_b)
    bh_bd = each(lambda z, w: stack(bf(z * w)), cut(b), w_end)
    kh_bd = each(lambda z, w: stack(bf(z * w)), cut(kmod), w_end)

    a_ab = each(lambda x, y: jnp.where(strict, _dot_nt(x, y), 0.0), at_b, bt_bd)
    a_ak = each(lambda x, y: bf(jnp.where(strict, _dot_nt(x, y), 0.0)), at_b, kt_bd)
    a_rb = each(lambda x, y: bf(jnp.where(incl, _dot_nt(x, y), 0.0)), rt_b, bt_bd)
    a_rk = each(lambda x, y: bf(jnp.where(incl, _dot_nt(x, y), 0.0)), rt_b, kt_bd)

    t_inv = each(lambda z: jnp.where(eye, 1.0, 0.0) - z, a_ab)
    pw = each(bf, a_ab)
    pw_bd = each(stack, pw)
    for _ in range(int(math.log2(c_len)) - 1):
        pw = each(lambda x, y: bf(_dot(x, y)), pw, pw_bd)
        pw_bd = each(stack, pw)
        t_inv = each(lambda t, y: t + _dot(bf(t), y), t_inv, pw_bd)
    t_b = each(bf, t_inv)

    y1_bd = each(lambda x, y: stack(bf(_dot(x, y))), a_ak, v_bd)
    at2_bd = each(lambda x, y: stack(bf(_dot(x, y))), t_b, at_bd)
    u2_bd = each(lambda x, y: stack(bf(_dot(x, y))), t_b, y1_bd)
    r2 = each(lambda z, x, y: bf(z - _dot(x, y)), rt_c, a_rb, at2_bd)
    o2 = each(lambda x, y, z, w: _dot(x, y) - _dot(z, w), a_rk, v_bd, a_rb, u2_bd)
    bh_t = each(lambda y: bf(_dot_nt(eye_b, y)), bh_bd)
    kh_t = each(lambda y: bf(_dot_nt(eye_b, y)), kh_bd)
    p_t = each(lambda lt, x, y: bf(jnp.where(eye, jnp.exp(lt), 0.0) - _dot(x, y)), ltot, bh_t, at2_bd)
    q_t = each(lambda x, y, z, w: _dot(x, y) - _dot(z, w), kh_t, v_bd, bh_t, u2_bd)

    state = state_ref[...]
    ys = []
    for j in range(n_ch):
        s_bd = stack(bf(state))
        ys.append(_dot(r2[j], s_bd) + o2[j])
        state = _dot(p_t[j], s_bd) + q_t[j]
    state_ref[...] = state
    y = jnp.concatenate(ys, axis=0) if n_ch > 1 else ys[0]

    mean = head_sum(y) * (1.0 / HEAD_DIM)
    dev = y - mean
    var = head_sum(dev * dev) * (1.0 / HEAD_DIM)
    yn = dev * lax.rsqrt(var + RWKV_GN_EPS) * gnw_ref[...] + gnb_ref[...]
    bonus = head_sum(r * kmod * rk_ref[...]) * v
    o_ref[...] = ((yn + bonus) * g).astype(o_ref.dtype)


def _rwkv(p_rwkv, mu, w0, ww2_pad, a0, wa2_pad, wg2, k_k, k_a, r_k, gn_w, gn_b, layer):
    b, t, pw = p_rwkv.shape
    tt = min(RWKV_TILE, t)
    c_len = min(RWKV_CHUNK, tt)
    vec = lambda n: _layer_spec((1, n), layer)
    return pl.pallas_call(
        functools.partial(_rwkv_kernel, c_len=c_len),
        grid=(b, t // tt),
        in_specs=[
            pl.BlockSpec((None, tt, pw), lambda i, c: (i, c, 0)),
            vec(pw), vec(RWKV_WIDTH),
            _layer_spec((RWKV_LORA, RWKV_WIDTH), layer),
            vec(RWKV_WIDTH),
            _layer_spec((RWKV_LORA, RWKV_WIDTH), layer),
            _layer_spec((RWKV_LORA, RWKV_WIDTH), layer),
            vec(RWKV_WIDTH), vec(RWKV_WIDTH), vec(RWKV_WIDTH), vec(RWKV_WIDTH), vec(RWKV_WIDTH),
        ],
        out_specs=pl.BlockSpec((None, tt, RWKV_WIDTH), lambda i, c: (i, c, 0)),
        out_shape=jax.ShapeDtypeStruct((b, t, RWKV_WIDTH), BF16),
        scratch_shapes=[pltpu.VMEM((8, pw), F32), pltpu.VMEM((HEAD_DIM, RWKV_WIDTH), F32)],
        compiler_params=_params("arbitrary", "arbitrary"),
        name="rwkv7_chunked",
    )(p_rwkv, mu, w0, ww2_pad, a0, wa2_pad, wg2, k_k, k_a, r_k, gn_w, gn_b)


def _mixout_kernel(mod_ref, h_ref, ya_ref, yr_ref, yc_ref, w_ref, o_ref):
    wa = ATT_WIDTH
    wr = wa + RWKV_WIDTH
    mixed = (_dot(ya_ref[...], w_ref[0:wa, :]) + _dot(yr_ref[...], w_ref[wa:wr, :])
             + _dot(yc_ref[...], w_ref[wr:, :]))
    o_ref[...] = h_ref[...] + (1.0 + mod_ref[2:3, :]) * mixed


def _mixout(h, mod3, y_att, y_rwkv, y_conv, w_out, layer):
    b, t, d = h.shape
    tm = min(TOKEN_TILE, t)
    tok = lambda width: pl.BlockSpec((None, tm, width), lambda i, j: (i, j, 0))
    return pl.pallas_call(
        _mixout_kernel,
        grid=(b, t // tm),
        in_specs=[
            pl.BlockSpec((None, 3, d), lambda i, j: (i, 0, 0)),
            tok(d), tok(ATT_WIDTH), tok(RWKV_WIDTH), tok(CONV_WIDTH),
            _layer_spec((d, d), layer),
        ],
        out_specs=tok(d),
        out_shape=jax.ShapeDtypeStruct((b, t, d), F32),
        compiler_params=_params("arbitrary", "arbitrary"),
        name="mix_out_proj",
    )(mod3, h, y_att, y_rwkv, y_conv, w_out)


def kernel(x, c, w_ada, b_ada, g_ffn1, w_ffn1_in, w_ffn1_out, g_mix, w_mix_in, w_mix_out, att_q_gain, att_k_gain, att_sinks, rwkv_mu, rwkv_w0, rwkv_w_w2, rwkv_a0, rwkv_a_w2, rwkv_g_w2, rwkv_k_k, rwkv_k_a, rwkv_r_k, rwkv_gn_w, rwkv_gn_b, conv_w, g_ffn2, w_ffn2_in, w_ffn2_out):
    n_layers, d = g_ffn1.shape
    bsz = x.shape[0]
    row3 = lambda z: z.astype(F32).reshape(n_layers, 1, -1)
    bf = lambda z: z.astype(BF16)

    mod = _modulation(c, w_ada, b_ada).reshape(n_layers, bsz, N_MOD // 3, 3, d)

    a_end = ATT_PROJ_WIDTH
    r_end = a_end + RWKV_PROJ_WIDTH
    w_att, w_rwkv, w_conv = bf(w_mix_in[:, :, :a_end]), bf(w_mix_in[:, :, a_end:r_end]), bf(w_mix_in[:, :, r_end:])
    w_mix_out_b = bf(w_mix_out)
    w1_in, w1_out, w2_in, w2_out = bf(w_ffn1_in), bf(w_ffn1_out), bf(w_ffn2_in), bf(w_ffn2_out)
    half = RWKV_LORA // 2
    zeros = jnp.zeros((n_layers, half, RWKV_WIDTH), F32)
    ww2_pad = bf(jnp.concatenate([rwkv_w_w2, zeros], axis=1))
    wa2_pad = bf(jnp.concatenate([zeros, rwkv_a_w2], axis=1))
    wg2 = bf(rwkv_g_w2)
    g1, gm, g2 = row3(g_ffn1), row3(g_mix), row3(g_ffn2)
    mu, w0, a0 = row3(rwkv_mu), row3(rwkv_w0), row3(rwkv_a0)
    k_k, k_a, r_k = row3(rwkv_k_k), row3(rwkv_k_a), row3(rwkv_r_k)
    gn_w, gn_b = row3(rwkv_gn_w), row3(rwkv_gn_b)

    h = x
    for l in range(n_layers):
        h = _ffn(h, mod[l, :, 0], g1, w1_in, w1_out, l)
        p_att, p_rwkv, y_conv = _mixin(h, mod[l, :, 1], gm, w_att, w_rwkv, w_conv, conv_w, l)
        y_att = _attention(p_att, att_q_gain[l], att_k_gain[l], att_sinks[l])
        y_rwkv = _rwkv(p_rwkv, mu, w0, ww2_pad, a0, wa2_pad, wg2, k_k, k_a, r_k, gn_w, gn_b, l)
        h = _mixout(h, mod[l, :, 1], y_att, y_rwkv, y_conv, w_mix_out_b, l)
        h = _ffn(h, mod[l, :, 2], g2, w2_in, w2_out, l)
    return h
```

```python
import functools
import math

import jax
import jax.numpy as jnp
from jax import lax
from jax.experimental import pallas as pl
from jax.experimental.pallas import tpu as pltpu

F32 = jnp.float32
BF16 = jnp.bfloat16

HEAD_DIM = 64
ATT_Q_HEADS = 8
ATT_KV_HEADS = 2
ATT_WIDTH = ATT_Q_HEADS * HEAD_DIM
ATT_KV_WIDTH = ATT_KV_HEADS * HEAD_DIM
ATT_PROJ_WIDTH = ATT_WIDTH + 2 * ATT_KV_WIDTH
ATT_BLOCK = 128
ATT_TILE = 512
RWKV_HEADS = 4
RWKV_WIDTH = RWKV_HEADS * HEAD_DIM
RWKV_LORA = 128
RWKV_PROJ_WIDTH = 3 * RWKV_WIDTH + 2 * RWKV_LORA
RWKV_GN_EPS = 64e-5
RWKV_CHUNK = 64
RWKV_TILE = 256
CONV_WIDTH = 256
CONV_K = 3
N_MOD = 9
EPS = 1e-6
NEG_BIG = -1e30
EXP_M05 = math.exp(-0.5)

TOKEN_TILE = 512
VMEM_LIMIT = 56 * 1024 * 1024


def _dot(a, b):
    return jnp.dot(a, b, preferred_element_type=F32)


def _dot_nt(a, b):
    return lax.dot_general(a, b, (((1,), (1,)), ((), ())), preferred_element_type=F32)


def _dot_tn(a, b):
    return lax.dot_general(a, b, (((0,), (0,)), ((), ())), preferred_element_type=F32)


def _split3(x):
    hi = x.astype(BF16)
    r1 = x - hi.astype(F32)
    mid = r1.astype(BF16)
    lo = (r1 - mid.astype(F32)).astype(BF16)
    return hi, mid, lo


def _dot_exact_rhs(x, m):
    hi, mid, lo = _split3(x)
    return _dot(hi, m) + _dot(mid, m) + _dot(lo, m)


def _dot_exact_lhs(m, x):
    hi, mid, lo = _split3(x)
    return _dot(m, hi) + _dot(m, mid) + _dot(m, lo)


def _block_ones(n, blk):
    r = lax.broadcasted_iota(jnp.int32, (n, n), 0) // blk
    c = lax.broadcasted_iota(jnp.int32, (n, n), 1) // blk
    return jnp.where(r == c, 1.0, 0.0).astype(BF16)


def _const_spec(shape):
    nd = len(shape)
    return pl.BlockSpec(shape, lambda *_: (0,) * nd, pipeline_mode=pl.Buffered(1))


def _layer_spec(shape, layer):
    nd = len(shape)
    return pl.BlockSpec((None,) + tuple(shape), lambda *_: (layer,) + (0,) * nd,
                        pipeline_mode=pl.Buffered(1))


def _params(*sem):
    return pltpu.CompilerParams(dimension_semantics=sem, vmem_limit_bytes=VMEM_LIMIT)


def _modulated_norm(x, gain, shift, scale):
    ms = jnp.mean(x * x, axis=-1, keepdims=True)
    return (x * lax.rsqrt(ms + EPS) * gain) * (1.0 + scale) + shift


def _mod_kernel(c_ref, w_ref, b_ref, o_ref):
    c = c_ref[...]
    act = (c * jax.nn.sigmoid(c)).astype(BF16)
    o_ref[...] = _dot(act, w_ref[...].astype(BF16)) + b_ref[...]


def _modulation(c, w_ada, b_ada):
    n_layers, d, n = w_ada.shape
    b = c.shape[0]
    tn = d
    return pl.pallas_call(
        _mod_kernel,
        grid=(n_layers, n // tn),
        in_specs=[
            pl.BlockSpec((b, d), lambda l, j: (0, 0)),
            pl.BlockSpec((None, d, tn), lambda l, j: (l, 0, j)),
            pl.BlockSpec((None, 1, tn), lambda l, j: (l, 0, j)),
        ],
        out_specs=pl.BlockSpec((None, b, tn), lambda l, j: (l, 0, j)),
        out_shape=jax.ShapeDtypeStruct((n_layers, b, n), F32),
        compiler_params=_params("arbitrary", "arbitrary"),
        name="adaln_mod",
    )(c, w_ada, b_ada.reshape(n_layers, 1, n))


def _ffn_kernel(mod_ref, g_ref, x_ref, wg_ref, wu_ref, wo_ref, o_ref, *, n_chunks):
    x = x_ref[...]
    hn = _modulated_norm(x, g_ref[...], mod_ref[0:1, :], mod_ref[1:2, :]).astype(BF16)
    d_ff = wg_ref.shape[1]
    tf = d_ff // n_chunks
    acc = None
    for j in range(n_chunks):
        sl = slice(j * tf, (j + 1) * tf)
        gate = _dot(hn, wg_ref[:, sl])
        up = _dot(hn, wu_ref[:, sl])
        act = (gate * jax.nn.sigmoid(gate) * up).astype(BF16)
        part = _dot(act, wo_ref[sl, :])
        acc = part if acc is None else acc + part
    o_ref[...] = x + (0.5 * (1.0 + mod_ref[2:3, :])) * acc


def _ffn(h, mod3, gain, w_in, w_out, layer):
    b, t, d = h.shape
    d_ff = w_out.shape[1]
    tm = min(TOKEN_TILE, t)
    n_chunks = 2 if d_ff % 256 == 0 else 1
    kern = functools.partial(_ffn_kernel, n_chunks=n_chunks)
    w_gate_spec = pl.BlockSpec((None, d, d_ff), lambda i, j: (layer, 0, 0), pipeline_mode=pl.Buffered(1))
    w_up_spec = pl.BlockSpec((None, d, d_ff), lambda i, j: (layer, 0, 1), pipeline_mode=pl.Buffered(1))
    return pl.pallas_call(
        kern,
        grid=(b, t // tm),
        in_specs=[
            pl.BlockSpec((None, 3, d), lambda i, j: (i, 0, 0)),
            _layer_spec((1, d), layer),
            pl.BlockSpec((None, tm, d), lambda i, j: (i, j, 0)),
            w_gate_spec,
            w_up_spec,
            _layer_spec((d_ff, d), layer),
        ],
        out_specs=pl.BlockSpec((None, tm, d), lambda i, j: (i, j, 0)),
        out_shape=jax.ShapeDtypeStruct((b, t, d), F32),
        compiler_params=_params("arbitrary", "arbitrary"),
        name="macaron_ffn",
    )(mod3, gain, h, w_in, w_in, w_out)


def _mixin_kernel(mod_ref, g_ref, x_ref, watt_ref, wrwkv_ref, wconv_ref, cw_ref,
                  patt_ref, prwkv_ref, yconv_ref, carry_ref):
    @pl.when(pl.program_id(1) == 0)
    def _():
        carry_ref[...] = jnp.zeros_like(carry_ref)

    x = x_ref[...]
    hn = _modulated_norm(x, g_ref[...], mod_ref[0:1, :], mod_ref[1:2, :]).astype(BF16)
    patt_ref[...] = _dot(hn, watt_ref[...]).astype(patt_ref.dtype)
    prwkv_ref[...] = _dot(hn, wrwkv_ref[...])
    pc = _dot(hn, wconv_ref[...])
    cwid = yconv_ref.shape[-1]
    b_gate = pc[:, 0:cwid]
    u = pc[:, cwid:2 * cwid] * pc[:, 2 * cwid:3 * cwid]
    tm = u.shape[0]
    row = lax.broadcasted_iota(jnp.int32, (tm, 1), 0)
    prev1 = carry_ref[1:2, :]
    prev2 = carry_ref[0:1, :]
    u1 = jnp.where(row == 0, prev1, pltpu.roll(u, 1, 0))
    u2 = jnp.where(row == 0, prev2, jnp.where(row == 1, prev1, pltpu.roll(u, 2, 0)))
    y = cw_ref[0:1, :] * u2 + cw_ref[1:2, :] * u1 + cw_ref[2:3, :] * u
    yconv_ref[...] = (b_gate * y).astype(yconv_ref.dtype)
    carry_ref[0:2, :] = u[tm - 2:tm, :]


def _mixin(h, mod3, gain, w_att, w_rwkv, w_conv, conv_w, layer):
    b, t, d = h.shape
    tm = min(TOKEN_TILE, t)
    tok = lambda width: pl.BlockSpec((None, tm, width), lambda i, j: (i, j, 0))
    return pl.pallas_call(
        _mixin_kernel,
        grid=(b, t // tm),
        in_specs=[
            pl.BlockSpec((None, 3, d), lambda i, j: (i, 0, 0)),
            _layer_spec((1, d), layer),
            tok(d),
            _layer_spec((d, ATT_PROJ_WIDTH), layer),
            _layer_spec((d, RWKV_PROJ_WIDTH), layer),
            _layer_spec((d, 3 * CONV_WIDTH), layer),
            _layer_spec((CONV_K, CONV_WIDTH), layer),
        ],
        out_specs=[tok(ATT_PROJ_WIDTH), tok(RWKV_PROJ_WIDTH), tok(CONV_WIDTH)],
        out_shape=[
            jax.ShapeDtypeStruct((b, t, ATT_PROJ_WIDTH), BF16),
            jax.ShapeDtypeStruct((b, t, RWKV_PROJ_WIDTH), F32),
            jax.ShapeDtypeStruct((b, t, CONV_WIDTH), BF16),
        ],
        scratch_shapes=[pltpu.VMEM((8, CONV_WIDTH), F32)],
        compiler_params=_params("arbitrary", "arbitrary"),
        name="mix_in_proj",
    )(mod3, gain, h, w_att, w_rwkv, w_conv, conv_w)


def _head_rms(x, ones_blk, gain):
    ss = _dot((x * x).astype(BF16), ones_blk)
    return x * lax.rsqrt(ss * (1.0 / HEAD_DIM) + EPS) * gain


def _attn_kernel(sink_ref, qg_ref, kg_ref, bias_ref, onesq_ref, onesk_ref, densel_ref,
                 q_ref, kvc_ref, kvp_ref, o_ref):
    n = pl.program_id(1)
    blk = ATT_BLOCK
    n_sub = q_ref.shape[0] // blk
    rows = 2 * blk
    low = lax.broadcasted_iota(jnp.int32, (1, 2 * HEAD_DIM), 1) < HEAD_DIM
    top = lax.broadcasted_iota(jnp.int32, (rows, 1), 0) < blk

    q = q_ref[...].astype(F32)
    qn = _head_rms(q, onesq_ref[...], qg_ref[...]).astype(BF16)
    kv = jnp.concatenate([kvp_ref[...], kvc_ref[...]], axis=0).astype(F32)
    k = kv[:, 0:ATT_KV_WIDTH]
    v = kv[:, ATT_KV_WIDTH:2 * ATT_KV_WIDTH]
    kn = _head_rms(k, onesk_ref[...], kg_ref[...])
    kr = pltpu.roll(kn, HEAD_DIM, 1)
    vr = pltpu.roll(v, HEAD_DIM, 1)
    zero = jnp.zeros_like(kn)
    bf = lambda z: z.astype(BF16)
    k_low = [bf(jnp.where(low, kn, zero)), bf(jnp.where(low, kr, zero))]
    k_high = [bf(jnp.where(low, zero, kr)), bf(jnp.where(low, zero, kn))]
    v_low = [bf(jnp.where(low, v, zero)), bf(jnp.where(low, vr, zero))]
    v_high = [bf(jnp.where(low, zero, vr)), bf(jnp.where(low, zero, v))]
    den_sel = densel_ref[...]

    for j in range(n_sub):
        bias = bias_ref[jnp.minimum(n, 1)] if j == 0 else bias_ref[1]
        keys = slice(j * blk, (j + 2) * blk)
        for g in range(ATT_KV_HEADS):
            qj = qn[j * blk:(j + 1) * blk]
            qg = jnp.concatenate([qj[:, 256 * g:256 * g + 128], qj[:, 256 * g + 128:256 * g + 256]], axis=0)
            kcat = jnp.concatenate([k_low[g][keys], k_high[g][keys]], axis=0)
            vcat = jnp.concatenate([v_low[g][keys], v_high[g][keys]], axis=0)
            s = _dot_nt(qg, kcat) + bias
            h0 = 4 * g
            sink_e = jnp.where(top, sink_ref[h0], sink_ref[h0 + 2])
            sink_o = jnp.where(top, sink_ref[h0 + 1], sink_ref[h0 + 3])
            s_e = s[:, 0:2 * blk]
            s_o = s[:, 2 * blk:4 * blk]
            m_e = jnp.maximum(jnp.max(s_e, axis=-1, keepdims=True), sink_e)
            m_o = jnp.maximum(jnp.max(s_o, axis=-1, keepdims=True), sink_o)
            p = bf(jnp.concatenate([jnp.exp2(s_e - m_e), jnp.exp2(s_o - m_o)], axis=1))
            nd = _dot(p, jnp.concatenate([vcat, den_sel], axis=1))
            den = nd[:, 2 * HEAD_DIM:] + jnp.where(low, jnp.exp2(sink_e - m_e), jnp.exp2(sink_o - m_o))
            out = (nd[:, 0:2 * HEAD_DIM] / den).astype(o_ref.dtype)
            o_ref[j * blk:(j + 1) * blk, 256 * g:256 * g + 128] = out[0:blk]
            o_ref[j * blk:(j + 1) * blk, 256 * g + 128:256 * g + 256] = out[blk:rows]


def _attention_constants():
    blk = ATT_BLOCK
    ri = jnp.arange(2 * blk)[:, None] % blk
    cj = jnp.arange(4 * blk)[None, :] % (2 * blk)
    band = (cj > ri) & (cj <= ri + blk)
    bias = jnp.stack([jnp.where(band & (cj >= blk), 0.0, NEG_BIG), jnp.where(band, 0.0, NEG_BIG)]).astype(F32)
    ones = lambda n: (jnp.arange(n)[:, None] // HEAD_DIM == jnp.arange(n)[None, :] // HEAD_DIM).astype(BF16)
    den_sel = (jnp.arange(4 * blk)[:, None] // (2 * blk) == jnp.arange(2 * HEAD_DIM)[None, :] // HEAD_DIM)
    return bias, ones(ATT_WIDTH), ones(ATT_KV_WIDTH), den_sel.astype(BF16)


def _attention(p_att, q_gain, k_gain, sinks):
    b, t, _ = p_att.shape
    blk = ATT_BLOCK
    tq = min(ATT_TILE, t)
    n_sub = tq // blk
    tile_gain = lambda g, reps: jnp.tile(g.astype(F32), reps).reshape(1, reps * HEAD_DIM)
    log2e = math.log2(math.e)
    bias, ones_q, ones_k, den_sel = _attention_constants()
    kv_col = ATT_WIDTH // (2 * ATT_KV_WIDTH)
    return pl.pallas_call(
        _attn_kernel,
        grid=(b, t // tq),
        in_specs=[
            pl.BlockSpec(memory_space=pltpu.SMEM),
            _const_spec((1, ATT_WIDTH)),
            _const_spec((1, ATT_KV_WIDTH)),
            _const_spec(bias.shape),
            _const_spec(ones_q.shape),
            _const_spec(ones_k.shape),
            _const_spec(den_sel.shape),
            pl.BlockSpec((None, tq, ATT_WIDTH), lambda i, n: (i, n, 0)),
            pl.BlockSpec((None, tq, 2 * ATT_KV_WIDTH), lambda i, n: (i, n, kv_col)),
            pl.BlockSpec((None, blk, 2 * ATT_KV_WIDTH), lambda i, n: (i, jnp.maximum(n * n_sub - 1, 0), kv_col)),
        ],
        out_specs=pl.BlockSpec((None, tq, ATT_WIDTH), lambda i, n: (i, n, 0)),
        out_shape=jax.ShapeDtypeStruct((b, t, ATT_WIDTH), BF16),
        compiler_params=_params("arbitrary", "arbitrary"),
        name="swa_sink_attention",
    )(sinks.astype(F32) * log2e, tile_gain(q_gain, ATT_Q_HEADS) * (HEAD_DIM ** -0.5 * log2e),
      tile_gain(k_gain, ATT_KV_HEADS), bias, ones_q, ones_k, den_sel, p_att, p_att, p_att)


def _rwkv_kernel(p_ref, mu_ref, w0_ref, ww2_ref, a0_ref, wa2_ref, wg2_ref, kk_ref, ka_ref, rk_ref,
                 gnw_ref, gnb_ref, o_ref, prev_ref, state_ref, *, c_len):
    tt = p_ref.shape[0]
    width = RWKV_WIDTH
    n_heads = RWKV_HEADS
    n_ch = tt // c_len
    bf = lambda z: z.astype(BF16)
    each = lambda f, *cols: [f(*args) for args in zip(*cols)]

    @pl.when(pl.program_id(1) == 0)
    def _():
        prev_ref[...] = jnp.zeros_like(prev_ref)
        state_ref[...] = jnp.zeros_like(state_ref)

    p = p_ref[...]
    trow = lax.broadcasted_iota(jnp.int32, (tt, 1), 0)
    p_prev = jnp.where(trow == 0, prev_ref[0:1, :], pltpu.roll(p, 1, 0))
    prev_ref[0:1, :] = p[tt - 1:tt, :]
    xs = p + mu_ref[...] * (p_prev - p)
    r = xs[:, 0:width]
    k = xs[:, width:2 * width]
    v = xs[:, 2 * width:3 * width]
    lora = xs[:, 3 * width:3 * width + RWKV_LORA]
    gate_in = xs[:, 3 * width + RWKV_LORA:]

    dw = _dot(bf(jnp.tanh(lora)), ww2_ref[...])
    da = _dot(bf(lora), wa2_ref[...])
    g = _dot(bf(jax.nn.sigmoid(gate_in)), wg2_ref[...])
    lw = -EXP_M05 * jax.nn.sigmoid(w0_ref[...] + dw)
    a = jax.nn.sigmoid(a0_ref[...] + da)

    ones_head = _block_ones(width, HEAD_DIM)
    ones_head2 = jnp.concatenate([ones_head, ones_head], axis=0)

    def head_sum(z):
        hi = bf(z)
        lo = bf(z - hi.astype(F32))
        return _dot(jnp.concatenate([hi, lo], axis=1), ones_head2)

    kk_raw = k * kk_ref[...]
    kk = kk_raw * lax.rsqrt(jnp.maximum(head_sum(kk_raw * kk_raw), 1e-24))
    kmod = k * (1.0 + (a - 1.0) * ka_ref[...])
    b = kk * a

    tri_r = lax.broadcasted_iota(jnp.int32, (tt, tt), 0)
    tri_c = lax.broadcasted_iota(jnp.int32, (tt, tt), 1)
    tri = jnp.where((tri_c <= tri_r) & (tri_c // c_len == tri_r // c_len), 1.0, 0.0).astype(BF16)
    lw_hi, lw_mid, lw_lo = _split3(lw)
    lc = _dot(jnp.concatenate([tri, tri, tri], axis=1), jnp.concatenate([lw_hi, lw_mid, lw_lo], axis=0))

    at = kk * jnp.exp(lc - lw)
    rt = r * jnp.exp(lc)
    w_inv = jnp.exp(-lc)
    bt = b * w_inv
    kt = kmod * w_inv

    lane_head = lax.broadcasted_iota(jnp.int32, (1, width), 1) // HEAD_DIM

    def stack(z):
        zero = jnp.zeros_like(z)
        return jnp.concatenate([jnp.where(lane_head == h, z, zero) for h in range(n_heads)], axis=0)

    mi = lax.broadcasted_iota(jnp.int32, (c_len, width), 0)
    mj = lax.broadcasted_iota(jnp.int32, (c_len, width), 1) & (c_len - 1)
    strict = mj < mi
    incl = mj <= mi
    eye = mj == mi
    eye_b = jnp.where(eye, 1.0, 0.0).astype(BF16)

    sl = [slice(j * c_len, (j + 1) * c_len) for j in range(n_ch)]
    cut = lambda z: [z[s] for s in sl]
    lc_c = cut(lc)
    ltot = [z[c_len - 1:c_len, :] for z in lc_c]
    w_end = each(lambda lt, lcj: jnp.exp(lt - lcj), ltot, lc_c)
    at_b, rt_c, v_b = cut(bf(at)), cut(rt), cut(bf(v))
    rt_b = each(bf, rt_c)
    bt_bd = each(stack, cut(bf(bt)))
    kt_bd = each(stack, cut(bf(kt)))
    v_bd = each(stack, v_b)
    at_bd = each(stack, at_b)
    bh_bd = each(lambda z, w: stack(bf(z * w)), cut(b), w_end)
    kh_bd = each(lambda z, w: stack(bf(z * w)), cut(kmod), w_end)

    a_ab = each(lambda x, y: jnp.where(strict, _dot_nt(x, y), 0.0), at_b, bt_bd)
    a_ak = each(lambda x, y: bf(jnp.where(strict, _dot_nt(x, y), 0.0)), at_b, kt_bd)
    a_rb = each(lambda x, y: bf(jnp.where(incl, _dot_nt(x, y), 0.0)), rt_b, bt_bd)
    a_rk = each(lambda x, y: bf(jnp.where(incl, _dot_nt(x, y), 0.0)), rt_b, kt_bd)

    t_inv = each(lambda z: jnp.where(eye, 1.0, 0.0) - z, a_ab)
    pw = each(bf, a_ab)
    pw_bd = each(stack, pw)
    for _ in range(int(math.log2(c_len)) - 1):
        pw = each(lambda x, y: bf(_dot(x, y)), pw, pw_bd)
        pw_bd = each(stack, pw)
        t_inv = each(lambda t, y: t + _dot(bf(t), y), t_inv, pw_bd)
    t_b = each(bf, t_inv)

    y1_bd = each(lambda x, y: stack(bf(_dot(x, y))), a_ak, v_bd)
    at2_bd = each(lambda x, y: stack(bf(_dot(x, y))), t_b, at_bd)
    u2_bd = each(lambda x, y: stack(bf(_dot(x, y))), t_b, y1_bd)
    r2 = each(lambda z, x, y: bf(z - _dot(x, y)), rt_c, a_rb, at2_bd)
    o2 = each(lambda x, y, z, w: _dot(x, y) - _dot(z, w), a_rk, v_bd, a_rb, u2_bd)
    bh_t = each(lambda y: bf(_dot_nt(eye_b, y)), bh_bd)
    kh_t = each(lambda y: bf(_dot_nt(eye_b, y)), kh_bd)
    p_t = each(lambda lt, x, y: bf(jnp.where(eye, jnp.exp(lt), 0.0) - _dot(x, y)), ltot, bh_t, at2_bd)
    q_t = each(lambda x, y, z, w: _dot(x, y) - _dot(z, w), kh_t, v_bd, bh_t, u2_bd)

    state = state_ref[...]
    ys = []
    for j in range(n_ch):
        s_bd = stack(bf(state))
        ys.append(_dot(r2[j], s_bd) + o2[j])
        state = _dot(p_t[j], s_bd) + q_t[j]
    state_ref[...] = state
    y = jnp.concatenate(ys, axis=0) if n_ch > 1 else ys[0]

    mean = head_sum(y) * (1.0 / HEAD_DIM)
    dev = y - mean
    var = head_sum(dev * dev) * (1.0 / HEAD_DIM)
    yn = dev * lax.rsqrt(var + RWKV_GN_EPS) * gnw_ref[...] + gnb_ref[...]
    bonus = head_sum(r * kmod * rk_ref[...]) * v
    o_ref[...] = ((yn + bonus) * g).astype(o_ref.dtype)


def _rwkv(p_rwkv, mu, w0, ww2_pad, a0, wa2_pad, wg2, k_k, k_a, r_k, gn_w, gn_b, layer):
    b, t, pw = p_rwkv.shape
    tt = min(RWKV_TILE, t)
    c_len = min(RWKV_CHUNK, tt)
    vec = lambda n: _layer_spec((1, n), layer)
    return pl.pallas_call(
        functools.partial(_rwkv_kernel, c_len=c_len),
        grid=(b, t // tt),
        in_specs=[
            pl.BlockSpec((None, tt, pw), lambda i, c: (i, c, 0)),
            vec(pw), vec(RWKV_WIDTH),
            _layer_spec((RWKV_LORA, RWKV_WIDTH), layer),
            vec(RWKV_WIDTH),
            _layer_spec((RWKV_LORA, RWKV_WIDTH), layer),
            _layer_spec((RWKV_LORA, RWKV_WIDTH), layer),
            vec(RWKV_WIDTH), vec(RWKV_WIDTH), vec(RWKV_WIDTH), vec(RWKV_WIDTH), vec(RWKV_WIDTH),
        ],
        out_specs=pl.BlockSpec((None, tt, RWKV_WIDTH), lambda i, c: (i, c, 0)),
        out_shape=jax.ShapeDtypeStruct((b, t, RWKV_WIDTH), BF16),
        scratch_shapes=[pltpu.VMEM((8, pw), F32), pltpu.VMEM((HEAD_DIM, RWKV_WIDTH), F32)],
        compiler_params=_params("arbitrary", "arbitrary"),
        name="rwkv7_chunked",
    )(p_rwkv, mu, w0, ww2_pad, a0, wa2_pad, wg2, k_k, k_a, r_k, gn_w, gn_b)


def _mixout_kernel(mod_ref, h_ref, ya_ref, yr_ref, yc_ref, w_ref, o_ref):
    wa = ATT_WIDTH
    wr = wa + RWKV_WIDTH
    mixed = (_dot(ya_ref[...], w_ref[0:wa, :]) + _dot(yr_ref[...], w_ref[wa:wr, :])
             + _dot(yc_ref[...], w_ref[wr:, :]))
    o_ref[...] = h_ref[...] + (1.0 + mod_ref[2:3, :]) * mixed


def _mixout(h, mod3, y_att, y_rwkv, y_conv, w_out, layer):
    b, t, d = h.shape
    tm = min(TOKEN_TILE, t)
    tok = lambda width: pl.BlockSpec((None, tm, width), lambda i, j: (i, j, 0))
    return pl.pallas_call(
        _mixout_kernel,
        grid=(b, t // tm),
        in_specs=[
            pl.BlockSpec((None, 3, d), lambda i, j: (i, 0, 0)),
            tok(d), tok(ATT_WIDTH), tok(RWKV_WIDTH), tok(CONV_WIDTH),
            _layer_spec((d, d), layer),
        ],
        out_specs=tok(d),
        out_shape=jax.ShapeDtypeStruct((b, t, d), F32),
        compiler_params=_params("arbitrary", "arbitrary"),
        name="mix_out_proj",
    )(mod3, h, y_att, y_rwkv, y_conv, w_out)


def kernel(x, c, w_ada, b_ada, g_ffn1, w_ffn1_in, w_ffn1_out, g_mix, w_mix_in, w_mix_out, att_q_gain, att_k_gain, att_sinks, rwkv_mu, rwkv_w0, rwkv_w_w2, rwkv_a0, rwkv_a_w2, rwkv_g_w2, rwkv_k_k, rwkv_k_a, rwkv_r_k, rwkv_gn_w, rwkv_gn_b, conv_w, g_ffn2, w_ffn2_in, w_ffn2_out):
    n_layers, d = g_ffn1.shape
    bsz = x.shape[0]
    row3 = lambda z: z.astype(F32).reshape(n_layers, 1, -1)
    bf = lambda z: z.astype(BF16)

    mod = _modulation(c, w_ada, b_ada).reshape(n_layers, bsz, N_MOD // 3, 3, d)

    a_end = ATT_PROJ_WIDTH
    r_end = a_end + RWKV_PROJ_WIDTH
    w_att, w_rwkv, w_conv = bf(w_mix_in[:, :, :a_end]), bf(w_mix_in[:, :, a_end:r_end]), bf(w_mix_in[:, :, r_end:])
    w_mix_out_b = bf(w_mix_out)
    w1_in, w1_out, w2_in, w2_out = bf(w_ffn1_in), bf(w_ffn1_out), bf(w_ffn2_in), bf(w_ffn2_out)
    half = RWKV_LORA // 2
    zeros = jnp.zeros((n_layers, half, RWKV_WIDTH), F32)
    ww2_pad = bf(jnp.concatenate([rwkv_w_w2, zeros], axis=1))
    wa2_pad = bf(jnp.concatenate([zeros, rwkv_a_w2], axis=1))
    wg2 = bf(rwkv_g_w2)
    g1, gm, g2 = row3(g_ffn1), row3(g_mix), row3(g_ffn2)
    mu, w0, a0 = row3(rwkv_mu), row3(rwkv_w0), row3(rwkv_a0)
    k_k, k_a, r_k = row3(rwkv_k_k), row3(rwkv_k_a), row3(rwkv_r_k)
    gn_w, gn_b = row3(rwkv_gn_w), row3(rwkv_gn_b)

    h = x
    for l in range(n_layers):
        h = _ffn(h, mod[l, :, 0], g1, w1_in, w1_out, l)
        p_att, p_rwkv, y_conv = _mixin(h, mod[l, :, 1], gm, w_att, w_rwkv, w_conv, conv_w, l)
        y_att = _attention(p_att, att_q_gain[l], att_k_gain[l], att_sinks[l])
        y_rwkv = _rwkv(p_rwkv, mu, w0, ww2_pad, a0, wa2_pad, wg2, k_k, k_a, r_k, gn_w, gn_b, l)
        h = _mixout(h, mod[l, :, 1], y_att, y_rwkv, y_conv, w_mix_out_b, l)
        h = _ffn(h, mod[l, :, 2], g2, w2_in, w2_out, l)
    return h
```

```python
import functools
import math

import jax
import jax.numpy as jnp
from jax import lax
from jax.experimental import pallas as pl
from jax.experimental.pallas import tpu as pltpu

F32 = jnp.float32
BF16 = jnp.bfloat16

HEAD_DIM = 64
ATT_Q_HEADS = 8
ATT_KV_HEADS = 2
ATT_WIDTH = ATT_Q_HEADS * HEAD_DIM
ATT_KV_WIDTH = ATT_KV_HEADS * HEAD_DIM
ATT_PROJ_WIDTH = ATT_WIDTH + 2 * ATT_KV_WIDTH
ATT_BLOCK = 128
ATT_TILE = 512
RWKV_HEADS = 4
RWKV_WIDTH = RWKV_HEADS * HEAD_DIM
RWKV_LORA = 128
RWKV_PROJ_WIDTH = 3 * RWKV_WIDTH + 2 * RWKV_LORA
RWKV_GN_EPS = 64e-5
RWKV_CHUNK = 64
RWKV_TILE = 256
RWKV_SEQS = 2
CONV_WIDTH = 256
CONV_K = 3
N_MOD = 9
EPS = 1e-6
NEG_BIG = -1e30
EXP_M05 = math.exp(-0.5)

TOKEN_TILE = 512
VMEM_LIMIT = 56 * 1024 * 1024


def _dot(a, b):
    return jnp.dot(a, b, preferred_element_type=F32)


def _dot_nt(a, b):
    return lax.dot_general(a, b, (((1,), (1,)), ((), ())), preferred_element_type=F32)


def _dot_tn(a, b):
    return lax.dot_general(a, b, (((0,), (0,)), ((), ())), preferred_element_type=F32)


def _split3(x):
    hi = x.astype(BF16)
    r1 = x - hi.astype(F32)
    mid = r1.astype(BF16)
    lo = (r1 - mid.astype(F32)).astype(BF16)
    return hi, mid, lo


def _dot_exact_rhs(x, m):
    hi, mid, lo = _split3(x)
    return _dot(hi, m) + _dot(mid, m) + _dot(lo, m)


def _dot_exact_lhs(m, x):
    hi, mid, lo = _split3(x)
    return _dot(m, hi) + _dot(m, mid) + _dot(m, lo)


def _block_ones(n, blk):
    r = lax.broadcasted_iota(jnp.int32, (n, n), 0) // blk
    c = lax.broadcasted_iota(jnp.int32, (n, n), 1) // blk
    return jnp.where(r == c, 1.0, 0.0).astype(BF16)


def _const_spec(shape):
    nd = len(shape)
    return pl.BlockSpec(shape, lambda *_: (0,) * nd, pipeline_mode=pl.Buffered(1))


def _layer_spec(shape, layer):
    nd = len(shape)
    return pl.BlockSpec((None,) + tuple(shape), lambda *_: (layer,) + (0,) * nd,
                        pipeline_mode=pl.Buffered(1))


def _params(*sem):
    return pltpu.CompilerParams(dimension_semantics=sem, vmem_limit_bytes=VMEM_LIMIT)


def _modulated_norm(x, gain, shift, scale):
    ms = jnp.mean(x * x, axis=-1, keepdims=True)
    return (x * lax.rsqrt(ms + EPS) * gain) * (1.0 + scale) + shift


def _mod_kernel(c_ref, w_ref, b_ref, o_ref):
    c = c_ref[...]
    act = (c * jax.nn.sigmoid(c)).astype(BF16)
    o_ref[...] = _dot(act, w_ref[...].astype(BF16)) + b_ref[...]


def _modulation(c, w_ada, b_ada):
    n_layers, d, n = w_ada.shape
    b = c.shape[0]
    tn = d
    return pl.pallas_call(
        _mod_kernel,
        grid=(n_layers, n // tn),
        in_specs=[
            pl.BlockSpec((b, d), lambda l, j: (0, 0)),
            pl.BlockSpec((None, d, tn), lambda l, j: (l, 0, j)),
            pl.BlockSpec((None, 1, tn), lambda l, j: (l, 0, j)),
        ],
        out_specs=pl.BlockSpec((None, b, tn), lambda l, j: (l, 0, j)),
        out_shape=jax.ShapeDtypeStruct((n_layers, b, n), F32),
        compiler_params=_params("arbitrary", "arbitrary"),
        name="adaln_mod",
    )(c, w_ada, b_ada.reshape(n_layers, 1, n))


def _ffn_kernel(mod_ref, g_ref, x_ref, wg_ref, wu_ref, wo_ref, o_ref, *, n_chunks):
    x = x_ref[...]
    hn = _modulated_norm(x, g_ref[...], mod_ref[0:1, :], mod_ref[1:2, :]).astype(BF16)
    d_ff = wg_ref.shape[1]
    tf = d_ff // n_chunks
    acc = None
    for j in range(n_chunks):
        sl = slice(j * tf, (j + 1) * tf)
        gate = _dot(hn, wg_ref[:, sl])
        up = _dot(hn, wu_ref[:, sl])
        act = (gate * jax.nn.sigmoid(gate) * up).astype(BF16)
        part = _dot(act, wo_ref[sl, :])
        acc = part if acc is None else acc + part
    o_ref[...] = x + (0.5 * (1.0 + mod_ref[2:3, :])) * acc


def _ffn(h, mod3, gain, w_in, w_out, layer):
    b, t, d = h.shape
    d_ff = w_out.shape[1]
    tm = min(TOKEN_TILE, t)
    n_chunks = 2 if d_ff % 256 == 0 else 1
    kern = functools.partial(_ffn_kernel, n_chunks=n_chunks)
    w_gate_spec = pl.BlockSpec((None, d, d_ff), lambda i, j: (layer, 0, 0), pipeline_mode=pl.Buffered(1))
    w_up_spec = pl.BlockSpec((None, d, d_ff), lambda i, j: (layer, 0, 1), pipeline_mode=pl.Buffered(1))
    return pl.pallas_call(
        kern,
        grid=(b, t // tm),
        in_specs=[
            pl.BlockSpec((None, 3, d), lambda i, j: (i, 0, 0)),
            _layer_spec((1, d), layer),
            pl.BlockSpec((None, tm, d), lambda i, j: (i, j, 0)),
            w_gate_spec,
            w_up_spec,
            _layer_spec((d_ff, d), layer),
        ],
        out_specs=pl.BlockSpec((None, tm, d), lambda i, j: (i, j, 0)),
        out_shape=jax.ShapeDtypeStruct((b, t, d), F32),
        compiler_params=_params("arbitrary", "arbitrary"),
        name="macaron_ffn",
    )(mod3, gain, h, w_in, w_in, w_out)


def _mixin_kernel(mod_ref, g_ref, x_ref, watt_ref, wrwkv_ref, wconv_ref, cw_ref,
                  patt_ref, prwkv_ref, yconv_ref, carry_ref):
    @pl.when(pl.program_id(1) == 0)
    def _():
        carry_ref[...] = jnp.zeros_like(carry_ref)

    x = x_ref[...]
    hn = _modulated_norm(x, g_ref[...], mod_ref[0:1, :], mod_ref[1:2, :]).astype(BF16)
    patt_ref[...] = _dot(hn, watt_ref[...]).astype(patt_ref.dtype)
    prwkv_ref[...] = _dot(hn, wrwkv_ref[...])
    pc = _dot(hn, wconv_ref[...])
    cwid = yconv_ref.shape[-1]
    b_gate = pc[:, 0:cwid]
    u = pc[:, cwid:2 * cwid] * pc[:, 2 * cwid:3 * cwid]
    tm = u.shape[0]
    row = lax.broadcasted_iota(jnp.int32, (tm, 1), 0)
    prev1 = carry_ref[1:2, :]
    prev2 = carry_ref[0:1, :]
    u1 = jnp.where(row == 0, prev1, pltpu.roll(u, 1, 0))
    u2 = jnp.where(row == 0, prev2, jnp.where(row == 1, prev1, pltpu.roll(u, 2, 0)))
    y = cw_ref[0:1, :] * u2 + cw_ref[1:2, :] * u1 + cw_ref[2:3, :] * u
    yconv_ref[...] = (b_gate * y).astype(yconv_ref.dtype)
    carry_ref[0:2, :] = u[tm - 2:tm, :]


def _mixin(h, mod3, gain, w_att, w_rwkv, w_conv, conv_w, layer):
    b, t, d = h.shape
    tm = min(TOKEN_TILE, t)
    tok = lambda width: pl.BlockSpec((None, tm, width), lambda i, j: (i, j, 0))
    return pl.pallas_call(
        _mixin_kernel,
        grid=(b, t // tm),
        in_specs=[
            pl.BlockSpec((None, 3, d), lambda i, j: (i, 0, 0)),
            _layer_spec((1, d), layer),
            tok(d),
            _layer_spec((d, ATT_PROJ_WIDTH), layer),
            _layer_spec((d, RWKV_PROJ_WIDTH), layer),
            _layer_spec((d, 3 * CONV_WIDTH), layer),
            _layer_spec((CONV_K, CONV_WIDTH), layer),
        ],
        out_specs=[tok(ATT_PROJ_WIDTH), tok(RWKV_PROJ_WIDTH), tok(CONV_WIDTH)],
        out_shape=[
            jax.ShapeDtypeStruct((b, t, ATT_PROJ_WIDTH), BF16),
            jax.ShapeDtypeStruct((b, t, RWKV_PROJ_WIDTH), F32),
            jax.ShapeDtypeStruct((b, t, CONV_WIDTH), BF16),
        ],
        scratch_shapes=[pltpu.VMEM((8, CONV_WIDTH), F32)],
        compiler_params=_params("arbitrary", "arbitrary"),
        name="mix_in_proj",
    )(mod3, gain, h, w_att, w_rwkv, w_conv, conv_w)


def _head_rms(x, ones_blk, gain):
    ss = _dot((x * x).astype(BF16), ones_blk)
    return x * lax.rsqrt(ss * (1.0 / HEAD_DIM) + EPS) * gain


def _attn_kernel(sink_ref, qg_ref, kg_ref, bias_ref, onesq_ref, onesk_ref, densel_ref,
                 q_ref, kvc_ref, kvp_ref, o_ref):
    n = pl.program_id(1)
    blk = ATT_BLOCK
    n_sub = q_ref.shape[0] // blk
    rows = 2 * blk
    low = lax.broadcasted_iota(jnp.int32, (1, 2 * HEAD_DIM), 1) < HEAD_DIM
    top = lax.broadcasted_iota(jnp.int32, (rows, 1), 0) < blk

    q = q_ref[...].astype(F32)
    qn = _head_rms(q, onesq_ref[...], qg_ref[...]).astype(BF16)
    kv = jnp.concatenate([kvp_ref[...], kvc_ref[...]], axis=0).astype(F32)
    k = kv[:, 0:ATT_KV_WIDTH]
    v = kv[:, ATT_KV_WIDTH:2 * ATT_KV_WIDTH]
    kn = _head_rms(k, onesk_ref[...], kg_ref[...])
    kr = pltpu.roll(kn, HEAD_DIM, 1)
    vr = pltpu.roll(v, HEAD_DIM, 1)
    zero = jnp.zeros_like(kn)
    bf = lambda z: z.astype(BF16)
    k_low = [bf(jnp.where(low, kn, zero)), bf(jnp.where(low, kr, zero))]
    k_high = [bf(jnp.where(low, zero, kr)), bf(jnp.where(low, zero, kn))]
    v_low = [bf(jnp.where(low, v, zero)), bf(jnp.where(low, vr, zero))]
    v_high = [bf(jnp.where(low, zero, vr)), bf(jnp.where(low, zero, v))]
    den_sel = densel_ref[...]

    for j in range(n_sub):
        bias = bias_ref[jnp.minimum(n, 1)] if j == 0 else bias_ref[1]
        keys = slice(j * blk, (j + 2) * blk)
        for g in range(ATT_KV_HEADS):
            qj = qn[j * blk:(j + 1) * blk]
            qg = jnp.concatenate([qj[:, 256 * g:256 * g + 128], qj[:, 256 * g + 128:256 * g + 256]], axis=0)
            kcat = jnp.concatenate([k_low[g][keys], k_high[g][keys]], axis=0)
            vcat = jnp.concatenate([v_low[g][keys], v_high[g][keys]], axis=0)
            s = _dot_nt(qg, kcat) + bias
            h0 = 4 * g
            sink_e = jnp.where(top, sink_ref[h0], sink_ref[h0 + 2])
            sink_o = jnp.where(top, sink_ref[h0 + 1], sink_ref[h0 + 3])
            s_e = s[:, 0:2 * blk]
            s_o = s[:, 2 * blk:4 * blk]
            m_e = jnp.maximum(jnp.max(s_e, axis=-1, keepdims=True), sink_e)
            m_o = jnp.maximum(jnp.max(s_o, axis=-1, keepdims=True), sink_o)
            p = bf(jnp.concatenate([jnp.exp2(s_e - m_e), jnp.exp2(s_o - m_o)], axis=1))
            nd = _dot(p, jnp.concatenate([vcat, den_sel], axis=1))
            den = nd[:, 2 * HEAD_DIM:] + jnp.where(low, jnp.exp2(sink_e - m_e), jnp.exp2(sink_o - m_o))
            out = (nd[:, 0:2 * HEAD_DIM] / den).astype(o_ref.dtype)
            o_ref[j * blk:(j + 1) * blk, 256 * g:256 * g + 128] = out[0:blk]
            o_ref[j * blk:(j + 1) * blk, 256 * g + 128:256 * g + 256] = out[blk:rows]


def _attention_constants():
    blk = ATT_BLOCK
    ri = jnp.arange(2 * blk)[:, None] % blk
    cj = jnp.arange(4 * blk)[None, :] % (2 * blk)
    band = (cj > ri) & (cj <= ri + blk)
    bias = jnp.stack([jnp.where(band & (cj >= blk), 0.0, NEG_BIG), jnp.where(band, 0.0, NEG_BIG)]).astype(F32)
    ones = lambda n: (jnp.arange(n)[:, None] // HEAD_DIM == jnp.arange(n)[None, :] // HEAD_DIM).astype(BF16)
    den_sel = (jnp.arange(4 * blk)[:, None] // (2 * blk) == jnp.arange(2 * HEAD_DIM)[None, :] // HEAD_DIM)
    return bias, ones(ATT_WIDTH), ones(ATT_KV_WIDTH), den_sel.astype(BF16)


def _attention(p_att, q_gain, k_gain, sinks):
    b, t, _ = p_att.shape
    blk = ATT_BLOCK
    tq = min(ATT_TILE, t)
    n_sub = tq // blk
    tile_gain = lambda g, reps: jnp.tile(g.astype(F32), reps).reshape(1, reps * HEAD_DIM)
    log2e = math.log2(math.e)
    bias, ones_q, ones_k, den_sel = _attention_constants()
    kv_col = ATT_WIDTH // (2 * ATT_KV_WIDTH)
    return pl.pallas_call(
        _attn_kernel,
        grid=(b, t // tq),
        in_specs=[
            pl.BlockSpec(memory_space=pltpu.SMEM),
            _const_spec((1, ATT_WIDTH)),
            _const_spec((1, ATT_KV_WIDTH)),
            _const_spec(bias.shape),
            _const_spec(ones_q.shape),
            _const_spec(ones_k.shape),
            _const_spec(den_sel.shape),
            pl.BlockSpec((None, tq, ATT_WIDTH), lambda i, n: (i, n, 0)),
            pl.BlockSpec((None, tq, 2 * ATT_KV_WIDTH), lambda i, n: (i, n, kv_col)),
            pl.BlockSpec((None, blk, 2 * ATT_KV_WIDTH), lambda i, n: (i, jnp.maximum(n * n_sub - 1, 0), kv_col)),
        ],
        out_specs=pl.BlockSpec((None, tq, ATT_WIDTH), lambda i, n: (i, n, 0)),
        out_shape=jax.ShapeDtypeStruct((b, t, ATT_WIDTH), BF16),
        compiler_params=_params("arbitrary", "arbitrary"),
        name="swa_sink_attention",
    )(sinks.astype(F32) * log2e, tile_gain(q_gain, ATT_Q_HEADS) * (HEAD_DIM ** -0.5 * log2e),
      tile_gain(k_gain, ATT_KV_HEADS), bias, ones_q, ones_k, den_sel, p_att, p_att, p_att)


def _rwkv_kernel(p_ref, mu_ref, w0_ref, ww2_ref, a0_ref, wa2_ref, wg2_ref, kk_ref, ka_ref, rk_ref,
                 gnw_ref, gnb_ref, o_ref, prev_ref, state_ref, *, c_len):
    n_seq, tt, _ = p_ref.shape
    width = RWKV_WIDTH
    n_heads = RWKV_HEADS
    n_ch = tt // c_len
    bf = lambda z: z.astype(BF16)
    each = lambda f, *cols: [f(*args) for args in zip(*cols)]

    @pl.when(pl.program_id(1) == 0)
    def _():
        prev_ref[...] = jnp.zeros_like(prev_ref)
        state_ref[...] = jnp.zeros_like(state_ref)

    ones_head = _block_ones(width, HEAD_DIM)
    ones_head2 = jnp.concatenate([ones_head, ones_head], axis=0)

    def head_sum(z):
        hi = bf(z)
        lo = bf(z - hi.astype(F32))
        return _dot(jnp.concatenate([hi, lo], axis=1), ones_head2)

    tri_r = lax.broadcasted_iota(jnp.int32, (tt, tt), 0)
    tri_c = lax.broadcasted_iota(jnp.int32, (tt, tt), 1)
    tri = jnp.where((tri_c <= tri_r) & (tri_c // c_len == tri_r // c_len), 1.0, 0.0).astype(BF16)
    tri3 = jnp.concatenate([tri, tri, tri], axis=1)
    trow = lax.broadcasted_iota(jnp.int32, (tt, 1), 0)
    lane_head = lax.broadcasted_iota(jnp.int32, (1, width), 1) // HEAD_DIM

    def stack(z):
        zero = jnp.zeros_like(z)
        return jnp.concatenate([jnp.where(lane_head == h, z, zero) for h in range(n_heads)], axis=0)

    sl = [slice(j * c_len, (j + 1) * c_len) for j in range(n_ch)]
    cut = lambda z: [z[s] for s in sl]

    def token_features(s):
        p = p_ref[s]
        p_prev = jnp.where(trow == 0, prev_ref[s, 0:1, :], pltpu.roll(p, 1, 0))
        prev_ref[s, 0:1, :] = p[tt - 1:tt, :]
        xs = p + mu_ref[...] * (p_prev - p)
        r = xs[:, 0:width]
        k = xs[:, width:2 * width]
        v = xs[:, 2 * width:3 * width]
        lora = xs[:, 3 * width:3 * width + RWKV_LORA]
        gate_in = xs[:, 3 * width + RWKV_LORA:]
        dw = _dot(bf(jnp.tanh(lora)), ww2_ref[...])
        da = _dot(bf(lora), wa2_ref[...])
        g = _dot(bf(jax.nn.sigmoid(gate_in)), wg2_ref[...])
        lw = -EXP_M05 * jax.nn.sigmoid(w0_ref[...] + dw)
        a = jax.nn.sigmoid(a0_ref[...] + da)
        kk_raw = k * kk_ref[...]
        kk = kk_raw * lax.rsqrt(jnp.maximum(head_sum(kk_raw * kk_raw), 1e-24))
        kmod = k * (1.0 + (a - 1.0) * ka_ref[...])
        b = kk * a
        lw_hi, lw_mid, lw_lo = _split3(lw)
        lc = _dot(tri3, jnp.concatenate([lw_hi, lw_mid, lw_lo], axis=0))
        w_inv = jnp.exp(-lc)
        lc_c = cut(lc)
        ltot = [z[c_len - 1:c_len, :] for z in lc_c]
        w_end = each(lambda lt, lcj: jnp.exp(lt - lcj), ltot, lc_c)
        chunks = dict(
            ltot=ltot,
            at=cut(bf(kk * jnp.exp(lc - lw))),
            rt=cut(r * jnp.exp(lc)),
            bt=cut(bf(b * w_inv)),
            kt=cut(bf(kmod * w_inv)),
            v=cut(bf(v)),
            bh=each(lambda z, w: bf(z * w), cut(b), w_end),
            kh=each(lambda z, w: bf(z * w), cut(kmod), w_end),
        )
        return chunks, (r, kmod, v, g)

    feats = [token_features(s) for s in range(n_seq)]
    col = lambda name: [z for chunks, _ in feats for z in chunks[name]]
    ltot, at_b, rt_c = col("ltot"), col("at"), col("rt")
    rt_b = each(bf, rt_c)
    bt_bd, kt_bd, v_bd, at_bd = (each(stack, col(nm)) for nm in ("bt", "kt", "v", "at"))
    bh_bd, kh_bd = each(stack, col("bh")), each(stack, col("kh"))

    mi = lax.broadcasted_iota(jnp.int32, (c_len, width), 0)
    mj = lax.broadcasted_iota(jnp.int32, (c_len, width), 1) & (c_len - 1)
    strict = mj < mi
    incl = mj <= mi
    eye = mj == mi
    eye_b = jnp.where(eye, 1.0, 0.0).astype(BF16)

    a_ab = each(lambda x, y: jnp.where(strict, _dot_nt(x, y), 0.0), at_b, bt_bd)
    a_ak = each(lambda x, y: bf(jnp.where(strict, _dot_nt(x, y), 0.0)), at_b, kt_bd)
    a_rb = each(lambda x, y: bf(jnp.where(incl, _dot_nt(x, y), 0.0)), rt_b, bt_bd)
    a_rk = each(lambda x, y: bf(jnp.where(incl, _dot_nt(x, y), 0.0)), rt_b, kt_bd)

    t_inv = each(lambda z: jnp.where(eye, 1.0, 0.0) - z, a_ab)
    pw = each(bf, a_ab)
    pw_bd = each(stack, pw)
    for _ in range(int(math.log2(c_len)) - 1):
        pw = each(lambda x, y: bf(_dot(x, y)), pw, pw_bd)
        pw_bd = each(stack, pw)
        t_inv = each(lambda t, y: t + _dot(bf(t), y), t_inv, pw_bd)
    t_b = each(bf, t_inv)

    y1_bd = each(lambda x, y: stack(bf(_dot(x, y))), a_ak, v_bd)
    at2_bd = each(lambda x, y: stack(bf(_dot(x, y))), t_b, at_bd)
    u2_bd = each(lambda x, y: stack(bf(_dot(x, y))), t_b, y1_bd)
    r2 = each(lambda z, x, y: bf(z - _dot(x, y)), rt_c, a_rb, at2_bd)
    o2 = each(lambda x, y, z, w: _dot(x, y) - _dot(z, w), a_rk, v_bd, a_rb, u2_bd)
    bh_t = each(lambda y: bf(_dot_nt(eye_b, y)), bh_bd)
    kh_t = each(lambda y: bf(_dot_nt(eye_b, y)), kh_bd)
    p_t = each(lambda lt, x, y: bf(jnp.where(eye, jnp.exp(lt), 0.0) - _dot(x, y)), ltot, bh_t, at2_bd)
    q_t = each(lambda x, y, z, w: _dot(x, y) - _dot(z, w), kh_t, v_bd, bh_t, u2_bd)

    states = [state_ref[s] for s in range(n_seq)]
    ys = [[] for _ in range(n_seq)]
    for j in range(n_ch):
        for s in range(n_seq):
            i = s * n_ch + j
            s_bd = stack(bf(states[s]))
            ys[s].append(_dot(r2[i], s_bd) + o2[i])
            states[s] = _dot(p_t[i], s_bd) + q_t[i]

    for s in range(n_seq):
        state_ref[s] = states[s]
        r, kmod, v, g = feats[s][1]
        y = jnp.concatenate(ys[s], axis=0) if n_ch > 1 else ys[s][0]
        mean = head_sum(y) * (1.0 / HEAD_DIM)
        dev = y - mean
        var = head_sum(dev * dev) * (1.0 / HEAD_DIM)
        yn = dev * lax.rsqrt(var + RWKV_GN_EPS) * gnw_ref[...] + gnb_ref[...]
        bonus = head_sum(r * kmod * rk_ref[...]) * v
        o_ref[s] = ((yn + bonus) * g).astype(o_ref.dtype)


def _rwkv(p_rwkv, mu, w0, ww2_pad, a0, wa2_pad, wg2, k_k, k_a, r_k, gn_w, gn_b, layer):
    b, t, pw = p_rwkv.shape
    tt = min(RWKV_TILE, t)
    c_len = min(RWKV_CHUNK, tt)
    n_seq = RWKV_SEQS if b % RWKV_SEQS == 0 else 1
    vec = lambda n: _layer_spec((1, n), layer)
    return pl.pallas_call(
        functools.partial(_rwkv_kernel, c_len=c_len),
        grid=(b // n_seq, t // tt),
        in_specs=[
            pl.BlockSpec((n_seq, tt, pw), lambda i, c: (i, c, 0)),
            vec(pw), vec(RWKV_WIDTH),
            _layer_spec((RWKV_LORA, RWKV_WIDTH), layer),
            vec(RWKV_WIDTH),
            _layer_spec((RWKV_LORA, RWKV_WIDTH), layer),
            _layer_spec((RWKV_LORA, RWKV_WIDTH), layer),
            vec(RWKV_WIDTH), vec(RWKV_WIDTH), vec(RWKV_WIDTH), vec(RWKV_WIDTH), vec(RWKV_WIDTH),
        ],
        out_specs=pl.BlockSpec((n_seq, tt, RWKV_WIDTH), lambda i, c: (i, c, 0)),
        out_shape=jax.ShapeDtypeStruct((b, t, RWKV_WIDTH), BF16),
        scratch_shapes=[pltpu.VMEM((n_seq, 8, pw), F32), pltpu.VMEM((n_seq, HEAD_DIM, RWKV_WIDTH), F32)],
        compiler_params=_params("arbitrary", "arbitrary"),
        name="rwkv7_chunked",
    )(p_rwkv, mu, w0, ww2_pad, a0, wa2_pad, wg2, k_k, k_a, r_k, gn_w, gn_b)


def _mixout_kernel(mod_ref, h_ref, ya_ref, yr_ref, yc_ref, w_ref, o_ref):
    wa = ATT_WIDTH
    wr = wa + RWKV_WIDTH
    mixed = (_dot(ya_ref[...], w_ref[0:wa, :]) + _dot(yr_ref[...], w_ref[wa:wr, :])
             + _dot(yc_ref[...], w_ref[wr:, :]))
    o_ref[...] = h_ref[...] + (1.0 + mod_ref[2:3, :]) * mixed


def _mixout(h, mod3, y_att, y_rwkv, y_conv, w_out, layer):
    b, t, d = h.shape
    tm = min(TOKEN_TILE, t)
    tok = lambda width: pl.BlockSpec((None, tm, width), lambda i, j: (i, j, 0))
    return pl.pallas_call(
        _mixout_kernel,
        grid=(b, t // tm),
        in_specs=[
            pl.BlockSpec((None, 3, d), lambda i, j: (i, 0, 0)),
            tok(d), tok(ATT_WIDTH), tok(RWKV_WIDTH), tok(CONV_WIDTH),
            _layer_spec((d, d), layer),
        ],
        out_specs=tok(d),
        out_shape=jax.ShapeDtypeStruct((b, t, d), F32),
        compiler_params=_params("arbitrary", "arbitrary"),
        name="mix_out_proj",
    )(mod3, h, y_att, y_rwkv, y_conv, w_out)


def kernel(x, c, w_ada, b_ada, g_ffn1, w_ffn1_in, w_ffn1_out, g_mix, w_mix_in, w_mix_out, att_q_gain, att_k_gain, att_sinks, rwkv_mu, rwkv_w0, rwkv_w_w2, rwkv_a0, rwkv_a_w2, rwkv_g_w2, rwkv_k_k, rwkv_k_a, rwkv_r_k, rwkv_gn_w, rwkv_gn_b, conv_w, g_ffn2, w_ffn2_in, w_ffn2_out):
    n_layers, d = g_ffn1.shape
    bsz = x.shape[0]
    row3 = lambda z: z.astype(F32).reshape(n_layers, 1, -1)
    bf = lambda z: z.astype(BF16)

    mod = _modulation(c, w_ada, b_ada).reshape(n_layers, bsz, N_MOD // 3, 3, d)

    a_end = ATT_PROJ_WIDTH
    r_end = a_end + RWKV_PROJ_WIDTH
    w_att, w_rwkv, w_conv = bf(w_mix_in[:, :, :a_end]), bf(w_mix_in[:, :, a_end:r_end]), bf(w_mix_in[:, :, r_end:])
    w_mix_out_b = bf(w_mix_out)
    w1_in, w1_out, w2_in, w2_out = bf(w_ffn1_in), bf(w_ffn1_out), bf(w_ffn2_in), bf(w_ffn2_out)
    half = RWKV_LORA // 2
    zeros = jnp.zeros((n_layers, half, RWKV_WIDTH), F32)
    ww2_pad = bf(jnp.concatenate([rwkv_w_w2, zeros], axis=1))
    wa2_pad = bf(jnp.concatenate([zeros, rwkv_a_w2], axis=1))
    wg2 = bf(rwkv_g_w2)
    g1, gm, g2 = row3(g_ffn1), row3(g_mix), row3(g_ffn2)
    mu, w0, a0 = row3(rwkv_mu), row3(rwkv_w0), row3(rwkv_a0)
    k_k, k_a, r_k = row3(rwkv_k_k), row3(rwkv_k_a), row3(rwkv_r_k)
    gn_w, gn_b = row3(rwkv_gn_w), row3(rwkv_gn_b)

    h = x
    for l in range(n_layers):
        h = _ffn(h, mod[l, :, 0], g1, w1_in, w1_out, l)
        p_att, p_rwkv, y_conv = _mixin(h, mod[l, :, 1], gm, w_att, w_rwkv, w_conv, conv_w, l)
        y_att = _attention(p_att, att_q_gain[l], att_k_gain[l], att_sinks[l])
        y_rwkv = _rwkv(p_rwkv, mu, w0, ww2_pad, a0, wa2_pad, wg2, k_k, k_a, r_k, gn_w, gn_b, l)
        h = _mixout(h, mod[l, :, 1], y_att, y_rwkv, y_conv, w_mix_out_b, l)
        h = _ffn(h, mod[l, :, 2], g2, w2_in, w2_out, l)
    return h
```

```python
import functools
import math

import jax
import jax.numpy as jnp
from jax import lax
from jax.experimental import pallas as pl
from jax.experimental.pallas import tpu as pltpu

F32 = jnp.float32
BF16 = jnp.bfloat16

HEAD_DIM = 64
ATT_Q_HEADS = 8
ATT_KV_HEADS = 2
ATT_WIDTH = ATT_Q_HEADS * HEAD_DIM
ATT_KV_WIDTH = ATT_KV_HEADS * HEAD_DIM
ATT_PROJ_WIDTH = ATT_WIDTH + 2 * ATT_KV_WIDTH
ATT_BLOCK = 128
ATT_TILE = 512
RWKV_HEADS = 4
RWKV_WIDTH = RWKV_HEADS * HEAD_DIM
RWKV_LORA = 128
RWKV_PROJ_WIDTH = 3 * RWKV_WIDTH + 2 * RWKV_LORA
RWKV_GN_EPS = 64e-5
RWKV_CHUNK = 64
RWKV_TILE = 256
RWKV_SEQS = 2
CONV_WIDTH = 256
CONV_K = 3
N_MOD = 9
EPS = 1e-6
NEG_BIG = -1e30
EXP_M05 = math.exp(-0.5)

TOKEN_TILE = 512
VMEM_LIMIT = 56 * 1024 * 1024


def _dot(a, b):
    return jnp.dot(a, b, preferred_element_type=F32)


def _dot_nt(a, b):
    return lax.dot_general(a, b, (((1,), (1,)), ((), ())), preferred_element_type=F32)


def _split3(x):
    hi = x.astype(BF16)
    r1 = x - hi.astype(F32)
    mid = r1.astype(BF16)
    lo = (r1 - mid.astype(F32)).astype(BF16)
    return hi, mid, lo


def _block_ones(n, blk):
    r = lax.broadcasted_iota(jnp.int32, (n, n), 0) // blk
    c = lax.broadcasted_iota(jnp.int32, (n, n), 1) // blk
    return jnp.where(r == c, 1.0, 0.0).astype(BF16)


def _const_spec(shape):
    nd = len(shape)
    return pl.BlockSpec(shape, lambda *_: (0,) * nd, pipeline_mode=pl.Buffered(1))


def _layer_spec(shape, layer):
    nd = len(shape)
    return pl.BlockSpec((None,) + tuple(shape), lambda *_: (layer,) + (0,) * nd,
                        pipeline_mode=pl.Buffered(1))


def _params(*sem):
    return pltpu.CompilerParams(dimension_semantics=sem, vmem_limit_bytes=VMEM_LIMIT)


def _modulated_norm(x, gain, shift, scale):
    ms = jnp.mean(x * x, axis=-1, keepdims=True)
    return (x * lax.rsqrt(ms + EPS) * gain) * (1.0 + scale) + shift


def _mod_kernel(c_ref, w_ref, b_ref, o_ref):
    c = c_ref[...]
    act = (c * jax.nn.sigmoid(c)).astype(BF16)
    o_ref[...] = _dot(act, w_ref[...].astype(BF16)) + b_ref[...]


def _modulation(c, w_ada, b_ada):
    n_layers, d, n = w_ada.shape
    b = c.shape[0]
    tn = d
    return pl.pallas_call(
        _mod_kernel,
        grid=(n_layers, n // tn),
        in_specs=[
            pl.BlockSpec((b, d), lambda l, j: (0, 0)),
            pl.BlockSpec((None, d, tn), lambda l, j: (l, 0, j)),
            pl.BlockSpec((None, 1, tn), lambda l, j: (l, 0, j)),
        ],
        out_specs=pl.BlockSpec((None, b, tn), lambda l, j: (l, 0, j)),
        out_shape=jax.ShapeDtypeStruct((n_layers, b, n), F32),
        compiler_params=_params("arbitrary", "arbitrary"),
        name="adaln_mod",
    )(c, w_ada, b_ada.reshape(n_layers, 1, n))


def _swiglu_residual(x, shift, scale, gate, gain, wg_ref, wu_ref, wo_ref, n_chunks):
    hn = _modulated_norm(x, gain, shift, scale).astype(BF16)
    tf = wg_ref.shape[1] // n_chunks
    acc = None
    for j in range(n_chunks):
        sl = slice(j * tf, (j + 1) * tf)
        g = _dot(hn, wg_ref[:, sl])
        up = _dot(hn, wu_ref[:, sl])
        act = (g * jax.nn.sigmoid(g) * up).astype(BF16)
        part = _dot(act, wo_ref[sl, :])
        acc = part if acc is None else acc + part
    return x + (0.5 * (1.0 + gate)) * acc


def _ffn_mixin_kernel(mod_ref, g1_ref, gm_ref, x_ref, wg_ref, wu_ref, wo_ref,
                      watt_ref, wrwkv_ref, wconv_ref, cw_ref,
                      h_ref, patt_ref, prwkv_ref, yconv_ref, carry_ref, *, n_chunks):
    @pl.when(pl.program_id(1) == 0)
    def _():
        carry_ref[...] = jnp.zeros_like(carry_ref)

    h = _swiglu_residual(x_ref[...], mod_ref[0:1, :], mod_ref[1:2, :], mod_ref[2:3, :], g1_ref[...],
                         wg_ref, wu_ref, wo_ref, n_chunks)
    h_ref[...] = h
    hn = _modulated_norm(h, gm_ref[...], mod_ref[3:4, :], mod_ref[4:5, :]).astype(BF16)
    patt_ref[...] = _dot(hn, watt_ref[...]).astype(patt_ref.dtype)
    prwkv_ref[...] = _dot(hn, wrwkv_ref[...])
    pc = _dot(hn, wconv_ref[...])
    cwid = yconv_ref.shape[-1]
    b_gate = pc[:, 0:cwid]
    u = pc[:, cwid:2 * cwid] * pc[:, 2 * cwid:3 * cwid]
    tm = u.shape[0]
    row = lax.broadcasted_iota(jnp.int32, (tm, 1), 0)
    prev1 = carry_ref[1:2, :]
    prev2 = carry_ref[0:1, :]
    u1 = jnp.where(row == 0, prev1, pltpu.roll(u, 1, 0))
    u2 = jnp.where(row == 0, prev2, jnp.where(row == 1, prev1, pltpu.roll(u, 2, 0)))
    y = cw_ref[0:1, :] * u2 + cw_ref[1:2, :] * u1 + cw_ref[2:3, :] * u
    yconv_ref[...] = (b_gate * y).astype(yconv_ref.dtype)
    carry_ref[0:2, :] = u[tm - 2:tm, :]


def _ffn_chunks(d_ff):
    return 2 if d_ff % 256 == 0 else 1


def _ffn_weight_specs(d, d_ff, layer):
    gate = pl.BlockSpec((None, d, d_ff), lambda i, j: (layer, 0, 0), pipeline_mode=pl.Buffered(1))
    up = pl.BlockSpec((None, d, d_ff), lambda i, j: (layer, 0, 1), pipeline_mode=pl.Buffered(1))
    return [gate, up, _layer_spec((d_ff, d), layer)]


def _ffn_mixin(h, mod9, g_ffn, g_mix, w_in, w_out, w_att, w_rwkv, w_conv, conv_w, layer):
    b, t, d = h.shape
    d_ff = w_out.shape[1]
    tm = min(TOKEN_TILE, t)
    tok = lambda width: pl.BlockSpec((None, tm, width), lambda i, j: (i, j, 0))
    return pl.pallas_call(
        functools.partial(_ffn_mixin_kernel, n_chunks=_ffn_chunks(d_ff)),
        grid=(b, t // tm),
        in_specs=[
            pl.BlockSpec((None, N_MOD, d), lambda i, j: (i, 0, 0)),
            _layer_spec((1, d), layer),
            _layer_spec((1, d), layer),
            tok(d),
            *_ffn_weight_specs(d, d_ff, layer),
            _layer_spec((d, ATT_PROJ_WIDTH), layer),
            _layer_spec((d, RWKV_PROJ_WIDTH), layer),
            _layer_spec((d, 3 * CONV_WIDTH), layer),
            _layer_spec((CONV_K, CONV_WIDTH), layer),
        ],
        out_specs=[tok(d), tok(ATT_PROJ_WIDTH), tok(RWKV_PROJ_WIDTH), tok(CONV_WIDTH)],
        out_shape=[
            jax.ShapeDtypeStruct((b, t, d), F32),
            jax.ShapeDtypeStruct((b, t, ATT_PROJ_WIDTH), BF16),
            jax.ShapeDtypeStruct((b, t, RWKV_PROJ_WIDTH), F32),
            jax.ShapeDtypeStruct((b, t, CONV_WIDTH), BF16),
        ],
        scratch_shapes=[pltpu.VMEM((8, CONV_WIDTH), F32)],
        compiler_params=_params("arbitrary", "arbitrary"),
        name="ffn1_mix_in",
    )(mod9, g_ffn, g_mix, h, w_in, w_in, w_out, w_att, w_rwkv, w_conv, conv_w)


def _mixout_ffn_kernel(mod_ref, g2_ref, h_ref, ya_ref, yr_ref, yc_ref, wmix_ref, wg_ref, wu_ref, wo_ref,
                       o_ref, *, n_chunks):
    wa = ATT_WIDTH
    wr = wa + RWKV_WIDTH
    mixed = (_dot(ya_ref[...], wmix_ref[0:wa, :]) + _dot(yr_ref[...], wmix_ref[wa:wr, :])
             + _dot(yc_ref[...], wmix_ref[wr:, :]))
    h = h_ref[...] + (1.0 + mod_ref[5:6, :]) * mixed
    o_ref[...] = _swiglu_residual(h, mod_ref[6:7, :], mod_ref[7:8, :], mod_ref[8:9, :], g2_ref[...],
                                  wg_ref, wu_ref, wo_ref, n_chunks)


def _mixout_ffn(h, mod9, g_ffn, y_att, y_rwkv, y_conv, w_mix_out, w_in, w_out, layer):
    b, t, d = h.shape
    d_ff = w_out.shape[1]
    tm = min(TOKEN_TILE, t)
    tok = lambda width: pl.BlockSpec((None, tm, width), lambda i, j: (i, j, 0))
    return pl.pallas_call(
        functools.partial(_mixout_ffn_kernel, n_chunks=_ffn_chunks(d_ff)),
        grid=(b, t // tm),
        in_specs=[
            pl.BlockSpec((None, N_MOD, d), lambda i, j: (i, 0, 0)),
            _layer_spec((1, d), layer),
            tok(d), tok(ATT_WIDTH), tok(RWKV_WIDTH), tok(CONV_WIDTH),
            _layer_spec((d, d), layer),
            *_ffn_weight_specs(d, d_ff, layer),
        ],
        out_specs=tok(d),
        out_shape=jax.ShapeDtypeStruct((b, t, d), F32),
        compiler_params=_params("arbitrary", "arbitrary"),
        name="mix_out_ffn2",
    )(mod9, g_ffn, h, y_att, y_rwkv, y_conv, w_mix_out, w_in, w_in, w_out)


def _head_rms(x, ones_blk, gain):
    ss = _dot((x * x).astype(BF16), ones_blk)
    return x * lax.rsqrt(ss * (1.0 / HEAD_DIM) + EPS) * gain


def _attn_kernel(sink_ref, qg_ref, kg_ref, bias_ref, onesq_ref, onesk_ref, densel_ref,
                 q_ref, kvc_ref, kvp_ref, o_ref):
    n = pl.program_id(1)
    blk = ATT_BLOCK
    n_sub = q_ref.shape[0] // blk
    rows = 2 * blk
    low = lax.broadcasted_iota(jnp.int32, (1, 2 * HEAD_DIM), 1) < HEAD_DIM
    top = lax.broadcasted_iota(jnp.int32, (rows, 1), 0) < blk

    q = q_ref[...].astype(F32)
    qn = _head_rms(q, onesq_ref[...], qg_ref[...]).astype(BF16)
    kv = jnp.concatenate([kvp_ref[...], kvc_ref[...]], axis=0).astype(F32)
    k = kv[:, 0:ATT_KV_WIDTH]
    v = kv[:, ATT_KV_WIDTH:2 * ATT_KV_WIDTH]
    kn = _head_rms(k, onesk_ref[...], kg_ref[...])
    kr = pltpu.roll(kn, HEAD_DIM, 1)
    vr = pltpu.roll(v, HEAD_DIM, 1)
    zero = jnp.zeros_like(kn)
    bf = lambda z: z.astype(BF16)
    k_low = [bf(jnp.where(low, kn, zero)), bf(jnp.where(low, kr, zero))]
    k_high = [bf(jnp.where(low, zero, kr)), bf(jnp.where(low, zero, kn))]
    v_low = [bf(jnp.where(low, v, zero)), bf(jnp.where(low, vr, zero))]
    v_high = [bf(jnp.where(low, zero, vr)), bf(jnp.where(low, zero, v))]
    den_sel = densel_ref[...]

    for j in range(n_sub):
        bias = bias_ref[jnp.minimum(n, 1)] if j == 0 else bias_ref[1]
        keys = slice(j * blk, (j + 2) * blk)
        for g in range(ATT_KV_HEADS):
            qj = qn[j * blk:(j + 1) * blk]
            qg = jnp.concatenate([qj[:, 256 * g:256 * g + 128], qj[:, 256 * g + 128:256 * g + 256]], axis=0)
            kcat = jnp.concatenate([k_low[g][keys], k_high[g][keys]], axis=0)
            vcat = jnp.concatenate([v_low[g][keys], v_high[g][keys]], axis=0)
            s = _dot_nt(qg, kcat) + bias
            h0 = 4 * g
            sink_e = jnp.where(top, sink_ref[h0], sink_ref[h0 + 2])
            sink_o = jnp.where(top, sink_ref[h0 + 1], sink_ref[h0 + 3])
            s_e = s[:, 0:2 * blk]
            s_o = s[:, 2 * blk:4 * blk]
            m_e = jnp.maximum(jnp.max(s_e, axis=-1, keepdims=True), sink_e)
            m_o = jnp.maximum(jnp.max(s_o, axis=-1, keepdims=True), sink_o)
            p = bf(jnp.concatenate([jnp.exp2(s_e - m_e), jnp.exp2(s_o - m_o)], axis=1))
            nd = _dot(p, jnp.concatenate([vcat, den_sel], axis=1))
            den = nd[:, 2 * HEAD_DIM:] + jnp.where(low, jnp.exp2(sink_e - m_e), jnp.exp2(sink_o - m_o))
            out = (nd[:, 0:2 * HEAD_DIM] / den).astype(o_ref.dtype)
            o_ref[j * blk:(j + 1) * blk, 256 * g:256 * g + 128] = out[0:blk]
            o_ref[j * blk:(j + 1) * blk, 256 * g + 128:256 * g + 256] = out[blk:rows]


def _attention_constants():
    blk = ATT_BLOCK
    ri = jnp.arange(2 * blk)[:, None] % blk
    cj = jnp.arange(4 * blk)[None, :] % (2 * blk)
    band = (cj > ri) & (cj <= ri + blk)
    bias = jnp.stack([jnp.where(band & (cj >= blk), 0.0, NEG_BIG), jnp.where(band, 0.0, NEG_BIG)]).astype(F32)
    ones = lambda n: (jnp.arange(n)[:, None] // HEAD_DIM == jnp.arange(n)[None, :] // HEAD_DIM).astype(BF16)
    den_sel = (jnp.arange(4 * blk)[:, None] // (2 * blk) == jnp.arange(2 * HEAD_DIM)[None, :] // HEAD_DIM)
    return bias, ones(ATT_WIDTH), ones(ATT_KV_WIDTH), den_sel.astype(BF16)


def _attention(p_att, q_gain, k_gain, sinks):
    b, t, _ = p_att.shape
    blk = ATT_BLOCK
    tq = min(ATT_TILE, t)
    n_sub = tq // blk
    tile_gain = lambda g, reps: jnp.tile(g.astype(F32), reps).reshape(1, reps * HEAD_DIM)
    log2e = math.log2(math.e)
    bias, ones_q, ones_k, den_sel = _attention_constants()
    kv_col = ATT_WIDTH // (2 * ATT_KV_WIDTH)
    return pl.pallas_call(
        _attn_kernel,
        grid=(b, t // tq),
        in_specs=[
            pl.BlockSpec(memory_space=pltpu.SMEM),
            _const_spec((1, ATT_WIDTH)),
            _const_spec((1, ATT_KV_WIDTH)),
            _const_spec(bias.shape),
            _const_spec(ones_q.shape),
            _const_spec(ones_k.shape),
            _const_spec(den_sel.shape),
            pl.BlockSpec((None, tq, ATT_WIDTH), lambda i, n: (i, n, 0)),
            pl.BlockSpec((None, tq, 2 * ATT_KV_WIDTH), lambda i, n: (i, n, kv_col)),
            pl.BlockSpec((None, blk, 2 * ATT_KV_WIDTH), lambda i, n: (i, jnp.maximum(n * n_sub - 1, 0), kv_col)),
        ],
        out_specs=pl.BlockSpec((None, tq, ATT_WIDTH), lambda i, n: (i, n, 0)),
        out_shape=jax.ShapeDtypeStruct((b, t, ATT_WIDTH), BF16),
        compiler_params=_params("arbitrary", "arbitrary"),
        name="swa_sink_attention",
    )(sinks.astype(F32) * log2e, tile_gain(q_gain, ATT_Q_HEADS) * (HEAD_DIM ** -0.5 * log2e),
      tile_gain(k_gain, ATT_KV_HEADS), bias, ones_q, ones_k, den_sel, p_att, p_att, p_att)


def _rwkv_kernel(p_ref, mu_ref, w0_ref, ww2_ref, a0_ref, wa2_ref, wg2_ref, kk_ref, ka_ref, rk_ref,
                 gnw_ref, gnb_ref, o_ref, prev_ref, state_ref, *, c_len):
    n_seq, tt, _ = p_ref.shape
    width = RWKV_WIDTH
    n_heads = RWKV_HEADS
    n_ch = tt // c_len
    bf = lambda z: z.astype(BF16)
    each = lambda f, *cols: [f(*args) for args in zip(*cols)]

    @pl.when(pl.program_id(1) == 0)
    def _():
        prev_ref[...] = jnp.zeros_like(prev_ref)
        state_ref[...] = jnp.zeros_like(state_ref)

    ones_head = _block_ones(width, HEAD_DIM)
    ones_head2 = jnp.concatenate([ones_head, ones_head], axis=0)

    def head_sum(z):
        hi = bf(z)
        lo = bf(z - hi.astype(F32))
        return _dot(jnp.concatenate([hi, lo], axis=1), ones_head2)

    tri_r = lax.broadcasted_iota(jnp.int32, (tt, tt), 0)
    tri_c = lax.broadcasted_iota(jnp.int32, (tt, tt), 1)
    tri = jnp.where((tri_c <= tri_r) & (tri_c // c_len == tri_r // c_len), 1.0, 0.0).astype(BF16)
    tri3 = jnp.concatenate([tri, tri, tri], axis=1)
    trow = lax.broadcasted_iota(jnp.int32, (tt, 1), 0)
    lane_head = lax.broadcasted_iota(jnp.int32, (1, width), 1) // HEAD_DIM

    def stack(z):
        zero = jnp.zeros_like(z)
        return jnp.concatenate([jnp.where(lane_head == h, z, zero) for h in range(n_heads)], axis=0)

    sl = [slice(j * c_len, (j + 1) * c_len) for j in range(n_ch)]
    cut = lambda z: [z[s] for s in sl]

    def token_features(s):
        p = p_ref[s]
        p_prev = jnp.where(trow == 0, prev_ref[s, 0:1, :], pltpu.roll(p, 1, 0))
        prev_ref[s, 0:1, :] = p[tt - 1:tt, :]
        xs = p + mu_ref[...] * (p_prev - p)
        r = xs[:, 0:width]
        k = xs[:, width:2 * width]
        v = xs[:, 2 * width:3 * width]
        lora = xs[:, 3 * width:3 * width + RWKV_LORA]
        gate_in = xs[:, 3 * width + RWKV_LORA:]
        dw = _dot(bf(jnp.tanh(lora)), ww2_ref[...])
        da = _dot(bf(lora), wa2_ref[...])
        g = _dot(bf(jax.nn.sigmoid(gate_in)), wg2_ref[...])
        lw = -EXP_M05 * jax.nn.sigmoid(w0_ref[...] + dw)
        a = jax.nn.sigmoid(a0_ref[...] + da)
        kk_raw = k * kk_ref[...]
        kk = kk_raw * lax.rsqrt(jnp.maximum(head_sum(kk_raw * kk_raw), 1e-24))
        kmod = k * (1.0 + (a - 1.0) * ka_ref[...])
        b = kk * a
        lw_hi, lw_mid, lw_lo = _split3(lw)
        lc = _dot(tri3, jnp.concatenate([lw_hi, lw_mid, lw_lo], axis=0))
        w_inv = jnp.exp(-lc)
        lc_c = cut(lc)
        ltot = [z[c_len - 1:c_len, :] for z in lc_c]
        w_end = each(lambda lt, lcj: jnp.exp(lt - lcj), ltot, lc_c)
        chunks = dict(
            ltot=ltot,
            at=cut(bf(kk * jnp.exp(lc - lw))),
            rt=cut(r * jnp.exp(lc)),
            bt=cut(bf(b * w_inv)),
            kt=cut(bf(kmod * w_inv)),
            v=cut(bf(v)),
            bh=each(lambda z, w: bf(z * w), cut(b), w_end),
            kh=each(lambda z, w: bf(z * w), cut(kmod), w_end),
        )
        return chunks, (r, kmod, v, g)

    feats = [token_features(s) for s in range(n_seq)]
    col = lambda name: [z for chunks, _ in feats for z in chunks[name]]
    ltot, at_b, rt_c = col("ltot"), col("at"), col("rt")
    rt_b = each(bf, rt_c)
    bt_bd, kt_bd, v_bd, at_bd = (each(stack, col(nm)) for nm in ("bt", "kt", "v", "at"))
    bh_bd, kh_bd = each(stack, col("bh")), each(stack, col("kh"))

    mi = lax.broadcasted_iota(jnp.int32, (c_len, width), 0)
    mj = lax.broadcasted_iota(jnp.int32, (c_len, width), 1) & (c_len - 1)
    strict = mj < mi
    incl = mj <= mi
    eye = mj == mi
    eye_b = jnp.where(eye, 1.0, 0.0).astype(BF16)

    a_ab = each(lambda x, y: jnp.where(strict, _dot_nt(x, y), 0.0), at_b, bt_bd)
    a_ak = each(lambda x, y: bf(jnp.where(strict, _dot_nt(x, y), 0.0)), at_b, kt_bd)
    a_rb = each(lambda x, y: bf(jnp.where(incl, _dot_nt(x, y), 0.0)), rt_b, bt_bd)
    a_rk = each(lambda x, y: bf(jnp.where(incl, _dot_nt(x, y), 0.0)), rt_b, kt_bd)

    t_inv = each(lambda z: jnp.where(eye, 1.0, 0.0) - z, a_ab)
    pw = each(bf, a_ab)
    pw_bd = each(stack, pw)
    for _ in range(int(math.log2(c_len)) - 1):
        pw = each(lambda x, y: bf(_dot(x, y)), pw, pw_bd)
        pw_bd = each(stack, pw)
        t_inv = each(lambda t, y: t + _dot(bf(t), y), t_inv, pw_bd)
    t_b = each(bf, t_inv)

    y1_bd = each(lambda x, y: stack(bf(_dot(x, y))), a_ak, v_bd)
    at2_bd = each(lambda x, y: stack(bf(_dot(x, y))), t_b, at_bd)
    u2_bd = each(lambda x, y: stack(bf(_dot(x, y))), t_b, y1_bd)
    r2 = each(lambda z, x, y: bf(z - _dot(x, y)), rt_c, a_rb, at2_bd)
    o2 = each(lambda x, y, z, w: _dot(x, y) - _dot(z, w), a_rk, v_bd, a_rb, u2_bd)
    bh_t = each(lambda y: bf(_dot_nt(eye_b, y)), bh_bd)
    kh_t = each(lambda y: bf(_dot_nt(eye_b, y)), kh_bd)
    p_t = each(lambda lt, x, y: bf(jnp.where(eye, jnp.exp(lt), 0.0) - _dot(x, y)), ltot, bh_t, at2_bd)
    q_t = each(lambda x, y, z, w: _dot(x, y) - _dot(z, w), kh_t, v_bd, bh_t, u2_bd)

    states = [state_ref[s] for s in range(n_seq)]
    ys = [[] for _ in range(n_seq)]
    for j in range(n_ch):
        for s in range(n_seq):
            i = s * n_ch + j
            s_bd = stack(bf(states[s]))
            ys[s].append(_dot(r2[i], s_bd) + o2[i])
            states[s] = _dot(p_t[i], s_bd) + q_t[i]

    for s in range(n_seq):
        state_ref[s] = states[s]
        r, kmod, v, g = feats[s][1]
        y = jnp.concatenate(ys[s], axis=0) if n_ch > 1 else ys[s][0]
        mean = head_sum(y) * (1.0 / HEAD_DIM)
        dev = y - mean
        var = head_sum(dev * dev) * (1.0 / HEAD_DIM)
        yn = dev * lax.rsqrt(var + RWKV_GN_EPS) * gnw_ref[...] + gnb_ref[...]
        bonus = head_sum(r * kmod * rk_ref[...]) * v
        o_ref[s] = ((yn + bonus) * g).astype(o_ref.dtype)


def _rwkv(p_rwkv, mu, w0, ww2_pad, a0, wa2_pad, wg2, k_k, k_a, r_k, gn_w, gn_b, layer):
    b, t, pw = p_rwkv.shape
    tt = min(RWKV_TILE, t)
    c_len = min(RWKV_CHUNK, tt)
    n_seq = RWKV_SEQS if b % RWKV_SEQS == 0 else 1
    vec = lambda n: _layer_spec((1, n), layer)
    return pl.pallas_call(
        functools.partial(_rwkv_kernel, c_len=c_len),
        grid=(b // n_seq, t // tt),
        in_specs=[
            pl.BlockSpec((n_seq, tt, pw), lambda i, c: (i, c, 0)),
            vec(pw), vec(RWKV_WIDTH),
            _layer_spec((RWKV_LORA, RWKV_WIDTH), layer),
            vec(RWKV_WIDTH),
            _layer_spec((RWKV_LORA, RWKV_WIDTH), layer),
            _layer_spec((RWKV_LORA, RWKV_WIDTH), layer),
            vec(RWKV_WIDTH), vec(RWKV_WIDTH), vec(RWKV_WIDTH), vec(RWKV_WIDTH), vec(RWKV_WIDTH),
        ],
        out_specs=pl.BlockSpec((n_seq, tt, RWKV_WIDTH), lambda i, c: (i, c, 0)),
        out_shape=jax.ShapeDtypeStruct((b, t, RWKV_WIDTH), BF16),
        scratch_shapes=[pltpu.VMEM((n_seq, 8, pw), F32), pltpu.VMEM((n_seq, HEAD_DIM, RWKV_WIDTH), F32)],
        compiler_params=_params("arbitrary", "arbitrary"),
        name="rwkv7_chunked",
    )(p_rwkv, mu, w0, ww2_pad, a0, wa2_pad, wg2, k_k, k_a, r_k, gn_w, gn_b)


def kernel(x, c, w_ada, b_ada, g_ffn1, w_ffn1_in, w_ffn1_out, g_mix, w_mix_in, w_mix_out, att_q_gain, att_k_gain, att_sinks, rwkv_mu, rwkv_w0, rwkv_w_w2, rwkv_a0, rwkv_a_w2, rwkv_g_w2, rwkv_k_k, rwkv_k_a, rwkv_r_k, rwkv_gn_w, rwkv_gn_b, conv_w, g_ffn2, w_ffn2_in, w_ffn2_out):
    n_layers, d = g_ffn1.shape
    bsz = x.shape[0]
    row3 = lambda z: z.astype(F32).reshape(n_layers, 1, -1)
    bf = lambda z: z.astype(BF16)

    mod = _modulation(c, w_ada, b_ada).reshape(n_layers, bsz, N_MOD, d)

    a_end = ATT_PROJ_WIDTH
    r_end = a_end + RWKV_PROJ_WIDTH
    w_att, w_rwkv, w_conv = bf(w_mix_in[:, :, :a_end]), bf(w_mix_in[:, :, a_end:r_end]), bf(w_mix_in[:, :, r_end:])
    w_mix_out_b = bf(w_mix_out)
    w1_in, w1_out, w2_in, w2_out = bf(w_ffn1_in), bf(w_ffn1_out), bf(w_ffn2_in), bf(w_ffn2_out)
    half = RWKV_LORA // 2
    zeros = jnp.zeros((n_layers, half, RWKV_WIDTH), F32)
    ww2_pad = bf(jnp.concatenate([rwkv_w_w2, zeros], axis=1))
    wa2_pad = bf(jnp.concatenate([zeros, rwkv_a_w2], axis=1))
    wg2 = bf(rwkv_g_w2)
    g1, gm, g2 = row3(g_ffn1), row3(g_mix), row3(g_ffn2)
    mu, w0, a0 = row3(rwkv_mu), row3(rwkv_w0), row3(rwkv_a0)
    k_k, k_a, r_k = row3(rwkv_k_k), row3(rwkv_k_a), row3(rwkv_r_k)
    gn_w, gn_b = row3(rwkv_gn_w), row3(rwkv_gn_b)

    h = x
    for l in range(n_layers):
        h, p_att, p_rwkv, y_conv = _ffn_mixin(h, mod[l], g1, gm, w1_in, w1_out, w_att, w_rwkv, w_conv, conv_w, l)
        y_att = _attention(p_att, att_q_gain[l], att_k_gain[l], att_sinks[l])
        y_rwkv = _rwkv(p_rwkv, mu, w0, ww2_pad, a0, wa2_pad, wg2, k_k, k_a, r_k, gn_w, gn_b, l)
        h = _mixout_ffn(h, mod[l], g2, y_att, y_rwkv, y_conv, w_mix_out_b, w2_in, w2_out, l)
    return h
```

```python
import functools
import math

import jax
import jax.numpy as jnp
from jax import lax
from jax.experimental import pallas as pl
from jax.experimental.pallas import tpu as pltpu

F32 = jnp.float32
BF16 = jnp.bfloat16

HEAD_DIM = 64
ATT_Q_HEADS = 8
ATT_KV_HEADS = 2
ATT_WIDTH = ATT_Q_HEADS * HEAD_DIM
ATT_KV_WIDTH = ATT_KV_HEADS * HEAD_DIM
ATT_PROJ_WIDTH = ATT_WIDTH + 2 * ATT_KV_WIDTH
ATT_BLOCK = 128
ATT_TILE = 512
RWKV_HEADS = 4
RWKV_WIDTH = RWKV_HEADS * HEAD_DIM
RWKV_LORA = 128
RWKV_PROJ_WIDTH = 3 * RWKV_WIDTH + 2 * RWKV_LORA
RWKV_GN_EPS = 64e-5
RWKV_CHUNK = 64
RWKV_TILE = 256
RWKV_SEQS = 2
CONV_WIDTH = 256
CONV_K = 3
N_MOD = 9
EPS = 1e-6
NEG_BIG = -1e30
EXP_M05 = math.exp(-0.5)

MXU_WIDTH = 256
TOKEN_TILE = 512
VMEM_LIMIT = 56 * 1024 * 1024


def _dot(a, b):
    return jnp.dot(a, b, preferred_element_type=F32)


def _dot_nt(a, b):
    return lax.dot_general(a, b, (((1,), (1,)), ((), ())), preferred_element_type=F32)


def _split3(x):
    hi = x.astype(BF16)
    r1 = x - hi.astype(F32)
    mid = r1.astype(BF16)
    lo = (r1 - mid.astype(F32)).astype(BF16)
    return hi, mid, lo


def _block_ones(n, blk):
    r = lax.broadcasted_iota(jnp.int32, (n, n), 0) // blk
    c = lax.broadcasted_iota(jnp.int32, (n, n), 1) // blk
    return jnp.where(r == c, 1.0, 0.0).astype(BF16)


def _const_spec(shape):
    nd = len(shape)
    return pl.BlockSpec(shape, lambda *_: (0,) * nd, pipeline_mode=pl.Buffered(1))


def _layer_spec(shape, layer):
    nd = len(shape)
    return pl.BlockSpec((None,) + tuple(shape), lambda *_: (layer,) + (0,) * nd,
                        pipeline_mode=pl.Buffered(1))


def _params(*sem):
    return pltpu.CompilerParams(dimension_semantics=sem, vmem_limit_bytes=VMEM_LIMIT)


def _modulated_norm(x, gain, shift, scale):
    ms = jnp.mean(x * x, axis=-1, keepdims=True)
    return (x * lax.rsqrt(ms + EPS) * gain) * (1.0 + scale) + shift


def _mod_kernel(c_ref, w_ref, b_ref, o_ref):
    c = c_ref[...]
    act = (c * jax.nn.sigmoid(c)).astype(BF16)
    o_ref[...] = _dot(act, w_ref[...].astype(BF16)) + b_ref[...]


def _modulation(c, w_ada, b_ada):
    n_layers, d, n = w_ada.shape
    b = c.shape[0]
    tn = d
    return pl.pallas_call(
        _mod_kernel,
        grid=(n_layers, n // tn),
        in_specs=[
            pl.BlockSpec((b, d), lambda l, j: (0, 0)),
            pl.BlockSpec((None, d, tn), lambda l, j: (l, 0, j)),
            pl.BlockSpec((None, 1, tn), lambda l, j: (l, 0, j)),
        ],
        out_specs=pl.BlockSpec((None, b, tn), lambda l, j: (l, 0, j)),
        out_shape=jax.ShapeDtypeStruct((n_layers, b, n), F32),
        compiler_params=_params("arbitrary", "arbitrary"),
        name="adaln_mod",
    )(c, w_ada, b_ada.reshape(n_layers, 1, n))


def _swiglu_residual(x, shift, scale, gate, gain, wg_ref, wu_ref, wo_ref, n_chunks):
    hn = _modulated_norm(x, gain, shift, scale).astype(BF16)
    d_ff = wg_ref.shape[1]
    n_tiles = -(-d_ff // MXU_WIDTH)
    edges = [min(d_ff, MXU_WIDTH * ((n_tiles * j + n_chunks - 1) // n_chunks)) for j in range(n_chunks + 1)]
    acc = None
    for j in range(n_chunks):
        sl = slice(edges[j], edges[j + 1])
        g = _dot(hn, wg_ref[:, sl])
        up = _dot(hn, wu_ref[:, sl])
        act = (g * jax.nn.sigmoid(g) * up).astype(BF16)
        part = _dot(act, wo_ref[sl, :])
        acc = part if acc is None else acc + part
    return x + (0.5 * (1.0 + gate)) * acc


def _ffn_mixin_kernel(mod_ref, g1_ref, gm_ref, x_ref, wg_ref, wu_ref, wo_ref,
                      watt_ref, wrwkv_ref, wconv_ref, cw_ref,
                      h_ref, patt_ref, prwkv_ref, yconv_ref, carry_ref, *, n_chunks):
    @pl.when(pl.program_id(1) == 0)
    def _():
        carry_ref[...] = jnp.zeros_like(carry_ref)

    h = _swiglu_residual(x_ref[...], mod_ref[0:1, :], mod_ref[1:2, :], mod_ref[2:3, :], g1_ref[...],
                         wg_ref, wu_ref, wo_ref, n_chunks)
    h_ref[...] = h
    hn = _modulated_norm(h, gm_ref[...], mod_ref[3:4, :], mod_ref[4:5, :]).astype(BF16)
    patt_ref[...] = _dot(hn, watt_ref[...]).astype(patt_ref.dtype)
    prwkv_ref[...] = _dot(hn, wrwkv_ref[...])
    pc = _dot(hn, wconv_ref[...])
    cwid = yconv_ref.shape[-1]
    b_gate = pc[:, 0:cwid]
    u = pc[:, cwid:2 * cwid] * pc[:, 2 * cwid:3 * cwid]
    tm = u.shape[0]
    row = lax.broadcasted_iota(jnp.int32, (tm, 1), 0)
    prev1 = carry_ref[1:2, :]
    prev2 = carry_ref[0:1, :]
    u1 = jnp.where(row == 0, prev1, pltpu.roll(u, 1, 0))
    u2 = jnp.where(row == 0, prev2, jnp.where(row == 1, prev1, pltpu.roll(u, 2, 0)))
    y = cw_ref[0:1, :] * u2 + cw_ref[1:2, :] * u1 + cw_ref[2:3, :] * u
    yconv_ref[...] = (b_gate * y).astype(yconv_ref.dtype)
    carry_ref[0:2, :] = u[tm - 2:tm, :]


def _ffn_chunks(d_ff):
    return 2 if d_ff % 256 == 0 else 1


def _ffn_weight_specs(d, d_ff, layer):
    gate = pl.BlockSpec((None, d, d_ff), lambda i, j: (layer, 0, 0), pipeline_mode=pl.Buffered(1))
    up = pl.BlockSpec((None, d, d_ff), lambda i, j: (layer, 0, 1), pipeline_mode=pl.Buffered(1))
    return [gate, up, _layer_spec((d_ff, d), layer)]


def _ffn_mixin(h, mod9, g_ffn, g_mix, w_in, w_out, w_att, w_rwkv, w_conv, conv_w, layer):
    b, t, d = h.shape
    d_ff = w_out.shape[1]
    tm = min(TOKEN_TILE, t)
    tok = lambda width: pl.BlockSpec((None, tm, width), lambda i, j: (i, j, 0))
    return pl.pallas_call(
        functools.partial(_ffn_mixin_kernel, n_chunks=_ffn_chunks(d_ff)),
        grid=(b, t // tm),
        in_specs=[
            pl.BlockSpec((None, N_MOD, d), lambda i, j: (i, 0, 0)),
            _layer_spec((1, d), layer),
            _layer_spec((1, d), layer),
            tok(d),
            *_ffn_weight_specs(d, d_ff, layer),
            _layer_spec((d, ATT_PROJ_WIDTH), layer),
            _layer_spec((d, RWKV_PROJ_WIDTH), layer),
            _layer_spec((d, 3 * CONV_WIDTH), layer),
            _layer_spec((CONV_K, CONV_WIDTH), layer),
        ],
        out_specs=[tok(d), tok(ATT_PROJ_WIDTH), tok(RWKV_PROJ_WIDTH), tok(CONV_WIDTH)],
        out_shape=[
            jax.ShapeDtypeStruct((b, t, d), F32),
            jax.ShapeDtypeStruct((b, t, ATT_PROJ_WIDTH), BF16),
            jax.ShapeDtypeStruct((b, t, RWKV_PROJ_WIDTH), F32),
            jax.ShapeDtypeStruct((b, t, CONV_WIDTH), BF16),
        ],
        scratch_shapes=[pltpu.VMEM((8, CONV_WIDTH), F32)],
        compiler_params=_params("arbitrary", "arbitrary"),
        name="ffn1_mix_in",
    )(mod9, g_ffn, g_mix, h, w_in, w_in, w_out, w_att, w_rwkv, w_conv, conv_w)


def _mixout_ffn_kernel(mod_ref, g2_ref, h_ref, ya_ref, yr_ref, yc_ref, wmix_ref, wg_ref, wu_ref, wo_ref,
                       o_ref, *, n_chunks):
    wa = ATT_WIDTH
    wr = wa + RWKV_WIDTH
    mixed = (_dot(ya_ref[...], wmix_ref[0:wa, :]) + _dot(yr_ref[...], wmix_ref[wa:wr, :])
             + _dot(yc_ref[...], wmix_ref[wr:, :]))
    h = h_ref[...] + (1.0 + mod_ref[5:6, :]) * mixed
    o_ref[...] = _swiglu_residual(h, mod_ref[6:7, :], mod_ref[7:8, :], mod_ref[8:9, :], g2_ref[...],
                                  wg_ref, wu_ref, wo_ref, n_chunks)


def _mixout_ffn(h, mod9, g_ffn, y_att, y_rwkv, y_conv, w_mix_out, w_in, w_out, layer):
    b, t, d = h.shape
    d_ff = w_out.shape[1]
    tm = min(TOKEN_TILE, t)
    tok = lambda width: pl.BlockSpec((None, tm, width), lambda i, j: (i, j, 0))
    return pl.pallas_call(
        functools.partial(_mixout_ffn_kernel, n_chunks=_ffn_chunks(d_ff)),
        grid=(b, t // tm),
        in_specs=[
            pl.BlockSpec((None, N_MOD, d), lambda i, j: (i, 0, 0)),
            _layer_spec((1, d), layer),
            tok(d), tok(ATT_WIDTH), tok(RWKV_WIDTH), tok(CONV_WIDTH),
            _layer_spec((d, d), layer),
            *_ffn_weight_specs(d, d_ff, layer),
        ],
        out_specs=tok(d),
        out_shape=jax.ShapeDtypeStruct((b, t, d), F32),
        compiler_params=_params("arbitrary", "arbitrary"),
        name="mix_out_ffn2",
    )(mod9, g_ffn, h, y_att, y_rwkv, y_conv, w_mix_out, w_in, w_in, w_out)


def _head_rms(x, ones_blk, gain):
    ss = _dot((x * x).astype(BF16), ones_blk)
    return x * lax.rsqrt(ss * (1.0 / HEAD_DIM) + EPS) * gain


def _attn_kernel(sink_ref, qg_ref, kg_ref, bias_ref, onesq_ref, onesk_ref, densel_ref,
                 q_ref, kvc_ref, kvp_ref, o_ref):
    n = pl.program_id(1)
    blk = ATT_BLOCK
    n_sub = q_ref.shape[0] // blk
    rows = 2 * blk
    low = lax.broadcasted_iota(jnp.int32, (1, 2 * HEAD_DIM), 1) < HEAD_DIM
    top = lax.broadcasted_iota(jnp.int32, (rows, 1), 0) < blk

    q = q_ref[...].astype(F32)
    qn = _head_rms(q, onesq_ref[...], qg_ref[...]).astype(BF16)
    kv = jnp.concatenate([kvp_ref[...], kvc_ref[...]], axis=0).astype(F32)
    k = kv[:, 0:ATT_KV_WIDTH]
    v = kv[:, ATT_KV_WIDTH:2 * ATT_KV_WIDTH]
    kn = _head_rms(k, onesk_ref[...], kg_ref[...])
    kr = pltpu.roll(kn, HEAD_DIM, 1)
    vr = pltpu.roll(v, HEAD_DIM, 1)
    zero = jnp.zeros_like(kn)
    bf = lambda z: z.astype(BF16)
    k_low = [bf(jnp.where(low, kn, zero)), bf(jnp.where(low, kr, zero))]
    k_high = [bf(jnp.where(low, zero, kr)), bf(jnp.where(low, zero, kn))]
    v_low = [bf(jnp.where(low, v, zero)), bf(jnp.where(low, vr, zero))]
    v_high = [bf(jnp.where(low, zero, vr)), bf(jnp.where(low, zero, v))]
    den_sel = densel_ref[...]

    for j in range(n_sub):
        bias = bias_ref[jnp.minimum(n, 1)] if j == 0 else bias_ref[1]
        keys = slice(j * blk, (j + 2) * blk)
        for g in range(ATT_KV_HEADS):
            qj = qn[j * blk:(j + 1) * blk]
            qg = jnp.concatenate([qj[:, 256 * g:256 * g + 128], qj[:, 256 * g + 128:256 * g + 256]], axis=0)
            kcat = jnp.concatenate([k_low[g][keys], k_high[g][keys]], axis=0)
            vcat = jnp.concatenate([v_low[g][keys], v_high[g][keys]], axis=0)
            s = _dot_nt(qg, kcat) + bias
            h0 = 4 * g
            sink_e = jnp.where(top, sink_ref[h0], sink_ref[h0 + 2])
            sink_o = jnp.where(top, sink_ref[h0 + 1], sink_ref[h0 + 3])
            s_e = s[:, 0:2 * blk]
            s_o = s[:, 2 * blk:4 * blk]
            m_e = jnp.maximum(jnp.max(s_e, axis=-1, keepdims=True), sink_e)
            m_o = jnp.maximum(jnp.max(s_o, axis=-1, keepdims=True), sink_o)
            p = bf(jnp.concatenate([jnp.exp2(s_e - m_e), jnp.exp2(s_o - m_o)], axis=1))
            nd = _dot(p, jnp.concatenate([vcat, den_sel], axis=1))
            den = nd[:, 2 * HEAD_DIM:] + jnp.where(low, jnp.exp2(sink_e - m_e), jnp.exp2(sink_o - m_o))
            out = (nd[:, 0:2 * HEAD_DIM] / den).astype(o_ref.dtype)
            o_ref[j * blk:(j + 1) * blk, 256 * g:256 * g + 128] = out[0:blk]
            o_ref[j * blk:(j + 1) * blk, 256 * g + 128:256 * g + 256] = out[blk:rows]


def _attention_constants():
    blk = ATT_BLOCK
    ri = jnp.arange(2 * blk)[:, None] % blk
    cj = jnp.arange(4 * blk)[None, :] % (2 * blk)
    band = (cj > ri) & (cj <= ri + blk)
    bias = jnp.stack([jnp.where(band & (cj >= blk), 0.0, NEG_BIG), jnp.where(band, 0.0, NEG_BIG)]).astype(F32)
    ones = lambda n: (jnp.arange(n)[:, None] // HEAD_DIM == jnp.arange(n)[None, :] // HEAD_DIM).astype(BF16)
    den_sel = (jnp.arange(4 * blk)[:, None] // (2 * blk) == jnp.arange(2 * HEAD_DIM)[None, :] // HEAD_DIM)
    return bias, ones(ATT_WIDTH), ones(ATT_KV_WIDTH), den_sel.astype(BF16)


def _attention(p_att, q_gain, k_gain, sinks):
    b, t, _ = p_att.shape
    blk = ATT_BLOCK
    tq = min(ATT_TILE, t)
    n_sub = tq // blk
    tile_gain = lambda g, reps: jnp.tile(g.astype(F32), reps).reshape(1, reps * HEAD_DIM)
    log2e = math.log2(math.e)
    bias, ones_q, ones_k, den_sel = _attention_constants()
    kv_col = ATT_WIDTH // (2 * ATT_KV_WIDTH)
    return pl.pallas_call(
        _attn_kernel,
        grid=(b, t // tq),
        in_specs=[
            pl.BlockSpec(memory_space=pltpu.SMEM),
            _const_spec((1, ATT_WIDTH)),
            _const_spec((1, ATT_KV_WIDTH)),
            _const_spec(bias.shape),
            _const_spec(ones_q.shape),
            _const_spec(ones_k.shape),
            _const_spec(den_sel.shape),
            pl.BlockSpec((None, tq, ATT_WIDTH), lambda i, n: (i, n, 0)),
            pl.BlockSpec((None, tq, 2 * ATT_KV_WIDTH), lambda i, n: (i, n, kv_col)),
            pl.BlockSpec((None, blk, 2 * ATT_KV_WIDTH), lambda i, n: (i, jnp.maximum(n * n_sub - 1, 0), kv_col)),
        ],
        out_specs=pl.BlockSpec((None, tq, ATT_WIDTH), lambda i, n: (i, n, 0)),
        out_shape=jax.ShapeDtypeStruct((b, t, ATT_WIDTH), BF16),
        compiler_params=_params("arbitrary", "arbitrary"),
        name="swa_sink_attention",
    )(sinks.astype(F32) * log2e, tile_gain(q_gain, ATT_Q_HEADS) * (HEAD_DIM ** -0.5 * log2e),
      tile_gain(k_gain, ATT_KV_HEADS), bias, ones_q, ones_k, den_sel, p_att, p_att, p_att)


def _rwkv_kernel(p_ref, mu_ref, w0_ref, ww2_ref, a0_ref, wa2_ref, wg2_ref, kk_ref, ka_ref, rk_ref,
                 gnw_ref, gnb_ref, o_ref, prev_ref, state_ref, *, c_len):
    n_seq, tt, _ = p_ref.shape
    width = RWKV_WIDTH
    n_heads = RWKV_HEADS
    n_ch = tt // c_len
    bf = lambda z: z.astype(BF16)
    each = lambda f, *cols: [f(*args) for args in zip(*cols)]

    @pl.when(pl.program_id(1) == 0)
    def _():
        prev_ref[...] = jnp.zeros_like(prev_ref)
        state_ref[...] = jnp.zeros_like(state_ref)

    ones_head = _block_ones(width, HEAD_DIM)
    ones_head2 = jnp.concatenate([ones_head, ones_head], axis=0)

    def head_sum(z):
        hi = bf(z)
        lo = bf(z - hi.astype(F32))
        return _dot(jnp.concatenate([hi, lo], axis=1), ones_head2)

    tri_r = lax.broadcasted_iota(jnp.int32, (tt, tt), 0)
    tri_c = lax.broadcasted_iota(jnp.int32, (tt, tt), 1)
    tri = jnp.where((tri_c <= tri_r) & (tri_c // c_len == tri_r // c_len), 1.0, 0.0).astype(BF16)
    tri3 = jnp.concatenate([tri, tri, tri], axis=1)
    trow = lax.broadcasted_iota(jnp.int32, (tt, 1), 0)
    lane_head = lax.broadcasted_iota(jnp.int32, (1, width), 1) // HEAD_DIM

    def stack(z):
        zero = jnp.zeros_like(z)
        return jnp.concatenate([jnp.where(lane_head == h, z, zero) for h in range(n_heads)], axis=0)

    sl = [slice(j * c_len, (j + 1) * c_len) for j in range(n_ch)]
    cut = lambda z: [z[s] for s in sl]

    def token_features(s):
        p = p_ref[s]
        p_prev = jnp.where(trow == 0, prev_ref[s, 0:1, :], pltpu.roll(p, 1, 0))
        prev_ref[s, 0:1, :] = p[tt - 1:tt, :]
        xs = p + mu_ref[...] * (p_prev - p)
        r = xs[:, 0:width]
        k = xs[:, width:2 * width]
        v = xs[:, 2 * width:3 * width]
        lora = xs[:, 3 * width:3 * width + RWKV_LORA]
        gate_in = xs[:, 3 * width + RWKV_LORA:]
        dw = _dot(bf(jnp.tanh(lora)), ww2_ref[...])
        da = _dot(bf(lora), wa2_ref[...])
        g = _dot(bf(jax.nn.sigmoid(gate_in)), wg2_ref[...])
        lw = -EXP_M05 * jax.nn.sigmoid(w0_ref[...] + dw)
        a = jax.nn.sigmoid(a0_ref[...] + da)
        kk_raw = k * kk_ref[...]
        kk = kk_raw * lax.rsqrt(jnp.maximum(head_sum(kk_raw * kk_raw), 1e-24))
        kmod = k * (1.0 + (a - 1.0) * ka_ref[...])
        b = kk * a
        lw_hi, lw_mid, lw_lo = _split3(lw)
        lc = _dot(tri3, jnp.concatenate([lw_hi, lw_mid, lw_lo], axis=0))
        w_inv = jnp.exp(-lc)
        lc_c = cut(lc)
        ltot = [z[c_len - 1:c_len, :] for z in lc_c]
        w_end = each(lambda lt, lcj: jnp.exp(lt - lcj), ltot, lc_c)
        chunks = dict(
            ltot=ltot,
            at=cut(bf(kk * jnp.exp(lc - lw))),
            rt=cut(r * jnp.exp(lc)),
            bt=cut(bf(b * w_inv)),
            kt=cut(bf(kmod * w_inv)),
            v=cut(bf(v)),
            bh=each(lambda z, w: bf(z * w), cut(b), w_end),
            kh=each(lambda z, w: bf(z * w), cut(kmod), w_end),
        )
        return chunks, (r, kmod, v, g)

    feats = [token_features(s) for s in range(n_seq)]
    col = lambda name: [z for chunks, _ in feats for z in chunks[name]]
    ltot, at_b, rt_c = col("ltot"), col("at"), col("rt")
    rt_b = each(bf, rt_c)
    bt_bd, kt_bd, v_bd, at_bd = (each(stack, col(nm)) for nm in ("bt", "kt", "v", "at"))
    bh_bd, kh_bd = each(stack, col("bh")), each(stack, col("kh"))

    mi = lax.broadcasted_iota(jnp.int32, (c_len, width), 0)
    mj = lax.broadcasted_iota(jnp.int32, (c_len, width), 1) & (c_len - 1)
    strict = mj < mi
    incl = mj <= mi
    eye = mj == mi
    eye_b = jnp.where(eye, 1.0, 0.0).astype(BF16)

    a_ab = each(lambda x, y: jnp.where(strict, _dot_nt(x, y), 0.0), at_b, bt_bd)
    a_ak = each(lambda x, y: bf(jnp.where(strict, _dot_nt(x, y), 0.0)), at_b, kt_bd)
    a_rb = each(lambda x, y: bf(jnp.where(incl, _dot_nt(x, y), 0.0)), rt_b, bt_bd)
    a_rk = each(lambda x, y: bf(jnp.where(incl, _dot_nt(x, y), 0.0)), rt_b, kt_bd)

    t_inv = each(lambda z: jnp.where(eye, 1.0, 0.0) - z, a_ab)
    pw = each(bf, a_ab)
    pw_bd = each(stack, pw)
    for _ in range(int(math.log2(c_len)) - 1):
        pw = each(lambda x, y: bf(_dot(x, y)), pw, pw_bd)
        pw_bd = each(stack, pw)
        t_inv = each(lambda t, y: t + _dot(bf(t), y), t_inv, pw_bd)
    t_b = each(bf, t_inv)

    y1_bd = each(lambda x, y: stack(bf(_dot(x, y))), a_ak, v_bd)
    at2_bd = each(lambda x, y: stack(bf(_dot(x, y))), t_b, at_bd)
    u2_bd = each(lambda x, y: stack(bf(_dot(x, y))), t_b, y1_bd)
    r2 = each(lambda z, x, y: bf(z - _dot(x, y)), rt_c, a_rb, at2_bd)
    o2 = each(lambda x, y, z, w: _dot(x, y) - _dot(z, w), a_rk, v_bd, a_rb, u2_bd)
    bh_t = each(lambda y: bf(_dot_nt(eye_b, y)), bh_bd)
    kh_t = each(lambda y: bf(_dot_nt(eye_b, y)), kh_bd)
    p_t = each(lambda lt, x, y: bf(jnp.where(eye, jnp.exp(lt), 0.0) - _dot(x, y)), ltot, bh_t, at2_bd)
    q_t = each(lambda x, y, z, w: _dot(x, y) - _dot(z, w), kh_t, v_bd, bh_t, u2_bd)

    states = [state_ref[s] for s in range(n_seq)]
    ys = [[] for _ in range(n_seq)]
    for j in range(n_ch):
        for s in range(n_seq):
            i = s * n_ch + j
            s_bd = stack(bf(states[s]))
            ys[s].append(_dot(r2[i], s_bd) + o2[i])
            states[s] = _dot(p_t[i], s_bd) + q_t[i]

    for s in range(n_seq):
        state_ref[s] = states[s]
        r, kmod, v, g = feats[s][1]
        y = jnp.concatenate(ys[s], axis=0) if n_ch > 1 else ys[s][0]
        mean = head_sum(y) * (1.0 / HEAD_DIM)
        dev = y - mean
        var = head_sum(dev * dev) * (1.0 / HEAD_DIM)
        yn = dev * lax.rsqrt(var + RWKV_GN_EPS) * gnw_ref[...] + gnb_ref[...]
        bonus = head_sum(r * kmod * rk_ref[...]) * v
        o_ref[s] = ((yn + bonus) * g).astype(o_ref.dtype)


def _rwkv(p_rwkv, mu, w0, ww2_pad, a0, wa2_pad, wg2, k_k, k_a, r_k, gn_w, gn_b, layer):
    b, t, pw = p_rwkv.shape
    tt = min(RWKV_TILE, t)
    c_len = min(RWKV_CHUNK, tt)
    n_seq = RWKV_SEQS if b % RWKV_SEQS == 0 else 1
    vec = lambda n: _layer_spec((1, n), layer)
    return pl.pallas_call(
        functools.partial(_rwkv_kernel, c_len=c_len),
        grid=(b // n_seq, t // tt),
        in_specs=[
            pl.BlockSpec((n_seq, tt, pw), lambda i, c: (i, c, 0)),
            vec(pw), vec(RWKV_WIDTH),
            _layer_spec((RWKV_LORA, RWKV_WIDTH), layer),
            vec(RWKV_WIDTH),
            _layer_spec((RWKV_LORA, RWKV_WIDTH), layer),
            _layer_spec((RWKV_LORA, RWKV_WIDTH), layer),
            vec(RWKV_WIDTH), vec(RWKV_WIDTH), vec(RWKV_WIDTH), vec(RWKV_WIDTH), vec(RWKV_WIDTH),
        ],
        out_specs=pl.BlockSpec((n_seq, tt, RWKV_WIDTH), lambda i, c: (i, c, 0)),
        out_shape=jax.ShapeDtypeStruct((b, t, RWKV_WIDTH), BF16),
        scratch_shapes=[pltpu.VMEM((n_seq, 8, pw), F32), pltpu.VMEM((n_seq, HEAD_DIM, RWKV_WIDTH), F32)],
        compiler_params=_params("arbitrary", "arbitrary"),
        name="rwkv7_chunked",
    )(p_rwkv, mu, w0, ww2_pad, a0, wa2_pad, wg2, k_k, k_a, r_k, gn_w, gn_b)


def kernel(x, c, w_ada, b_ada, g_ffn1, w_ffn1_in, w_ffn1_out, g_mix, w_mix_in, w_mix_out, att_q_gain, att_k_gain, att_sinks, rwkv_mu, rwkv_w0, rwkv_w_w2, rwkv_a0, rwkv_a_w2, rwkv_g_w2, rwkv_k_k, rwkv_k_a, rwkv_r_k, rwkv_gn_w, rwkv_gn_b, conv_w, g_ffn2, w_ffn2_in, w_ffn2_out):
    n_layers, d = g_ffn1.shape
    bsz = x.shape[0]
    row3 = lambda z: z.astype(F32).reshape(n_layers, 1, -1)
    bf = lambda z: z.astype(BF16)

    mod = _modulation(c, w_ada, b_ada).reshape(n_layers, bsz, N_MOD, d)

    a_end = ATT_PROJ_WIDTH
    r_end = a_end + RWKV_PROJ_WIDTH
    w_att, w_rwkv, w_conv = bf(w_mix_in[:, :, :a_end]), bf(w_mix_in[:, :, a_end:r_end]), bf(w_mix_in[:, :, r_end:])
    w_mix_out_b = bf(w_mix_out)
    w1_in, w1_out, w2_in, w2_out = bf(w_ffn1_in), bf(w_ffn1_out), bf(w_ffn2_in), bf(w_ffn2_out)
    half = RWKV_LORA // 2
    zeros = jnp.zeros((n_layers, half, RWKV_WIDTH), F32)
    ww2_pad = bf(jnp.concatenate([rwkv_w_w2, zeros], axis=1))
    wa2_pad = bf(jnp.concatenate([zeros, rwkv_a_w2], axis=1))
    wg2 = bf(rwkv_g_w2)
    g1, gm, g2 = row3(g_ffn1), row3(g_mix), row3(g_ffn2)
    mu, w0, a0 = row3(rwkv_mu), row3(rwkv_w0), row3(rwkv_a0)
    k_k, k_a, r_k = row3(rwkv_k_k), row3(rwkv_k_a), row3(rwkv_r_k)
    gn_w, gn_b = row3(rwkv_gn_w), row3(rwkv_gn_b)

    h = x
    for l in range(n_layers):
        h, p_att, p_rwkv, y_conv = _ffn_mixin(h, mod[l], g1, gm, w1_in, w1_out, w_att, w_rwkv, w_conv, conv_w, l)
        y_att = _attention(p_att, att_q_gain[l], att_k_gain[l], att_sinks[l])
        y_rwkv = _rwkv(p_rwkv, mu, w0, ww2_pad, a0, wa2_pad, wg2, k_k, k_a, r_k, gn_w, gn_b, l)
        h = _mixout_ffn(h, mod[l], g2, y_att, y_rwkv, y_conv, w_mix_out_b, w2_in, w2_out, l)
    return h
```

```python
import functools
import math

import jax
import jax.numpy as jnp
from jax import lax
from jax.experimental import pallas as pl
from jax.experimental.pallas import tpu as pltpu

F32 = jnp.float32
BF16 = jnp.bfloat16

HEAD_DIM = 64
ATT_Q_HEADS = 8
ATT_KV_HEADS = 2
ATT_WIDTH = ATT_Q_HEADS * HEAD_DIM
ATT_KV_WIDTH = ATT_KV_HEADS * HEAD_DIM
ATT_PROJ_WIDTH = ATT_WIDTH + 2 * ATT_KV_WIDTH
ATT_BLOCK = 128
ATT_TILE = 512
RWKV_HEADS = 4
RWKV_WIDTH = RWKV_HEADS * HEAD_DIM
RWKV_LORA = 128
RWKV_PROJ_WIDTH = 3 * RWKV_WIDTH + 2 * RWKV_LORA
RWKV_GN_EPS = 64e-5
RWKV_CHUNK = 64
RWKV_TILE = 256
RWKV_SEQS = 2
CONV_WIDTH = 256
CONV_K = 3
N_MOD = 9
EPS = 1e-6
NEG_BIG = -1e30
EXP_M05 = math.exp(-0.5)

MXU_WIDTH = 256
TOKEN_TILE = 512
VMEM_LIMIT = 56 * 1024 * 1024


def _dot(a, b):
    return jnp.dot(a, b, preferred_element_type=F32)


def _dot_nt(a, b):
    return lax.dot_general(a, b, (((1,), (1,)), ((), ())), preferred_element_type=F32)


def _split3(x):
    hi = x.astype(BF16)
    r1 = x - hi.astype(F32)
    mid = r1.astype(BF16)
    lo = (r1 - mid.astype(F32)).astype(BF16)
    return hi, mid, lo


def _block_ones(n, blk):
    r = lax.broadcasted_iota(jnp.int32, (n, n), 0) // blk
    c = lax.broadcasted_iota(jnp.int32, (n, n), 1) // blk
    return jnp.where(r == c, 1.0, 0.0).astype(BF16)


def _const_spec(shape):
    nd = len(shape)
    return pl.BlockSpec(shape, lambda *_: (0,) * nd, pipeline_mode=pl.Buffered(1))


def _layer_spec(shape, layer):
    nd = len(shape)
    return pl.BlockSpec((None,) + tuple(shape), lambda *_: (layer,) + (0,) * nd,
                        pipeline_mode=pl.Buffered(1))


def _params(*sem):
    return pltpu.CompilerParams(dimension_semantics=sem, vmem_limit_bytes=VMEM_LIMIT)


def _modulated_norm(x, gain, shift, scale):
    ms = jnp.mean(x * x, axis=-1, keepdims=True)
    return (x * lax.rsqrt(ms + EPS) * gain) * (1.0 + scale) + shift


def _mod_kernel(c_ref, w_ref, b_ref, o_ref):
    c = c_ref[...]
    act = (c * jax.nn.sigmoid(c)).astype(BF16)
    o_ref[...] = _dot(act, w_ref[...].astype(BF16)) + b_ref[...]


def _modulation(c, w_ada, b_ada):
    n_layers, d, n = w_ada.shape
    b = c.shape[0]
    tn = d
    return pl.pallas_call(
        _mod_kernel,
        grid=(n_layers, n // tn),
        in_specs=[
            pl.BlockSpec((b, d), lambda l, j: (0, 0)),
            pl.BlockSpec((None, d, tn), lambda l, j: (l, 0, j)),
            pl.BlockSpec((None, 1, tn), lambda l, j: (l, 0, j)),
        ],
        out_specs=pl.BlockSpec((None, b, tn), lambda l, j: (l, 0, j)),
        out_shape=jax.ShapeDtypeStruct((n_layers, b, n), F32),
        compiler_params=_params("arbitrary", "arbitrary"),
        name="adaln_mod",
    )(c, w_ada, b_ada.reshape(n_layers, 1, n))


def _swiglu_residual(x, shift, scale, gate, gain, wg_ref, wu_ref, wo_ref, n_chunks):
    hn = _modulated_norm(x, gain, shift, scale).astype(BF16)
    d_ff = wg_ref.shape[1]
    n_tiles = -(-d_ff // MXU_WIDTH)
    edges = [min(d_ff, MXU_WIDTH * ((n_tiles * j + n_chunks - 1) // n_chunks)) for j in range(n_chunks + 1)]
    acc = None
    for j in range(n_chunks):
        sl = slice(edges[j], edges[j + 1])
        g = _dot(hn, wg_ref[:, sl])
        up = _dot(hn, wu_ref[:, sl])
        act = (g * jax.nn.sigmoid(g) * up).astype(BF16)
        part = _dot(act, wo_ref[sl, :])
        acc = part if acc is None else acc + part
    return x + (0.5 * (1.0 + gate)) * acc


def _ffn_mixin_kernel(mod_ref, g1_ref, gm_ref, x_ref, wg_ref, wu_ref, wo_ref,
                      watt_ref, wrwkv_ref, wconv_ref, cw_ref,
                      h_ref, patt_ref, prwkv_ref, yconv_ref, carry_ref, *, n_chunks):
    @pl.when(pl.program_id(1) == 0)
    def _():
        carry_ref[...] = jnp.zeros_like(carry_ref)

    h = _swiglu_residual(x_ref[...], mod_ref[0:1, :], mod_ref[1:2, :], mod_ref[2:3, :], g1_ref[...],
                         wg_ref, wu_ref, wo_ref, n_chunks)
    h_ref[...] = h
    hn = _modulated_norm(h, gm_ref[...], mod_ref[3:4, :], mod_ref[4:5, :]).astype(BF16)
    patt_ref[...] = _dot(hn, watt_ref[...]).astype(patt_ref.dtype)
    prwkv_ref[...] = _dot(hn, wrwkv_ref[...])
    pc = _dot(hn, wconv_ref[...])
    cwid = yconv_ref.shape[-1]
    b_gate = pc[:, 0:cwid]
    u = pc[:, cwid:2 * cwid] * pc[:, 2 * cwid:3 * cwid]
    tm = u.shape[0]
    row = lax.broadcasted_iota(jnp.int32, (tm, 1), 0)
    prev1 = carry_ref[1:2, :]
    prev2 = carry_ref[0:1, :]
    u1 = jnp.where(row == 0, prev1, pltpu.roll(u, 1, 0))
    u2 = jnp.where(row == 0, prev2, jnp.where(row == 1, prev1, pltpu.roll(u, 2, 0)))
    y = cw_ref[0:1, :] * u2 + cw_ref[1:2, :] * u1 + cw_ref[2:3, :] * u
    yconv_ref[...] = (b_gate * y).astype(yconv_ref.dtype)
    carry_ref[0:2, :] = u[tm - 2:tm, :]


def _ffn_chunks(d_ff):
    return 2 if d_ff % 256 == 0 else 1


def _ffn_weight_specs(d, d_ff, layer):
    gate = pl.BlockSpec((None, d, d_ff), lambda i, j: (layer, 0, 0), pipeline_mode=pl.Buffered(1))
    up = pl.BlockSpec((None, d, d_ff), lambda i, j: (layer, 0, 1), pipeline_mode=pl.Buffered(1))
    return [gate, up, _layer_spec((d_ff, d), layer)]


def _ffn_mixin(h, mod9, g_ffn, g_mix, w_in, w_out, w_att, w_rwkv, w_conv, conv_w, layer):
    b, t, d = h.shape
    d_ff = w_out.shape[1]
    tm = min(TOKEN_TILE, t)
    tok = lambda width: pl.BlockSpec((None, tm, width), lambda i, j: (i, j, 0))
    return pl.pallas_call(
        functools.partial(_ffn_mixin_kernel, n_chunks=_ffn_chunks(d_ff)),
        grid=(b, t // tm),
        in_specs=[
            pl.BlockSpec((None, N_MOD, d), lambda i, j: (i, 0, 0)),
            _layer_spec((1, d), layer),
            _layer_spec((1, d), layer),
            tok(d),
            *_ffn_weight_specs(d, d_ff, layer),
            _layer_spec((d, ATT_PROJ_WIDTH), layer),
            _layer_spec((d, RWKV_PROJ_WIDTH), layer),
            _layer_spec((d, 3 * CONV_WIDTH), layer),
            _layer_spec((CONV_K, CONV_WIDTH), layer),
        ],
        out_specs=[tok(d), tok(ATT_PROJ_WIDTH), tok(RWKV_PROJ_WIDTH), tok(CONV_WIDTH)],
        out_shape=[
            jax.ShapeDtypeStruct((b, t, d), F32),
            jax.ShapeDtypeStruct((b, t, ATT_PROJ_WIDTH), BF16),
            jax.ShapeDtypeStruct((b, t, RWKV_PROJ_WIDTH), F32),
            jax.ShapeDtypeStruct((b, t, CONV_WIDTH), BF16),
        ],
        scratch_shapes=[pltpu.VMEM((8, CONV_WIDTH), F32)],
        compiler_params=_params("arbitrary", "arbitrary"),
        name="ffn1_mix_in",
    )(mod9, g_ffn, g_mix, h, w_in, w_in, w_out, w_att, w_rwkv, w_conv, conv_w)


def _mixout_ffn_kernel(mod_ref, g2_ref, h_ref, ya_ref, yr_ref, yc_ref, wmix_ref, wg_ref, wu_ref, wo_ref,
                       o_ref, *, n_chunks):
    wa = ATT_WIDTH
    wr = wa + RWKV_WIDTH
    mixed = (_dot(ya_ref[...], wmix_ref[0:wa, :]) + _dot(yr_ref[...], wmix_ref[wa:wr, :])
             + _dot(yc_ref[...], wmix_ref[wr:, :]))
    h = h_ref[...] + (1.0 + mod_ref[5:6, :]) * mixed
    o_ref[...] = _swiglu_residual(h, mod_ref[6:7, :], mod_ref[7:8, :], mod_ref[8:9, :], g2_ref[...],
                                  wg_ref, wu_ref, wo_ref, n_chunks)


def _mixout_ffn(h, mod9, g_ffn, y_att, y_rwkv, y_conv, w_mix_out, w_in, w_out, layer):
    b, t, d = h.shape
    d_ff = w_out.shape[1]
    tm = min(TOKEN_TILE, t)
    tok = lambda width: pl.BlockSpec((None, tm, width), lambda i, j: (i, j, 0))
    return pl.pallas_call(
        functools.partial(_mixout_ffn_kernel, n_chunks=_ffn_chunks(d_ff)),
        grid=(b, t // tm),
        in_specs=[
            pl.BlockSpec((None, N_MOD, d), lambda i, j: (i, 0, 0)),
            _layer_spec((1, d), layer),
            tok(d), tok(ATT_WIDTH), tok(RWKV_WIDTH), tok(CONV_WIDTH),
            _layer_spec((d, d), layer),
            *_ffn_weight_specs(d, d_ff, layer),
        ],
        out_specs=tok(d),
        out_shape=jax.ShapeDtypeStruct((b, t, d), F32),
        compiler_params=_params("arbitrary", "arbitrary"),
        name="mix_out_ffn2",
    )(mod9, g_ffn, h, y_att, y_rwkv, y_conv, w_mix_out, w_in, w_in, w_out)


def _head_rms(x, ones_blk, gain):
    ss = _dot((x * x).astype(BF16), ones_blk)
    return x * lax.rsqrt(ss * (1.0 / HEAD_DIM) + EPS) * gain


def _attn_kernel(sink_ref, qg_ref, kg_ref, bias_ref, onesq_ref, onesk_ref, densel_ref,
                 q_ref, kvc_ref, kvp_ref, o_ref):
    n = pl.program_id(1)
    blk = ATT_BLOCK
    n_sub = q_ref.shape[0] // blk
    rows = 2 * blk
    low = lax.broadcasted_iota(jnp.int32, (1, 2 * HEAD_DIM), 1) < HEAD_DIM
    top = lax.broadcasted_iota(jnp.int32, (rows, 1), 0) < blk

    q = q_ref[...].astype(F32)
    qn = _head_rms(q, onesq_ref[...], qg_ref[...]).astype(BF16)
    kv = jnp.concatenate([kvp_ref[...], kvc_ref[...]], axis=0).astype(F32)
    k = kv[:, 0:ATT_KV_WIDTH]
    v = kv[:, ATT_KV_WIDTH:2 * ATT_KV_WIDTH]
    kn = _head_rms(k, onesk_ref[...], kg_ref[...])
    kr = pltpu.roll(kn, HEAD_DIM, 1)
    vr = pltpu.roll(v, HEAD_DIM, 1)
    zero = jnp.zeros_like(kn)
    bf = lambda z: z.astype(BF16)
    k_low = [bf(jnp.where(low, kn, zero)), bf(jnp.where(low, kr, zero))]
    k_high = [bf(jnp.where(low, zero, kr)), bf(jnp.where(low, zero, kn))]
    v_low = [bf(jnp.where(low, v, zero)), bf(jnp.where(low, vr, zero))]
    v_high = [bf(jnp.where(low, zero, vr)), bf(jnp.where(low, zero, v))]
    den_sel = densel_ref[...]

    for j in range(n_sub):
        bias = bias_ref[jnp.minimum(n, 1)] if j == 0 else bias_ref[1]
        keys = slice(j * blk, (j + 2) * blk)
        for g in range(ATT_KV_HEADS):
            qj = qn[j * blk:(j + 1) * blk]
            qg = jnp.concatenate([qj[:, 256 * g:256 * g + 128], qj[:, 256 * g + 128:256 * g + 256]], axis=0)
            kcat = jnp.concatenate([k_low[g][keys], k_high[g][keys]], axis=0)
            vcat = jnp.concatenate([v_low[g][keys], v_high[g][keys]], axis=0)
            s = _dot_nt(qg, kcat) + bias
            h0 = 4 * g
            sink_e = jnp.where(top, sink_ref[h0], sink_ref[h0 + 2])
            sink_o = jnp.where(top, sink_ref[h0 + 1], sink_ref[h0 + 3])
            s_e = s[:, 0:2 * blk]
            s_o = s[:, 2 * blk:4 * blk]
            m_e = jnp.maximum(jnp.max(s_e, axis=-1, keepdims=True), sink_e)
            m_o = jnp.maximum(jnp.max(s_o, axis=-1, keepdims=True), sink_o)
            p = bf(jnp.concatenate([jnp.exp2(s_e - m_e), jnp.exp2(s_o - m_o)], axis=1))
            nd = _dot(p, jnp.concatenate([vcat, den_sel], axis=1))
            den = nd[:, 2 * HEAD_DIM:] + jnp.where(low, jnp.exp2(sink_e - m_e), jnp.exp2(sink_o - m_o))
            out = (nd[:, 0:2 * HEAD_DIM] / den).astype(o_ref.dtype)
            o_ref[j * blk:(j + 1) * blk, 256 * g:256 * g + 128] = out[0:blk]
            o_ref[j * blk:(j + 1) * blk, 256 * g + 128:256 * g + 256] = out[blk:rows]


def _attention_constants():
    blk = ATT_BLOCK
    ri = jnp.arange(2 * blk)[:, None] % blk
    cj = jnp.arange(4 * blk)[None, :] % (2 * blk)
    band = (cj > ri) & (cj <= ri + blk)
    bias = jnp.stack([jnp.where(band & (cj >= blk), 0.0, NEG_BIG), jnp.where(band, 0.0, NEG_BIG)]).astype(F32)
    ones = lambda n: (jnp.arange(n)[:, None] // HEAD_DIM == jnp.arange(n)[None, :] // HEAD_DIM).astype(BF16)
    den_sel = (jnp.arange(4 * blk)[:, None] // (2 * blk) == jnp.arange(2 * HEAD_DIM)[None, :] // HEAD_DIM)
    return bias, ones(ATT_WIDTH), ones(ATT_KV_WIDTH), den_sel.astype(BF16)


def _attention(p_att, q_gain, k_gain, sinks):
    b, t, _ = p_att.shape
    blk = ATT_BLOCK
    tq = min(ATT_TILE, t)
    n_sub = tq // blk
    tile_gain = lambda g, reps: jnp.tile(g.astype(F32), reps).reshape(1, reps * HEAD_DIM)
    log2e = math.log2(math.e)
    bias, ones_q, ones_k, den_sel = _attention_constants()
    kv_col = ATT_WIDTH // (2 * ATT_KV_WIDTH)
    return pl.pallas_call(
        _attn_kernel,
        grid=(b, t // tq),
        in_specs=[
            pl.BlockSpec(memory_space=pltpu.SMEM),
            _const_spec((1, ATT_WIDTH)),
            _const_spec((1, ATT_KV_WIDTH)),
            _const_spec(bias.shape),
            _const_spec(ones_q.shape),
            _const_spec(ones_k.shape),
            _const_spec(den_sel.shape),
            pl.BlockSpec((None, tq, ATT_WIDTH), lambda i, n: (i, n, 0)),
            pl.BlockSpec((None, tq, 2 * ATT_KV_WIDTH), lambda i, n: (i, n, kv_col)),
            pl.BlockSpec((None, blk, 2 * ATT_KV_WIDTH), lambda i, n: (i, jnp.maximum(n * n_sub - 1, 0), kv_col)),
        ],
        out_specs=pl.BlockSpec((None, tq, ATT_WIDTH), lambda i, n: (i, n, 0)),
        out_shape=jax.ShapeDtypeStruct((b, t, ATT_WIDTH), BF16),
        compiler_params=_params("arbitrary", "arbitrary"),
        name="swa_sink_attention",
    )(sinks.astype(F32) * log2e, tile_gain(q_gain, ATT_Q_HEADS) * (HEAD_DIM ** -0.5 * log2e),
      tile_gain(k_gain, ATT_KV_HEADS), bias, ones_q, ones_k, den_sel, p_att, p_att, p_att)


def _rwkv_kernel(p_ref, mu_ref, w0_ref, ww2_ref, a0_ref, wa2_ref, wg2_ref, kk_ref, ka_ref, rk_ref,
                 gnw_ref, gnb_ref, o_ref, prev_ref, state_ref, *, c_len):
    n_seq, tt, _ = p_ref.shape
    width = RWKV_WIDTH
    n_heads = RWKV_HEADS
    n_ch = tt // c_len
    bf = lambda z: z.astype(BF16)
    each = lambda f, *cols: [f(*args) for args in zip(*cols)]

    @pl.when(pl.program_id(1) == 0)
    def _():
        prev_ref[...] = jnp.zeros_like(prev_ref)
        state_ref[...] = jnp.zeros_like(state_ref)

    ones_head = _block_ones(width, HEAD_DIM)
    ones_head2 = jnp.concatenate([ones_head, ones_head], axis=0)

    def head_sum(z):
        hi = bf(z)
        lo = bf(z - hi.astype(F32))
        return _dot(jnp.concatenate([hi, lo], axis=1), ones_head2)

    tri_r = lax.broadcasted_iota(jnp.int32, (tt, tt), 0)
    tri_c = lax.broadcasted_iota(jnp.int32, (tt, tt), 1)
    tri = jnp.where((tri_c <= tri_r) & (tri_c // c_len == tri_r // c_len), 1.0, 0.0).astype(BF16)
    tri3 = jnp.concatenate([tri, tri, tri], axis=1)
    trow = lax.broadcasted_iota(jnp.int32, (tt, 1), 0)
    lane_head = lax.broadcasted_iota(jnp.int32, (1, width), 1) // HEAD_DIM

    def stack(z):
        zero = jnp.zeros_like(z)
        return jnp.concatenate([jnp.where(lane_head == h, z, zero) for h in range(n_heads)], axis=0)

    sl = [slice(j * c_len, (j + 1) * c_len) for j in range(n_ch)]
    cut = lambda z: [z[s] for s in sl]

    def token_features(s):
        p = p_ref[s]
        p_prev = jnp.where(trow == 0, prev_ref[s, 0:1, :], pltpu.roll(p, 1, 0))
        prev_ref[s, 0:1, :] = p[tt - 1:tt, :]
        xs = p + mu_ref[...] * (p_prev - p)
        r = xs[:, 0:width]
        k = xs[:, width:2 * width]
        v = xs[:, 2 * width:3 * width]
        lora = xs[:, 3 * width:3 * width + RWKV_LORA]
        gate_in = xs[:, 3 * width + RWKV_LORA:]
        dw = _dot(bf(jnp.tanh(lora)), ww2_ref[...])
        da = _dot(bf(lora), wa2_ref[...])
        g = _dot(bf(jax.nn.sigmoid(gate_in)), wg2_ref[...])
        lw = -EXP_M05 * jax.nn.sigmoid(w0_ref[...] + dw)
        a = jax.nn.sigmoid(a0_ref[...] + da)
        kk_raw = k * kk_ref[...]
        kk = kk_raw * lax.rsqrt(jnp.maximum(head_sum(kk_raw * kk_raw), 1e-24))
        kmod = k * (1.0 + (a - 1.0) * ka_ref[...])
        b = kk * a
        lw_hi, lw_mid, lw_lo = _split3(lw)
        lc = _dot(tri3, jnp.concatenate([lw_hi, lw_mid, lw_lo], axis=0))
        w_inv = jnp.exp(-lc)
        lc_c = cut(lc)
        ltot = [z[c_len - 1:c_len, :] for z in lc_c]
        w_end = each(lambda lt, lcj: jnp.exp(lt - lcj), ltot, lc_c)
        chunks = dict(
            ltot=ltot,
            at=cut(bf(kk * jnp.exp(lc - lw))),
            rt=cut(r * jnp.exp(lc)),
            bt=cut(bf(b * w_inv)),
            kt=cut(bf(kmod * w_inv)),
            v=cut(bf(v)),
            bh=each(lambda z, w: bf(z * w), cut(b), w_end),
            kh=each(lambda z, w: bf(z * w), cut(kmod), w_end),
        )
        return chunks, (r, kmod, v, g)

    feats = [token_features(s) for s in range(n_seq)]
    col = lambda name: [z for chunks, _ in feats for z in chunks[name]]
    ltot, at_b, rt_c = col("ltot"), col("at"), col("rt")
    rt_b = each(bf, rt_c)
    bt_bd, kt_bd, v_bd, at_bd = (each(stack, col(nm)) for nm in ("bt", "kt", "v", "at"))
    bh_bd, kh_bd = each(stack, col("bh")), each(stack, col("kh"))

    mi = lax.broadcasted_iota(jnp.int32, (c_len, width), 0)
    mj = lax.broadcasted_iota(jnp.int32, (c_len, width), 1) & (c_len - 1)
    strict = mj < mi
    incl = mj <= mi
    eye = mj == mi
    eye_b = jnp.where(eye, 1.0, 0.0).astype(BF16)

    rows2 = lambda x, y: jnp.concatenate([x, y], axis=0)
    top, bot = (lambda z: z[0:c_len]), (lambda z: z[c_len:2 * c_len])
    ar_b = each(rows2, at_b, rt_b)
    g_b = each(_dot_nt, ar_b, bt_bd)
    g_k = each(_dot_nt, ar_b, kt_bd)
    a_ab = each(lambda z: jnp.where(strict, top(z), 0.0), g_b)
    a_rb = each(lambda z: bf(jnp.where(incl, bot(z), 0.0)), g_b)
    a_ak = each(lambda z: bf(jnp.where(strict, top(z), 0.0)), g_k)
    a_rk = each(lambda z: bf(jnp.where(incl, bot(z), 0.0)), g_k)
    bh_t = each(lambda y: bf(_dot_nt(eye_b, y)), bh_bd)
    kh_t = each(lambda y: bf(_dot_nt(eye_b, y)), kh_bd)
    on_v = each(lambda x, y, z, w: _dot(jnp.concatenate([x, y, z], axis=0), w), a_ak, a_rk, kh_t, v_bd)
    y1_bd = each(lambda z: stack(bf(top(z))), on_v)

    t_inv = each(lambda z: jnp.where(eye, 1.0, 0.0) - z, a_ab)
    pw = each(bf, a_ab)
    pw = each(lambda x: bf(_dot(x, stack(x))), pw)
    n_sq = int(math.log2(c_len)) - 1
    for lvl in range(n_sq):
        pw_bd = each(stack, pw)
        if lvl + 1 < n_sq:
            res = each(lambda t, x, y: _dot(rows2(bf(t), x), y), t_inv, pw, pw_bd)
            t_inv = each(lambda t, z: t + top(z), t_inv, res)
            pw = each(lambda z: bf(bot(z)), res)
        else:
            t_inv = each(lambda t, y: t + _dot(bf(t), y), t_inv, pw_bd)
    t_b = each(bf, t_inv)

    at2_bd = each(lambda x, y: stack(bf(_dot(x, y))), t_b, at_bd)
    u2_bd = each(lambda x, y: stack(bf(_dot(x, y))), t_b, y1_bd)
    ab_t = each(rows2, a_rb, bh_t)
    on_at2 = each(_dot, ab_t, at2_bd)
    on_u2 = each(_dot, ab_t, u2_bd)
    r2 = each(lambda z, m: bf(z - top(m)), rt_c, on_at2)
    p_t = each(lambda lt, m: bf(jnp.where(eye, jnp.exp(lt), 0.0) - bot(m)), ltot, on_at2)
    o2 = each(lambda z, m: z[c_len:2 * c_len] - top(m), on_v, on_u2)
    q_t = each(lambda z, m: z[2 * c_len:3 * c_len] - bot(m), on_v, on_u2)
    rp = each(rows2, r2, p_t)

    states = [state_ref[s] for s in range(n_seq)]
    ys = [[] for _ in range(n_seq)]
    for j in range(n_ch):
        for s in range(n_seq):
            i = s * n_ch + j
            res = _dot(rp[i], stack(bf(states[s])))
            ys[s].append(top(res) + o2[i])
            states[s] = bot(res) + q_t[i]

    for s in range(n_seq):
        state_ref[s] = states[s]
        r, kmod, v, g = feats[s][1]
        y = jnp.concatenate(ys[s], axis=0) if n_ch > 1 else ys[s][0]
        mean = head_sum(y) * (1.0 / HEAD_DIM)
        dev = y - mean
        var = head_sum(dev * dev) * (1.0 / HEAD_DIM)
        yn = dev * lax.rsqrt(var + RWKV_GN_EPS) * gnw_ref[...] + gnb_ref[...]
        bonus = head_sum(r * kmod * rk_ref[...]) * v
        o_ref[s] = ((yn + bonus) * g).astype(o_ref.dtype)


def _rwkv(p_rwkv, mu, w0, ww2_pad, a0, wa2_pad, wg2, k_k, k_a, r_k, gn_w, gn_b, layer):
    b, t, pw = p_rwkv.shape
    tt = min(RWKV_TILE, t)
    c_len = min(RWKV_CHUNK, tt)
    n_seq = RWKV_SEQS if b % RWKV_SEQS == 0 else 1
    vec = lambda n: _layer_spec((1, n), layer)
    return pl.pallas_call(
        functools.partial(_rwkv_kernel, c_len=c_len),
        grid=(b // n_seq, t // tt),
        in_specs=[
            pl.BlockSpec((n_seq, tt, pw), lambda i, c: (i, c, 0)),
            vec(pw), vec(RWKV_WIDTH),
            _layer_spec((RWKV_LORA, RWKV_WIDTH), layer),
            vec(RWKV_WIDTH),
            _layer_spec((RWKV_LORA, RWKV_WIDTH), layer),
            _layer_spec((RWKV_LORA, RWKV_WIDTH), layer),
            vec(RWKV_WIDTH), vec(RWKV_WIDTH), vec(RWKV_WIDTH), vec(RWKV_WIDTH), vec(RWKV_WIDTH),
        ],
        out_specs=pl.BlockSpec((n_seq, tt, RWKV_WIDTH), lambda i, c: (i, c, 0)),
        out_shape=jax.ShapeDtypeStruct((b, t, RWKV_WIDTH), BF16),
        scratch_shapes=[pltpu.VMEM((n_seq, 8, pw), F32), pltpu.VMEM((n_seq, HEAD_DIM, RWKV_WIDTH), F32)],
        compiler_params=_params("arbitrary", "arbitrary"),
        name="rwkv7_chunked",
    )(p_rwkv, mu, w0, ww2_pad, a0, wa2_pad, wg2, k_k, k_a, r_k, gn_w, gn_b)


def kernel(x, c, w_ada, b_ada, g_ffn1, w_ffn1_in, w_ffn1_out, g_mix, w_mix_in, w_mix_out, att_q_gain, att_k_gain, att_sinks, rwkv_mu, rwkv_w0, rwkv_w_w2, rwkv_a0, rwkv_a_w2, rwkv_g_w2, rwkv_k_k, rwkv_k_a, rwkv_r_k, rwkv_gn_w, rwkv_gn_b, conv_w, g_ffn2, w_ffn2_in, w_ffn2_out):
    n_layers, d = g_ffn1.shape
    bsz = x.shape[0]
    row3 = lambda z: z.astype(F32).reshape(n_layers, 1, -1)
    bf = lambda z: z.astype(BF16)

    mod = _modulation(c, w_ada, b_ada).reshape(n_layers, bsz, N_MOD, d)

    a_end = ATT_PROJ_WIDTH
    r_end = a_end + RWKV_PROJ_WIDTH
    w_att, w_rwkv, w_conv = bf(w_mix_in[:, :, :a_end]), bf(w_mix_in[:, :, a_end:r_end]), bf(w_mix_in[:, :, r_end:])
    w_mix_out_b = bf(w_mix_out)
    w1_in, w1_out, w2_in, w2_out = bf(w_ffn1_in), bf(w_ffn1_out), bf(w_ffn2_in), bf(w_ffn2_out)
    half = RWKV_LORA // 2
    zeros = jnp.zeros((n_layers, half, RWKV_WIDTH), F32)
    ww2_pad = bf(jnp.concatenate([rwkv_w_w2, zeros], axis=1))
    wa2_pad = bf(jnp.concatenate([zeros, rwkv_a_w2], axis=1))
    wg2 = bf(rwkv_g_w2)
    g1, gm, g2 = row3(g_ffn1), row3(g_mix), row3(g_ffn2)
    mu, w0, a0 = row3(rwkv_mu), row3(rwkv_w0), row3(rwkv_a0)
    k_k, k_a, r_k = row3(rwkv_k_k), row3(rwkv_k_a), row3(rwkv_r_k)
    gn_w, gn_b = row3(rwkv_gn_w), row3(rwkv_gn_b)

    h = x
    for l in range(n_layers):
        h, p_att, p_rwkv, y_conv = _ffn_mixin(h, mod[l], g1, gm, w1_in, w1_out, w_att, w_rwkv, w_conv, conv_w, l)
        y_att = _attention(p_att, att_q_gain[l], att_k_gain[l], att_sinks[l])
        y_rwkv = _rwkv(p_rwkv, mu, w0, ww2_pad, a0, wa2_pad, wg2, k_k, k_a, r_k, gn_w, gn_b, l)
        h = _mixout_ffn(h, mod[l], g2, y_att, y_rwkv, y_conv, w_mix_out_b, w2_in, w2_out, l)
    return h
```

```python
import functools
import math

import jax
import jax.numpy as jnp
from jax import lax
from jax.experimental import pallas as pl
from jax.experimental.pallas import tpu as pltpu

F32 = jnp.float32
BF16 = jnp.bfloat16

HEAD_DIM = 64
ATT_Q_HEADS = 8
ATT_KV_HEADS = 2
ATT_WIDTH = ATT_Q_HEADS * HEAD_DIM
ATT_KV_WIDTH = ATT_KV_HEADS * HEAD_DIM
ATT_PROJ_WIDTH = ATT_WIDTH + 2 * ATT_KV_WIDTH
ATT_BLOCK = 128
ATT_TILE = 512
RWKV_HEADS = 4
RWKV_WIDTH = RWKV_HEADS * HEAD_DIM
RWKV_LORA = 128
RWKV_PROJ_WIDTH = 3 * RWKV_WIDTH + 2 * RWKV_LORA
RWKV_GN_EPS = 64e-5
RWKV_CHUNK = 64
RWKV_TILE = 256
RWKV_SEQS = 4
CONV_WIDTH = 256
CONV_K = 3
N_MOD = 9
EPS = 1e-6
NEG_BIG = -1e30
EXP_M05 = math.exp(-0.5)

MXU_WIDTH = 256
BF16_SUBLANES = 16
TOKEN_TILE = 512
VMEM_LIMIT = 56 * 1024 * 1024


def _dot(a, b):
    return jnp.dot(a, b, preferred_element_type=F32)


def _dot_nt(a, b):
    return lax.dot_general(a, b, (((1,), (1,)), ((), ())), preferred_element_type=F32)


def _split3(x):
    hi = x.astype(BF16)
    r1 = x - hi.astype(F32)
    mid = r1.astype(BF16)
    lo = (r1 - mid.astype(F32)).astype(BF16)
    return hi, mid, lo


def _block_ones(n, blk):
    r = lax.broadcasted_iota(jnp.int32, (n, n), 0) // blk
    c = lax.broadcasted_iota(jnp.int32, (n, n), 1) // blk
    return jnp.where(r == c, 1.0, 0.0).astype(BF16)


def _const_spec(shape):
    nd = len(shape)
    return pl.BlockSpec(shape, lambda *_: (0,) * nd, pipeline_mode=pl.Buffered(1))


def _layer_spec(shape, layer):
    nd = len(shape)
    return pl.BlockSpec((None,) + tuple(shape), lambda *_: (layer,) + (0,) * nd,
                        pipeline_mode=pl.Buffered(1))


def _params(*sem):
    return pltpu.CompilerParams(dimension_semantics=sem, vmem_limit_bytes=VMEM_LIMIT)


def _modulated_norm(x, gain, shift, scale):
    ms = jnp.mean(x * x, axis=-1, keepdims=True)
    return (x * lax.rsqrt(ms + EPS) * gain) * (1.0 + scale) + shift


def _mod_kernel(c_ref, w_ref, b_ref, o_ref):
    c = c_ref[...]
    act = (c * jax.nn.sigmoid(c)).astype(BF16)
    o_ref[...] = _dot(act, w_ref[...].astype(BF16)) + b_ref[...]


def _modulation(c, w_ada, b_ada):
    n_layers, d, n = w_ada.shape
    b = c.shape[0]
    tn = d
    return pl.pallas_call(
        _mod_kernel,
        grid=(n_layers, n // tn),
        in_specs=[
            pl.BlockSpec((b, d), lambda l, j: (0, 0)),
            pl.BlockSpec((None, d, tn), lambda l, j: (l, 0, j)),
            pl.BlockSpec((None, 1, tn), lambda l, j: (l, 0, j)),
        ],
        out_specs=pl.BlockSpec((None, b, tn), lambda l, j: (l, 0, j)),
        out_shape=jax.ShapeDtypeStruct((n_layers, b, n), F32),
        compiler_params=_params("arbitrary", "arbitrary"),
        name="adaln_mod",
    )(c, w_ada, b_ada.reshape(n_layers, 1, n))


def _swiglu_residual(x, shift, scale, gate, gain, wg_ref, wu_ref, wo_ref, n_chunks):
    hn = _modulated_norm(x, gain, shift, scale).astype(BF16)
    d_ff = wg_ref.shape[1]
    n_tiles = -(-d_ff // MXU_WIDTH)
    edges = [min(d_ff, MXU_WIDTH * ((n_tiles * j + n_chunks - 1) // n_chunks)) for j in range(n_chunks + 1)]
    acc = None
    for j in range(n_chunks):
        sl = slice(edges[j], edges[j + 1])
        g = _dot(hn, wg_ref[:, sl])
        up = _dot(hn, wu_ref[:, sl])
        act = (g * jax.nn.sigmoid(g) * up).astype(BF16)
        part = _dot(act, wo_ref[sl, :])
        acc = part if acc is None else acc + part
    return x + (0.5 * (1.0 + gate)) * acc


def _cast_jobs(sources, n_steps, grid_cols):
    in_specs, out_specs, out_shapes = [], [], []
    for arr, layer in sources:
        _, rows, cols = arr.shape
        rb = min(r for r in range(BF16_SUBLANES, rows + 1, BF16_SUBLANES) if rows % r == 0 and rows // r <= n_steps)
        nblk = rows // rb
        blk = lambda i, j, nblk=nblk: jnp.minimum(i * grid_cols + j, nblk - 1)
        in_specs.append(pl.BlockSpec((None, rb, cols), lambda i, j, blk=blk, layer=layer: (layer, blk(i, j), 0)))
        out_specs.append(pl.BlockSpec((rb, cols), lambda i, j, blk=blk: (blk(i, j), 0)))
        out_shapes.append(jax.ShapeDtypeStruct((rows, cols), BF16))
    return in_specs, out_specs, out_shapes


def _run_casts(src_refs, dst_refs):
    for src, dst in zip(src_refs, dst_refs):
        dst[...] = src[...].astype(dst.dtype)


def _ffn_mixin_kernel(*refs, n_chunks, n_cast):
    (mod_ref, g1_ref, gm_ref, x_ref, wg_ref, wu_ref, wo_ref, wmix_ref, cw_ref) = refs[:9]
    cast_src = refs[9:9 + n_cast]
    h_ref, patt_ref, prwkv_ref, yconv_ref = refs[9 + n_cast:13 + n_cast]
    cast_dst = refs[13 + n_cast:13 + 2 * n_cast]
    carry_ref = refs[13 + 2 * n_cast]

    @pl.when(pl.program_id(1) == 0)
    def _():
        carry_ref[...] = jnp.zeros_like(carry_ref)

    h = _swiglu_residual(x_ref[...], mod_ref[0:1, :], mod_ref[1:2, :], mod_ref[2:3, :], g1_ref[...],
                         wg_ref, wu_ref, wo_ref, n_chunks)
    h_ref[...] = h
    hn = _modulated_norm(h, gm_ref[...], mod_ref[3:4, :], mod_ref[4:5, :]).astype(BF16)
    a_end = ATT_PROJ_WIDTH
    r_end = a_end + RWKV_PROJ_WIDTH
    patt_ref[...] = _dot(hn, wmix_ref[:, 0:a_end]).astype(patt_ref.dtype)
    prwkv_ref[...] = _dot(hn, wmix_ref[:, a_end:r_end])
    pc = _dot(hn, wmix_ref[:, r_end:])
    cwid = yconv_ref.shape[-1]
    b_gate = pc[:, 0:cwid]
    u = pc[:, cwid:2 * cwid] * pc[:, 2 * cwid:3 * cwid]
    tm = u.shape[0]
    row = lax.broadcasted_iota(jnp.int32, (tm, 1), 0)
    prev1 = carry_ref[1:2, :]
    prev2 = carry_ref[0:1, :]
    u1 = jnp.where(row == 0, prev1, pltpu.roll(u, 1, 0))
    u2 = jnp.where(row == 0, prev2, jnp.where(row == 1, prev1, pltpu.roll(u, 2, 0)))
    y = cw_ref[0:1, :] * u2 + cw_ref[1:2, :] * u1 + cw_ref[2:3, :] * u
    yconv_ref[...] = (b_gate * y).astype(yconv_ref.dtype)
    carry_ref[0:2, :] = u[tm - 2:tm, :]
    _run_casts(cast_src, cast_dst)


def _ffn_chunks(d_ff):
    return 2 if d_ff % 256 == 0 else 1


def _ffn_weight_specs(d, d_ff):
    gate = pl.BlockSpec((d, d_ff), lambda i, j: (0, 0), pipeline_mode=pl.Buffered(1))
    up = pl.BlockSpec((d, d_ff), lambda i, j: (0, 1), pipeline_mode=pl.Buffered(1))
    return [gate, up, _const_spec((d_ff, d))]


def _ffn_mixin(h, mod9, g_ffn, g_mix, w_in, w_out, w_mix_in, conv_w, layer, cast_sources):
    b, t, d = h.shape
    d_ff = w_out.shape[0]
    tm = min(TOKEN_TILE, t)
    nt = t // tm
    tok = lambda width: pl.BlockSpec((None, tm, width), lambda i, j: (i, j, 0))
    c_in, c_out, c_shapes = _cast_jobs(cast_sources, b * nt, nt)
    outs = pl.pallas_call(
        functools.partial(_ffn_mixin_kernel, n_chunks=_ffn_chunks(d_ff), n_cast=len(cast_sources)),
        grid=(b, nt),
        in_specs=[
            pl.BlockSpec((None, N_MOD, d), lambda i, j: (i, 0, 0)),
            _layer_spec((1, d), layer),
            _layer_spec((1, d), layer),
            tok(d),
            *_ffn_weight_specs(d, d_ff),
            _const_spec(w_mix_in.shape),
            _layer_spec((CONV_K, CONV_WIDTH), layer),
            *c_in,
        ],
        out_specs=[tok(d), tok(ATT_PROJ_WIDTH), tok(RWKV_PROJ_WIDTH), tok(CONV_WIDTH), *c_out],
        out_shape=[
            jax.ShapeDtypeStruct((b, t, d), F32),
            jax.ShapeDtypeStruct((b, t, ATT_PROJ_WIDTH), BF16),
            jax.ShapeDtypeStruct((b, t, RWKV_PROJ_WIDTH), F32),
            jax.ShapeDtypeStruct((b, t, CONV_WIDTH), BF16),
            *c_shapes,
        ],
        scratch_shapes=[pltpu.VMEM((8, CONV_WIDTH), F32)],
        compiler_params=_params("arbitrary", "arbitrary"),
        name="ffn1_mix_in",
    )(mod9, g_ffn, g_mix, h, w_in, w_in, w_out, w_mix_in, conv_w, *[a for a, _ in cast_sources])
    return outs[:4], outs[4:]


def _mixout_ffn_kernel(*refs, n_chunks, n_cast):
    (mod_ref, g2_ref, h_ref, ya_ref, yr_ref, yc_ref, wmix_ref, wg_ref, wu_ref, wo_ref) = refs[:10]
    cast_src = refs[10:10 + n_cast]
    o_ref = refs[10 + n_cast]
    cast_dst = refs[11 + n_cast:11 + 2 * n_cast]
    wa = ATT_WIDTH
    wr = wa + RWKV_WIDTH
    mixed = (_dot(ya_ref[...], wmix_ref[0:wa, :]) + _dot(yr_ref[...], wmix_ref[wa:wr, :])
             + _dot(yc_ref[...], wmix_ref[wr:, :]))
    h = h_ref[...] + (1.0 + mod_ref[5:6, :]) * mixed
    o_ref[...] = _swiglu_residual(h, mod_ref[6:7, :], mod_ref[7:8, :], mod_ref[8:9, :], g2_ref[...],
                                  wg_ref, wu_ref, wo_ref, n_chunks)
    _run_casts(cast_src, cast_dst)


def _mixout_ffn(h, mod9, g_ffn, y_att, y_rwkv, y_conv, w_mix_out, w_in, w_out, layer, cast_sources):
    b, t, d = h.shape
    d_ff = w_out.shape[0]
    tm = min(TOKEN_TILE, t)
    nt = t // tm
    tok = lambda width: pl.BlockSpec((None, tm, width), lambda i, j: (i, j, 0))
    c_in, c_out, c_shapes = _cast_jobs(cast_sources, b * nt, nt)
    outs = pl.pallas_call(
        functools.partial(_mixout_ffn_kernel, n_chunks=_ffn_chunks(d_ff), n_cast=len(cast_sources)),
        grid=(b, nt),
        in_specs=[
            pl.BlockSpec((None, N_MOD, d), lambda i, j: (i, 0, 0)),
            _layer_spec((1, d), layer),
            tok(d), tok(ATT_WIDTH), tok(RWKV_WIDTH), tok(CONV_WIDTH),
            _const_spec(w_mix_out.shape),
            *_ffn_weight_specs(d, d_ff),
            *c_in,
        ],
        out_specs=[tok(d), *c_out],
        out_shape=[jax.ShapeDtypeStruct((b, t, d), F32), *c_shapes],
        compiler_params=_params("arbitrary", "arbitrary"),
        name="mix_out_ffn2",
    )(mod9, g_ffn, h, y_att, y_rwkv, y_conv, w_mix_out, w_in, w_in, w_out, *[a for a, _ in cast_sources])
    return outs[0], outs[1:]


def _head_rms(x, ones_blk, gain):
    ss = _dot((x * x).astype(BF16), ones_blk)
    return x * lax.rsqrt(ss * (1.0 / HEAD_DIM) + EPS) * gain


def _attn_kernel(sink_ref, qg_ref, kg_ref, bias_ref, onesq_ref, onesk_ref, densel_ref,
                 q_ref, kvc_ref, kvp_ref, o_ref):
    n = pl.program_id(1)
    blk = ATT_BLOCK
    n_sub = q_ref.shape[0] // blk
    rows = 2 * blk
    low = lax.broadcasted_iota(jnp.int32, (1, 2 * HEAD_DIM), 1) < HEAD_DIM
    top = lax.broadcasted_iota(jnp.int32, (rows, 1), 0) < blk

    q = q_ref[...].astype(F32)
    qn = _head_rms(q, onesq_ref[...], qg_ref[...]).astype(BF16)
    kv = jnp.concatenate([kvp_ref[...], kvc_ref[...]], axis=0).astype(F32)
    k = kv[:, 0:ATT_KV_WIDTH]
    v = kv[:, ATT_KV_WIDTH:2 * ATT_KV_WIDTH]
    kn = _head_rms(k, onesk_ref[...], kg_ref[...])
    kr = pltpu.roll(kn, HEAD_DIM, 1)
    vr = pltpu.roll(v, HEAD_DIM, 1)
    zero = jnp.zeros_like(kn)
    bf = lambda z: z.astype(BF16)
    k_low = [bf(jnp.where(low, kn, zero)), bf(jnp.where(low, kr, zero))]
    k_high = [bf(jnp.where(low, zero, kr)), bf(jnp.where(low, zero, kn))]
    v_low = [bf(jnp.where(low, v, zero)), bf(jnp.where(low, vr, zero))]
    v_high = [bf(jnp.where(low, zero, vr)), bf(jnp.where(low, zero, v))]
    den_sel = densel_ref[...]

    for j in range(n_sub):
        bias = bias_ref[jnp.minimum(n, 1)] if j == 0 else bias_ref[1]
        keys = slice(j * blk, (j + 2) * blk)
        for g in range(ATT_KV_HEADS):
            qj = qn[j * blk:(j + 1) * blk]
            qg = jnp.concatenate([qj[:, 256 * g:256 * g + 128], qj[:, 256 * g + 128:256 * g + 256]], axis=0)
            kcat = jnp.concatenate([k_low[g][keys], k_high[g][keys]], axis=0)
            vcat = jnp.concatenate([v_low[g][keys], v_high[g][keys]], axis=0)
            s = _dot_nt(qg, kcat) + bias
            h0 = 4 * g
            sink_e = jnp.where(top, sink_ref[h0], sink_ref[h0 + 2])
            sink_o = jnp.where(top, sink_ref[h0 + 1], sink_ref[h0 + 3])
            s_e = s[:, 0:2 * blk]
            s_o = s[:, 2 * blk:4 * blk]
            m_e = jnp.maximum(jnp.max(s_e, axis=-1, keepdims=True), sink_e)
            m_o = jnp.maximum(jnp.max(s_o, axis=-1, keepdims=True), sink_o)
            p = bf(jnp.concatenate([jnp.exp2(s_e - m_e), jnp.exp2(s_o - m_o)], axis=1))
            nd = _dot(p, jnp.concatenate([vcat, den_sel], axis=1))
            den = nd[:, 2 * HEAD_DIM:] + jnp.where(low, jnp.exp2(sink_e - m_e), jnp.exp2(sink_o - m_o))
            out = (nd[:, 0:2 * HEAD_DIM] / den).astype(o_ref.dtype)
            o_ref[j * blk:(j + 1) * blk, 256 * g:256 * g + 128] = out[0:blk]
            o_ref[j * blk:(j + 1) * blk, 256 * g + 128:256 * g + 256] = out[blk:rows]


def _attention_constants():
    blk = ATT_BLOCK
    ri = jnp.arange(2 * blk)[:, None] % blk
    cj = jnp.arange(4 * blk)[None, :] % (2 * blk)
    band = (cj > ri) & (cj <= ri + blk)
    bias = jnp.stack([jnp.where(band & (cj >= blk), 0.0, NEG_BIG), jnp.where(band, 0.0, NEG_BIG)]).astype(F32)
    ones = lambda n: (jnp.arange(n)[:, None] // HEAD_DIM == jnp.arange(n)[None, :] // HEAD_DIM).astype(BF16)
    den_sel = (jnp.arange(4 * blk)[:, None] // (2 * blk) == jnp.arange(2 * HEAD_DIM)[None, :] // HEAD_DIM)
    return bias, ones(ATT_WIDTH), ones(ATT_KV_WIDTH), den_sel.astype(BF16)


def _attention(p_att, q_gain, k_gain, sinks):
    b, t, _ = p_att.shape
    blk = ATT_BLOCK
    tq = min(ATT_TILE, t)
    n_sub = tq // blk
    tile_gain = lambda g, reps: jnp.tile(g.astype(F32), reps).reshape(1, reps * HEAD_DIM)
    log2e = math.log2(math.e)
    bias, ones_q, ones_k, den_sel = _attention_constants()
    kv_col = ATT_WIDTH // (2 * ATT_KV_WIDTH)
    return pl.pallas_call(
        _attn_kernel,
        grid=(b, t // tq),
        in_specs=[
            pl.BlockSpec(memory_space=pltpu.SMEM),
            _const_spec((1, ATT_WIDTH)),
            _const_spec((1, ATT_KV_WIDTH)),
            _const_spec(bias.shape),
            _const_spec(ones_q.shape),
            _const_spec(ones_k.shape),
            _const_spec(den_sel.shape),
            pl.BlockSpec((None, tq, ATT_WIDTH), lambda i, n: (i, n, 0)),
            pl.BlockSpec((None, tq, 2 * ATT_KV_WIDTH), lambda i, n: (i, n, kv_col)),
            pl.BlockSpec((None, blk, 2 * ATT_KV_WIDTH), lambda i, n: (i, jnp.maximum(n * n_sub - 1, 0), kv_col)),
        ],
        out_specs=pl.BlockSpec((None, tq, ATT_WIDTH), lambda i, n: (i, n, 0)),
        out_shape=jax.ShapeDtypeStruct((b, t, ATT_WIDTH), BF16),
        compiler_params=_params("arbitrary", "arbitrary"),
        name="swa_sink_attention",
    )(sinks.astype(F32) * log2e, tile_gain(q_gain, ATT_Q_HEADS) * (HEAD_DIM ** -0.5 * log2e),
      tile_gain(k_gain, ATT_KV_HEADS), bias, ones_q, ones_k, den_sel, p_att, p_att, p_att)


def _rwkv_kernel(p_ref, mu_ref, w0_ref, ww2_ref, a0_ref, wa2_ref, wg2_ref, kk_ref, ka_ref, rk_ref,
                 gnw_ref, gnb_ref, o_ref, prev_ref, state_ref, *, c_len):
    n_seq, tt, _ = p_ref.shape
    width = RWKV_WIDTH
    n_heads = RWKV_HEADS
    n_ch = tt // c_len
    bf = lambda z: z.astype(BF16)
    each = lambda f, *cols: [f(*args) for args in zip(*cols)]

    @pl.when(pl.program_id(1) == 0)
    def _():
        prev_ref[...] = jnp.zeros_like(prev_ref)
        state_ref[...] = jnp.zeros_like(state_ref)

    ones_head = _block_ones(width, HEAD_DIM)
    ones_head2 = jnp.concatenate([ones_head, ones_head], axis=0)

    def head_sum(z):
        hi = bf(z)
        lo = bf(z - hi.astype(F32))
        return _dot(jnp.concatenate([hi, lo], axis=1), ones_head2)

    tri_r = lax.broadcasted_iota(jnp.int32, (tt, tt), 0)
    tri_c = lax.broadcasted_iota(jnp.int32, (tt, tt), 1)
    tri = jnp.where((tri_c <= tri_r) & (tri_c // c_len == tri_r // c_len), 1.0, 0.0).astype(BF16)
    tri3 = jnp.concatenate([tri, tri, tri], axis=1)
    trow = lax.broadcasted_iota(jnp.int32, (tt, 1), 0)
    lane_head = lax.broadcasted_iota(jnp.int32, (1, width), 1) // HEAD_DIM

    def stack(z):
        zero = jnp.zeros_like(z)
        return jnp.concatenate([jnp.where(lane_head == h, z, zero) for h in range(n_heads)], axis=0)

    sl = [slice(j * c_len, (j + 1) * c_len) for j in range(n_ch)]
    cut = lambda z: [z[s] for s in sl]

    def token_features(s):
        p = p_ref[s]
        p_prev = jnp.where(trow == 0, prev_ref[s, 0:1, :], pltpu.roll(p, 1, 0))
        prev_ref[s, 0:1, :] = p[tt - 1:tt, :]
        xs = p + mu_ref[...] * (p_prev - p)
        r = xs[:, 0:width]
        k = xs[:, width:2 * width]
        v = xs[:, 2 * width:3 * width]
        lora = xs[:, 3 * width:3 * width + RWKV_LORA]
        gate_in = xs[:, 3 * width + RWKV_LORA:]
        dw = _dot(bf(jnp.tanh(lora)), ww2_ref[...])
        da = _dot(bf(lora), wa2_ref[...])
        g = _dot(bf(jax.nn.sigmoid(gate_in)), wg2_ref[...])
        lw = -EXP_M05 * jax.nn.sigmoid(w0_ref[...] + dw)
        a = jax.nn.sigmoid(a0_ref[...] + da)
        kk_raw = k * kk_ref[...]
        kk = kk_raw * lax.rsqrt(jnp.maximum(head_sum(kk_raw * kk_raw), 1e-24))
        kmod = k * (1.0 + (a - 1.0) * ka_ref[...])
        b = kk * a
        lw_hi, lw_mid, lw_lo = _split3(lw)
        lc = _dot(tri3, jnp.concatenate([lw_hi, lw_mid, lw_lo], axis=0))
        w_inv = jnp.exp(-lc)
        lc_c = cut(lc)
        ltot = [z[c_len - 1:c_len, :] for z in lc_c]
        w_end = each(lambda lt, lcj: jnp.exp(lt - lcj), ltot, lc_c)
        chunks = dict(
            ltot=ltot,
            at=cut(bf(kk * jnp.exp(lc - lw))),
            rt=cut(r * jnp.exp(lc)),
            bt=cut(bf(b * w_inv)),
            kt=cut(bf(kmod * w_inv)),
            v=cut(bf(v)),
            bh=each(lambda z, w: bf(z * w), cut(b), w_end),
            kh=each(lambda z, w: bf(z * w), cut(kmod), w_end),
        )
        return chunks, (r, kmod, v, g)

    feats = [token_features(s) for s in range(n_seq)]
    col = lambda name: [z for chunks, _ in feats for z in chunks[name]]
    ltot, at_b, rt_c = col("ltot"), col("at"), col("rt")
    rt_b = each(bf, rt_c)
    bt_bd, kt_bd, v_bd, at_bd = (each(stack, col(nm)) for nm in ("bt", "kt", "v", "at"))
    bh_bd, kh_bd = each(stack, col("bh")), each(stack, col("kh"))

    mi = lax.broadcasted_iota(jnp.int32, (c_len, width), 0)
    mj = lax.broadcasted_iota(jnp.int32, (c_len, width), 1) & (c_len - 1)
    strict = mj < mi
    incl = mj <= mi
    eye = mj == mi
    eye_b = jnp.where(eye, 1.0, 0.0).astype(BF16)

    rows2 = lambda x, y: jnp.concatenate([x, y], axis=0)
    top, bot = (lambda z: z[0:c_len]), (lambda z: z[c_len:2 * c_len])
    ar_b = each(rows2, at_b, rt_b)
    g_b = each(_dot_nt, ar_b, bt_bd)
    g_k = each(_dot_nt, ar_b, kt_bd)
    a_ab = each(lambda z: jnp.where(strict, top(z), 0.0), g_b)
    a_rb = each(lambda z: bf(jnp.where(incl, bot(z), 0.0)), g_b)
    a_ak = each(lambda z: bf(jnp.where(strict, top(z), 0.0)), g_k)
    a_rk = each(lambda z: bf(jnp.where(incl, bot(z), 0.0)), g_k)
    bh_t = each(lambda y: bf(_dot_nt(eye_b, y)), bh_bd)
    kh_t = each(lambda y: bf(_dot_nt(eye_b, y)), kh_bd)
    on_v = each(lambda x, y, z, w: _dot(jnp.concatenate([x, y, z], axis=0), w), a_ak, a_rk, kh_t, v_bd)
    y1_bd = each(lambda z: stack(bf(top(z))), on_v)

    t_inv = each(lambda z: jnp.where(eye, 1.0, 0.0) - z, a_ab)
    pw = each(bf, a_ab)
    pw = each(lambda x: bf(_dot(x, stack(x))), pw)
    n_sq = int(math.log2(c_len)) - 1
    for lvl in range(n_sq):
        pw_bd = each(stack, pw)
        if lvl + 1 < n_sq:
            res = each(lambda t, x, y: _dot(rows2(bf(t), x), y), t_inv, pw, pw_bd)
            t_inv = each(lambda t, z: t + top(z), t_inv, res)
            pw = each(lambda z: bf(bot(z)), res)
        else:
            t_inv = each(lambda t, y: t + _dot(bf(t), y), t_inv, pw_bd)
    t_b = each(bf, t_inv)

    at2_bd = each(lambda x, y: stack(bf(_dot(x, y))), t_b, at_bd)
    u2_bd = each(lambda x, y: stack(bf(_dot(x, y))), t_b, y1_bd)
    ab_t = each(rows2, a_rb, bh_t)
    on_at2 = each(_dot, ab_t, at2_bd)
    on_u2 = each(_dot, ab_t, u2_bd)
    r2 = each(lambda z, m: bf(z - top(m)), rt_c, on_at2)
    p_t = each(lambda lt, m: bf(jnp.where(eye, jnp.exp(lt), 0.0) - bot(m)), ltot, on_at2)
    o2 = each(lambda z, m: z[c_len:2 * c_len] - top(m), on_v, on_u2)
    q_t = each(lambda z, m: z[2 * c_len:3 * c_len] - bot(m), on_v, on_u2)
    rp = each(rows2, r2, p_t)

    states = [state_ref[s] for s in range(n_seq)]
    ys = [[] for _ in range(n_seq)]
    for j in range(n_ch):
        for s in range(n_seq):
            i = s * n_ch + j
            res = _dot(rp[i], stack(bf(states[s])))
            ys[s].append(top(res) + o2[i])
            states[s] = bot(res) + q_t[i]

    for s in range(n_seq):
        state_ref[s] = states[s]
        r, kmod, v, g = feats[s][1]
        y = jnp.concatenate(ys[s], axis=0) if n_ch > 1 else ys[s][0]
        mean = head_sum(y) * (1.0 / HEAD_DIM)
        dev = y - mean
        var = head_sum(dev * dev) * (1.0 / HEAD_DIM)
        yn = dev * lax.rsqrt(var + RWKV_GN_EPS) * gnw_ref[...] + gnb_ref[...]
        bonus = head_sum(r * kmod * rk_ref[...]) * v
        o_ref[s] = ((yn + bonus) * g).astype(o_ref.dtype)


def _rwkv(p_rwkv, mu, w0, ww2_pad, a0, wa2_pad, wg2, k_k, k_a, r_k, gn_w, gn_b, layer):
    b, t, pw = p_rwkv.shape
    tt = min(RWKV_TILE, t)
    c_len = min(RWKV_CHUNK, tt)
    n_seq = RWKV_SEQS if b % RWKV_SEQS == 0 else 1
    vec = lambda n: _layer_spec((1, n), layer)
    return pl.pallas_call(
        functools.partial(_rwkv_kernel, c_len=c_len),
        grid=(b // n_seq, t // tt),
        in_specs=[
            pl.BlockSpec((n_seq, tt, pw), lambda i, c: (i, c, 0)),
            vec(pw), vec(RWKV_WIDTH),
            _layer_spec((RWKV_LORA, RWKV_WIDTH), layer),
            vec(RWKV_WIDTH),
            _layer_spec((RWKV_LORA, RWKV_WIDTH), layer),
            _layer_spec((RWKV_LORA, RWKV_WIDTH), layer),
            vec(RWKV_WIDTH), vec(RWKV_WIDTH), vec(RWKV_WIDTH), vec(RWKV_WIDTH), vec(RWKV_WIDTH),
        ],
        out_specs=pl.BlockSpec((n_seq, tt, RWKV_WIDTH), lambda i, c: (i, c, 0)),
        out_shape=jax.ShapeDtypeStruct((b, t, RWKV_WIDTH), BF16),
        scratch_shapes=[pltpu.VMEM((n_seq, 8, pw), F32), pltpu.VMEM((n_seq, HEAD_DIM, RWKV_WIDTH), F32)],
        compiler_params=_params("arbitrary", "arbitrary"),
        name="rwkv7_chunked",
    )(p_rwkv, mu, w0, ww2_pad, a0, wa2_pad, wg2, k_k, k_a, r_k, gn_w, gn_b)


def kernel(x, c, w_ada, b_ada, g_ffn1, w_ffn1_in, w_ffn1_out, g_mix, w_mix_in, w_mix_out, att_q_gain, att_k_gain, att_sinks, rwkv_mu, rwkv_w0, rwkv_w_w2, rwkv_a0, rwkv_a_w2, rwkv_g_w2, rwkv_k_k, rwkv_k_a, rwkv_r_k, rwkv_gn_w, rwkv_gn_b, conv_w, g_ffn2, w_ffn2_in, w_ffn2_out):
    n_layers, d = g_ffn1.shape
    bsz = x.shape[0]
    row3 = lambda z: z.astype(F32).reshape(n_layers, 1, -1)
    bf = lambda z: z.astype(BF16)

    mod = _modulation(c, w_ada, b_ada).reshape(n_layers, bsz, N_MOD, d)

    half = RWKV_LORA // 2
    zeros = jnp.zeros((n_layers, half, RWKV_WIDTH), F32)
    ww2_pad = bf(jnp.concatenate([rwkv_w_w2, zeros], axis=1))
    wa2_pad = bf(jnp.concatenate([zeros, rwkv_a_w2], axis=1))
    wg2 = bf(rwkv_g_w2)
    g1, gm, g2 = row3(g_ffn1), row3(g_mix), row3(g_ffn2)
    mu, w0, a0 = row3(rwkv_mu), row3(rwkv_w0), row3(rwkv_a0)
    k_k, k_a, r_k = row3(rwkv_k_k), row3(rwkv_k_a), row3(rwkv_r_k)
    gn_w, gn_b = row3(rwkv_gn_w), row3(rwkv_gn_b)

    first = lambda l: [(w_ffn1_in, l), (w_ffn1_out, l), (w_mix_in, l)]
    second = lambda l: [(w_ffn2_in, l), (w_ffn2_out, l), (w_mix_out, l)]
    w1_in, w1_out, w_mix_in_b = bf(w_ffn1_in[0]), bf(w_ffn1_out[0]), bf(w_mix_in[0])

    h = x
    for l in range(n_layers):
        (h, p_att, p_rwkv, y_conv), (w2_in, w2_out, w_mix_out_b) = _ffn_mixin(
            h, mod[l], g1, gm, w1_in, w1_out, w_mix_in_b, conv_w, l, second(l))
        y_att = _attention(p_att, att_q_gain[l], att_k_gain[l], att_sinks[l])
        y_rwkv = _rwkv(p_rwkv, mu, w0, ww2_pad, a0, wa2_pad, wg2, k_k, k_a, r_k, gn_w, gn_b, l)
        h, nxt = _mixout_ffn(h, mod[l], g2, y_att, y_rwkv, y_conv, w_mix_out_b, w2_in, w2_out, l,
                             first(l + 1) if l + 1 < n_layers else [])
        if nxt:
            w1_in, w1_out, w_mix_in_b = nxt
    return h
```

```python
import functools
import math

import jax
import jax.numpy as jnp
from jax import lax
from jax.experimental import pallas as pl
from jax.experimental.pallas import tpu as pltpu

F32 = jnp.float32
BF16 = jnp.bfloat16

HEAD_DIM = 64
ATT_Q_HEADS = 8
ATT_KV_HEADS = 2
ATT_WIDTH = ATT_Q_HEADS * HEAD_DIM
ATT_KV_WIDTH = ATT_KV_HEADS * HEAD_DIM
ATT_PROJ_WIDTH = ATT_WIDTH + 2 * ATT_KV_WIDTH
ATT_BLOCK = 128
ATT_TILE = 1024
RWKV_HEADS = 4
RWKV_WIDTH = RWKV_HEADS * HEAD_DIM
RWKV_LORA = 128
RWKV_PROJ_WIDTH = 3 * RWKV_WIDTH + 2 * RWKV_LORA
RWKV_GN_EPS = 64e-5
RWKV_CHUNK = 64
RWKV_TILE = 256
RWKV_SEQS = 4
CONV_WIDTH = 256
CONV_K = 3
N_MOD = 9
EPS = 1e-6
NEG_BIG = -1e30
EXP_M05 = math.exp(-0.5)

MXU_WIDTH = 256
BF16_SUBLANES = 16
TOKEN_TILE = 512
VMEM_LIMIT = 56 * 1024 * 1024


def _dot(a, b):
    return jnp.dot(a, b, preferred_element_type=F32)


def _dot_nt(a, b):
    return lax.dot_general(a, b, (((1,), (1,)), ((), ())), preferred_element_type=F32)


def _block_ones(n, blk):
    r = lax.broadcasted_iota(jnp.int32, (n, n), 0) // blk
    c = lax.broadcasted_iota(jnp.int32, (n, n), 1) // blk
    return jnp.where(r == c, 1.0, 0.0).astype(BF16)


def _const_spec(shape):
    nd = len(shape)
    return pl.BlockSpec(shape, lambda *_: (0,) * nd, pipeline_mode=pl.Buffered(1))


def _layer_spec(shape, layer):
    nd = len(shape)
    return pl.BlockSpec((None,) + tuple(shape), lambda *_: (layer,) + (0,) * nd,
                        pipeline_mode=pl.Buffered(1))


def _params(*sem):
    return pltpu.CompilerParams(dimension_semantics=sem, vmem_limit_bytes=VMEM_LIMIT)


def _modulated_norm(x, gain, shift, scale):
    ms = jnp.mean(x * x, axis=-1, keepdims=True)
    return (x * lax.rsqrt(ms + EPS) * gain) * (1.0 + scale) + shift


def _mod_kernel(c_ref, w_ref, b_ref, o_ref):
    c = c_ref[...]
    act = (c * jax.nn.sigmoid(c)).astype(BF16)
    o_ref[...] = _dot(act, w_ref[...].astype(BF16)) + b_ref[...]


def _modulation(c, w_ada, b_ada):
    n_layers, d, n = w_ada.shape
    b = c.shape[0]
    tn = d
    return pl.pallas_call(
        _mod_kernel,
        grid=(n_layers, n // tn),
        in_specs=[
            pl.BlockSpec((b, d), lambda l, j: (0, 0)),
            pl.BlockSpec((None, d, tn), lambda l, j: (l, 0, j)),
            pl.BlockSpec((None, 1, tn), lambda l, j: (l, 0, j)),
        ],
        out_specs=pl.BlockSpec((None, b, tn), lambda l, j: (l, 0, j)),
        out_shape=jax.ShapeDtypeStruct((n_layers, b, n), F32),
        compiler_params=_params("arbitrary", "arbitrary"),
        name="adaln_mod",
    )(c, w_ada, b_ada.reshape(n_layers, 1, n))


def _swiglu_residual(x, shift, scale, gate, gain, wg_ref, wu_ref, wo_ref, n_chunks):
    hn = _modulated_norm(x, gain, shift, scale).astype(BF16)
    d_ff = wg_ref.shape[1]
    n_tiles = -(-d_ff // MXU_WIDTH)
    edges = [min(d_ff, MXU_WIDTH * ((n_tiles * j + n_chunks - 1) // n_chunks)) for j in range(n_chunks + 1)]
    acc = None
    for j in range(n_chunks):
        sl = slice(edges[j], edges[j + 1])
        g = _dot(hn, wg_ref[:, sl])
        up = _dot(hn, wu_ref[:, sl])
        act = (g * jax.nn.sigmoid(g) * up).astype(BF16)
        part = _dot(act, wo_ref[sl, :])
        acc = part if acc is None else acc + part
    return x + (0.5 * (1.0 + gate)) * acc


def _cast_jobs(sources, n_steps, grid_cols):
    in_specs, out_specs, out_shapes = [], [], []
    for arr, layer in sources:
        _, rows, cols = arr.shape
        rb = min(r for r in range(BF16_SUBLANES, rows + 1, BF16_SUBLANES) if rows % r == 0 and rows // r <= n_steps)
        nblk = rows // rb
        blk = lambda i, j, nblk=nblk: jnp.minimum(i * grid_cols + j, nblk - 1)
        in_specs.append(pl.BlockSpec((None, rb, cols), lambda i, j, blk=blk, layer=layer: (layer, blk(i, j), 0)))
        out_specs.append(pl.BlockSpec((rb, cols), lambda i, j, blk=blk: (blk(i, j), 0)))
        out_shapes.append(jax.ShapeDtypeStruct((rows, cols), BF16))
    return in_specs, out_specs, out_shapes


def _run_casts(src_refs, dst_refs):
    for src, dst in zip(src_refs, dst_refs):
        dst[...] = src[...].astype(dst.dtype)


def _ffn_mixin_kernel(*refs, n_chunks, n_cast):
    (mod_ref, g1_ref, gm_ref, x_ref, wg_ref, wu_ref, wo_ref, wmix_ref, cw_ref) = refs[:9]
    cast_src = refs[9:9 + n_cast]
    h_ref, patt_ref, prwkv_ref, yconv_ref = refs[9 + n_cast:13 + n_cast]
    cast_dst = refs[13 + n_cast:13 + 2 * n_cast]
    carry_ref = refs[13 + 2 * n_cast]

    @pl.when(pl.program_id(1) == 0)
    def _():
        carry_ref[...] = jnp.zeros_like(carry_ref)

    h = _swiglu_residual(x_ref[...], mod_ref[0:1, :], mod_ref[1:2, :], mod_ref[2:3, :], g1_ref[...],
                         wg_ref, wu_ref, wo_ref, n_chunks)
    h_ref[...] = h
    hn = _modulated_norm(h, gm_ref[...], mod_ref[3:4, :], mod_ref[4:5, :]).astype(BF16)
    a_end = ATT_PROJ_WIDTH
    r_end = a_end + RWKV_PROJ_WIDTH
    patt_ref[...] = _dot(hn, wmix_ref[:, 0:a_end]).astype(patt_ref.dtype)
    prwkv_ref[...] = _dot(hn, wmix_ref[:, a_end:r_end])
    pc = _dot(hn, wmix_ref[:, r_end:])
    cwid = yconv_ref.shape[-1]
    b_gate = pc[:, 0:cwid]
    u = pc[:, cwid:2 * cwid] * pc[:, 2 * cwid:3 * cwid]
    tm = u.shape[0]
    row = lax.broadcasted_iota(jnp.int32, (tm, 1), 0)
    prev1 = carry_ref[1:2, :]
    prev2 = carry_ref[0:1, :]
    u1 = jnp.where(row == 0, prev1, pltpu.roll(u, 1, 0))
    u2 = jnp.where(row == 0, prev2, jnp.where(row == 1, prev1, pltpu.roll(u, 2, 0)))
    y = cw_ref[0:1, :] * u2 + cw_ref[1:2, :] * u1 + cw_ref[2:3, :] * u
    yconv_ref[...] = (b_gate * y).astype(yconv_ref.dtype)
    carry_ref[0:2, :] = u[tm - 2:tm, :]
    _run_casts(cast_src, cast_dst)


def _ffn_chunks(d_ff):
    return 2 if d_ff % 256 == 0 else 1


def _ffn_weight_specs(d, d_ff):
    gate = pl.BlockSpec((d, d_ff), lambda i, j: (0, 0), pipeline_mode=pl.Buffered(1))
    up = pl.BlockSpec((d, d_ff), lambda i, j: (0, 1), pipeline_mode=pl.Buffered(1))
    return [gate, up, _const_spec((d_ff, d))]


def _ffn_mixin(h, mod9, g_ffn, g_mix, w_in, w_out, w_mix_in, conv_w, layer, cast_sources):
    b, t, d = h.shape
    d_ff = w_out.shape[0]
    tm = min(TOKEN_TILE, t)
    nt = t // tm
    tok = lambda width: pl.BlockSpec((None, tm, width), lambda i, j: (i, j, 0))
    c_in, c_out, c_shapes = _cast_jobs(cast_sources, b * nt, nt)
    outs = pl.pallas_call(
        functools.partial(_ffn_mixin_kernel, n_chunks=_ffn_chunks(d_ff), n_cast=len(cast_sources)),
        grid=(b, nt),
        in_specs=[
            pl.BlockSpec((None, N_MOD, d), lambda i, j: (i, 0, 0)),
            _layer_spec((1, d), layer),
            _layer_spec((1, d), layer),
            tok(d),
            *_ffn_weight_specs(d, d_ff),
            _const_spec(w_mix_in.shape),
            _layer_spec((CONV_K, CONV_WIDTH), layer),
            *c_in,
        ],
        out_specs=[tok(d), tok(ATT_PROJ_WIDTH), tok(RWKV_PROJ_WIDTH), tok(CONV_WIDTH), *c_out],
        out_shape=[
            jax.ShapeDtypeStruct((b, t, d), F32),
            jax.ShapeDtypeStruct((b, t, ATT_PROJ_WIDTH), BF16),
            jax.ShapeDtypeStruct((b, t, RWKV_PROJ_WIDTH), F32),
            jax.ShapeDtypeStruct((b, t, CONV_WIDTH), BF16),
            *c_shapes,
        ],
        scratch_shapes=[pltpu.VMEM((8, CONV_WIDTH), F32)],
        compiler_params=_params("arbitrary", "arbitrary"),
        name="ffn1_mix_in",
    )(mod9, g_ffn, g_mix, h, w_in, w_in, w_out, w_mix_in, conv_w, *[a for a, _ in cast_sources])
    return outs[:4], outs[4:]


def _mixout_ffn_kernel(*refs, n_chunks, n_cast):
    (mod_ref, g2_ref, h_ref, ya_ref, yr_ref, yc_ref, wmix_ref, wg_ref, wu_ref, wo_ref) = refs[:10]
    cast_src = refs[10:10 + n_cast]
    o_ref = refs[10 + n_cast]
    cast_dst = refs[11 + n_cast:11 + 2 * n_cast]
    wa = ATT_WIDTH
    wr = wa + RWKV_WIDTH
    mixed = (_dot(ya_ref[...], wmix_ref[0:wa, :]) + _dot(yr_ref[...], wmix_ref[wa:wr, :])
             + _dot(yc_ref[...], wmix_ref[wr:, :]))
    h = h_ref[...] + (1.0 + mod_ref[5:6, :]) * mixed
    o_ref[...] = _swiglu_residual(h, mod_ref[6:7, :], mod_ref[7:8, :], mod_ref[8:9, :], g2_ref[...],
                                  wg_ref, wu_ref, wo_ref, n_chunks)
    _run_casts(cast_src, cast_dst)


def _mixout_ffn(h, mod9, g_ffn, y_att, y_rwkv, y_conv, w_mix_out, w_in, w_out, layer, cast_sources):
    b, t, d = h.shape
    d_ff = w_out.shape[0]
    tm = min(TOKEN_TILE, t)
    nt = t // tm
    tok = lambda width: pl.BlockSpec((None, tm, width), lambda i, j: (i, j, 0))
    c_in, c_out, c_shapes = _cast_jobs(cast_sources, b * nt, nt)
    outs = pl.pallas_call(
        functools.partial(_mixout_ffn_kernel, n_chunks=_ffn_chunks(d_ff), n_cast=len(cast_sources)),
        grid=(b, nt),
        in_specs=[
            pl.BlockSpec((None, N_MOD, d), lambda i, j: (i, 0, 0)),
            _layer_spec((1, d), layer),
            tok(d), tok(ATT_WIDTH), tok(RWKV_WIDTH), tok(CONV_WIDTH),
            _const_spec(w_mix_out.shape),
            *_ffn_weight_specs(d, d_ff),
            *c_in,
        ],
        out_specs=[tok(d), *c_out],
        out_shape=[jax.ShapeDtypeStruct((b, t, d), F32), *c_shapes],
        compiler_params=_params("arbitrary", "arbitrary"),
        name="mix_out_ffn2",
    )(mod9, g_ffn, h, y_att, y_rwkv, y_conv, w_mix_out, w_in, w_in, w_out, *[a for a, _ in cast_sources])
    return outs[0], outs[1:]


def _head_rms(x, ones_blk, gain):
    ss = _dot((x * x).astype(BF16), ones_blk)
    return x * lax.rsqrt(ss * (1.0 / HEAD_DIM) + EPS) * gain


def _attn_kernel(sink_ref, qg_ref, kg_ref, bias_ref, onesq_ref, onesk_ref, densel_ref,
                 q_ref, kvc_ref, kvp_ref, o_ref):
    n = pl.program_id(1)
    blk = ATT_BLOCK
    n_sub = q_ref.shape[0] // blk
    rows = 2 * blk
    low = lax.broadcasted_iota(jnp.int32, (1, 2 * HEAD_DIM), 1) < HEAD_DIM
    top = lax.broadcasted_iota(jnp.int32, (rows, 1), 0) < blk

    q = q_ref[...].astype(F32)
    qn = _head_rms(q, onesq_ref[...], qg_ref[...]).astype(BF16)
    kv = jnp.concatenate([kvp_ref[...], kvc_ref[...]], axis=0).astype(F32)
    k = kv[:, 0:ATT_KV_WIDTH]
    v = kv[:, ATT_KV_WIDTH:2 * ATT_KV_WIDTH]
    kn = _head_rms(k, onesk_ref[...], kg_ref[...])
    kr = pltpu.roll(kn, HEAD_DIM, 1)
    vr = pltpu.roll(v, HEAD_DIM, 1)
    zero = jnp.zeros_like(kn)
    bf = lambda z: z.astype(BF16)
    k_low = [bf(jnp.where(low, kn, zero)), bf(jnp.where(low, kr, zero))]
    k_high = [bf(jnp.where(low, zero, kr)), bf(jnp.where(low, zero, kn))]
    v_low = [bf(jnp.where(low, v, zero)), bf(jnp.where(low, vr, zero))]
    v_high = [bf(jnp.where(low, zero, vr)), bf(jnp.where(low, zero, v))]
    den_sel = densel_ref[...]

    for j in range(n_sub):
        bias = bias_ref[jnp.minimum(n, 1)] if j == 0 else bias_ref[1]
        keys = slice(j * blk, (j + 2) * blk)
        for g in range(ATT_KV_HEADS):
            qj = qn[j * blk:(j + 1) * blk]
            qg = jnp.concatenate([qj[:, 256 * g:256 * g + 128], qj[:, 256 * g + 128:256 * g + 256]], axis=0)
            kcat = jnp.concatenate([k_low[g][keys], k_high[g][keys]], axis=0)
            vcat = jnp.concatenate([v_low[g][keys], v_high[g][keys]], axis=0)
            s = _dot_nt(qg, kcat) + bias
            h0 = 4 * g
            sink_e = jnp.where(top, sink_ref[h0], sink_ref[h0 + 2])
            sink_o = jnp.where(top, sink_ref[h0 + 1], sink_ref[h0 + 3])
            s_e = s[:, 0:2 * blk]
            s_o = s[:, 2 * blk:4 * blk]
            m_e = jnp.maximum(jnp.max(s_e, axis=-1, keepdims=True), sink_e)
            m_o = jnp.maximum(jnp.max(s_o, axis=-1, keepdims=True), sink_o)
            p = bf(jnp.concatenate([jnp.exp2(s_e - m_e), jnp.exp2(s_o - m_o)], axis=1))
            nd = _dot(p, jnp.concatenate([vcat, den_sel], axis=1))
            den = nd[:, 2 * HEAD_DIM:] + jnp.where(low, jnp.exp2(sink_e - m_e), jnp.exp2(sink_o - m_o))
            out = (nd[:, 0:2 * HEAD_DIM] / den).astype(o_ref.dtype)
            o_ref[j * blk:(j + 1) * blk, 256 * g:256 * g + 128] = out[0:blk]
            o_ref[j * blk:(j + 1) * blk, 256 * g + 128:256 * g + 256] = out[blk:rows]


def _attention_constants():
    blk = ATT_BLOCK
    ri = jnp.arange(2 * blk)[:, None] % blk
    cj = jnp.arange(4 * blk)[None, :] % (2 * blk)
    band = (cj > ri) & (cj <= ri + blk)
    bias = jnp.stack([jnp.where(band & (cj >= blk), 0.0, NEG_BIG), jnp.where(band, 0.0, NEG_BIG)]).astype(F32)
    ones = lambda n: (jnp.arange(n)[:, None] // HEAD_DIM == jnp.arange(n)[None, :] // HEAD_DIM).astype(BF16)
    den_sel = (jnp.arange(4 * blk)[:, None] // (2 * blk) == jnp.arange(2 * HEAD_DIM)[None, :] // HEAD_DIM)
    return bias, ones(ATT_WIDTH), ones(ATT_KV_WIDTH), den_sel.astype(BF16)


def _attention(p_att, q_gain, k_gain, sinks):
    b, t, _ = p_att.shape
    blk = ATT_BLOCK
    tq = min(ATT_TILE, t)
    n_sub = tq // blk
    tile_gain = lambda g, reps: jnp.tile(g.astype(F32), reps).reshape(1, reps * HEAD_DIM)
    log2e = math.log2(math.e)
    bias, ones_q, ones_k, den_sel = _attention_constants()
    kv_col = ATT_WIDTH // (2 * ATT_KV_WIDTH)
    return pl.pallas_call(
        _attn_kernel,
        grid=(b, t // tq),
        in_specs=[
            pl.BlockSpec(memory_space=pltpu.SMEM),
            _const_spec((1, ATT_WIDTH)),
            _const_spec((1, ATT_KV_WIDTH)),
            _const_spec(bias.shape),
            _const_spec(ones_q.shape),
            _const_spec(ones_k.shape),
            _const_spec(den_sel.shape),
            pl.BlockSpec((None, tq, ATT_WIDTH), lambda i, n: (i, n, 0)),
            pl.BlockSpec((None, tq, 2 * ATT_KV_WIDTH), lambda i, n: (i, n, kv_col)),
            pl.BlockSpec((None, blk, 2 * ATT_KV_WIDTH), lambda i, n: (i, jnp.maximum(n * n_sub - 1, 0), kv_col)),
        ],
        out_specs=pl.BlockSpec((None, tq, ATT_WIDTH), lambda i, n: (i, n, 0)),
        out_shape=jax.ShapeDtypeStruct((b, t, ATT_WIDTH), BF16),
        compiler_params=_params("arbitrary", "arbitrary"),
        name="swa_sink_attention",
    )(sinks.astype(F32) * log2e, tile_gain(q_gain, ATT_Q_HEADS) * (HEAD_DIM ** -0.5 * log2e),
      tile_gain(k_gain, ATT_KV_HEADS), bias, ones_q, ones_k, den_sel, p_att, p_att, p_att)


def _rwkv_kernel(p_ref, mu_ref, w0_ref, ww2_ref, a0_ref, wa2_ref, wg2_ref, kk_ref, ka_ref, rk_ref,
                 gnw_ref, gnb_ref, o_ref, prev_ref, state_ref, *, c_len):
    n_seq, tt, _ = p_ref.shape
    width = RWKV_WIDTH
    n_heads = RWKV_HEADS
    n_ch = tt // c_len
    bf = lambda z: z.astype(BF16)
    each = lambda f, *cols: [f(*args) for args in zip(*cols)]

    @pl.when(pl.program_id(1) == 0)
    def _():
        prev_ref[...] = jnp.zeros_like(prev_ref)
        state_ref[...] = jnp.zeros_like(state_ref)

    ones_head = _block_ones(width, HEAD_DIM)

    def head_sum(z):
        return _dot(bf(z), ones_head)

    tri_r = lax.broadcasted_iota(jnp.int32, (tt, tt), 0)
    tri_c = lax.broadcasted_iota(jnp.int32, (tt, tt), 1)
    tri = jnp.where((tri_c <= tri_r) & (tri_c // c_len == tri_r // c_len), 1.0, 0.0).astype(BF16)
    tri2 = jnp.concatenate([tri, tri], axis=1)
    trow = lax.broadcasted_iota(jnp.int32, (tt, 1), 0)
    lane_head = lax.broadcasted_iota(jnp.int32, (1, width), 1) // HEAD_DIM

    def stack(z):
        zero = jnp.zeros_like(z)
        return jnp.concatenate([jnp.where(lane_head == h, z, zero) for h in range(n_heads)], axis=0)

    def head_transpose(z_bd):
        zt = jnp.transpose(z_bd.astype(F32))
        out = zt[0:HEAD_DIM]
        for h in range(1, n_heads):
            out = out + zt[h * HEAD_DIM:(h + 1) * HEAD_DIM]
        return bf(out)

    sl = [slice(j * c_len, (j + 1) * c_len) for j in range(n_ch)]
    cut = lambda z: [z[s] for s in sl]

    def token_features(s):
        p = p_ref[s]
        p_prev = jnp.where(trow == 0, prev_ref[s, 0:1, :], pltpu.roll(p, 1, 0))
        prev_ref[s, 0:1, :] = p[tt - 1:tt, :]
        xs = p + mu_ref[...] * (p_prev - p)
        r = xs[:, 0:width]
        k = xs[:, width:2 * width]
        v = xs[:, 2 * width:3 * width]
        lora = xs[:, 3 * width:3 * width + RWKV_LORA]
        gate_in = xs[:, 3 * width + RWKV_LORA:]
        dw = _dot(bf(jnp.tanh(lora)), ww2_ref[...])
        da = _dot(bf(lora), wa2_ref[...])
        g = _dot(bf(jax.nn.sigmoid(gate_in)), wg2_ref[...])
        lw = -EXP_M05 * jax.nn.sigmoid(w0_ref[...] + dw)
        a = jax.nn.sigmoid(a0_ref[...] + da)
        kk_raw = k * kk_ref[...]
        kk = kk_raw * lax.rsqrt(jnp.maximum(head_sum(kk_raw * kk_raw), 1e-24))
        kmod = k * (1.0 + (a - 1.0) * ka_ref[...])
        b = kk * a
        lw_hi = bf(lw)
        lw_lo = bf(lw - lw_hi.astype(F32))
        lc = _dot(tri2, jnp.concatenate([lw_hi, lw_lo], axis=0))
        w_inv = jnp.exp(-lc)
        lc_c = cut(lc)
        ltot = [z[c_len - 1:c_len, :] for z in lc_c]
        w_end = each(lambda lt, lcj: jnp.exp(lt - lcj), ltot, lc_c)
        chunks = dict(
            ltot=ltot,
            at=cut(bf(kk * jnp.exp(lc - lw))),
            rt=cut(r * jnp.exp(lc)),
            bt=cut(bf(b * w_inv)),
            kt=cut(bf(kmod * w_inv)),
            v=cut(bf(v)),
            bh=each(lambda z, w: bf(z * w), cut(b), w_end),
            kh=each(lambda z, w: bf(z * w), cut(kmod), w_end),
        )
        return chunks, (r, kmod, v, g)

    feats = [token_features(s) for s in range(n_seq)]
    col = lambda name: [z for chunks, _ in feats for z in chunks[name]]
    ltot, at_b, rt_c = col("ltot"), col("at"), col("rt")
    rt_b = each(bf, rt_c)
    bt_bd, kt_bd, v_bd, at_bd = (each(stack, col(nm)) for nm in ("bt", "kt", "v", "at"))
    bh_bd, kh_bd = each(stack, col("bh")), each(stack, col("kh"))

    mi = lax.broadcasted_iota(jnp.int32, (c_len, width), 0)
    mj = lax.broadcasted_iota(jnp.int32, (c_len, width), 1) & (c_len - 1)
    strict = mj < mi
    incl = mj <= mi
    eye = mj == mi

    rows2 = lambda x, y: jnp.concatenate([x, y], axis=0)
    top, bot = (lambda z: z[0:c_len]), (lambda z: z[c_len:2 * c_len])
    ar_b = each(rows2, at_b, rt_b)
    g_b = each(_dot_nt, ar_b, bt_bd)
    g_k = each(_dot_nt, ar_b, kt_bd)
    a_ab = each(lambda z: jnp.where(strict, top(z), 0.0), g_b)
    a_rb = each(lambda z: bf(jnp.where(incl, bot(z), 0.0)), g_b)
    a_ak = each(lambda z: bf(jnp.where(strict, top(z), 0.0)), g_k)
    a_rk = each(lambda z: bf(jnp.where(incl, bot(z), 0.0)), g_k)
    bh_t = each(head_transpose, bh_bd)
    kh_t = each(head_transpose, kh_bd)
    on_v = each(lambda x, y, z, w: _dot(jnp.concatenate([x, y, z], axis=0), w), a_ak, a_rk, kh_t, v_bd)
    y1_bd = each(lambda z: stack(bf(top(z))), on_v)

    t_inv = each(lambda z: jnp.where(eye, 1.0, 0.0) - z, a_ab)
    pw = each(bf, a_ab)
    pw = each(lambda x: bf(_dot(x, stack(x))), pw)
    n_sq = int(math.log2(c_len)) - 1
    for lvl in range(n_sq):
        pw_bd = each(stack, pw)
        if lvl + 1 < n_sq:
            res = each(lambda t, x, y: _dot(rows2(bf(t), x), y), t_inv, pw, pw_bd)
            t_inv = each(lambda t, z: t + top(z), t_inv, res)
            pw = each(lambda z: bf(bot(z)), res)
        else:
            t_inv = each(lambda t, y: t + _dot(bf(t), y), t_inv, pw_bd)
    t_b = each(bf, t_inv)

    at2_bd = each(lambda x, y: stack(bf(_dot(x, y))), t_b, at_bd)
    u2_bd = each(lambda x, y: stack(bf(_dot(x, y))), t_b, y1_bd)
    ab_t = each(rows2, a_rb, bh_t)
    on_at2 = each(_dot, ab_t, at2_bd)
    on_u2 = each(_dot, ab_t, u2_bd)
    r2 = each(lambda z, m: bf(z - top(m)), rt_c, on_at2)
    p_t = each(lambda lt, m: bf(jnp.where(eye, jnp.exp(lt), 0.0) - bot(m)), ltot, on_at2)
    o2 = each(lambda z, m: z[c_len:2 * c_len] - top(m), on_v, on_u2)
    q_t = each(lambda z, m: z[2 * c_len:3 * c_len] - bot(m), on_v, on_u2)
    rp = each(rows2, r2, p_t)

    states = [state_ref[s] for s in range(n_seq)]
    ys = [[] for _ in range(n_seq)]
    for j in range(n_ch):
        for s in range(n_seq):
            i = s * n_ch + j
            res = _dot(rp[i], stack(bf(states[s])))
            ys[s].append(top(res) + o2[i])
            states[s] = bot(res) + q_t[i]

    for s in range(n_seq):
        state_ref[s] = states[s]
        r, kmod, v, g = feats[s][1]
        y = jnp.concatenate(ys[s], axis=0) if n_ch > 1 else ys[s][0]
        mean = head_sum(y) * (1.0 / HEAD_DIM)
        dev = y - mean
        var = head_sum(dev * dev) * (1.0 / HEAD_DIM)
        yn = dev * lax.rsqrt(var + RWKV_GN_EPS) * gnw_ref[...] + gnb_ref[...]
        bonus = head_sum(r * kmod * rk_ref[...]) * v
        o_ref[s] = ((yn + bonus) * g).astype(o_ref.dtype)


def _rwkv(p_rwkv, mu, w0, ww2_pad, a0, wa2_pad, wg2, k_k, k_a, r_k, gn_w, gn_b, layer):
    b, t, pw = p_rwkv.shape
    tt = min(RWKV_TILE, t)
    c_len = min(RWKV_CHUNK, tt)
    n_seq = RWKV_SEQS if b % RWKV_SEQS == 0 else 1
    vec = lambda n: _layer_spec((1, n), layer)
    return pl.pallas_call(
        functools.partial(_rwkv_kernel, c_len=c_len),
        grid=(b // n_seq, t // tt),
        in_specs=[
            pl.BlockSpec((n_seq, tt, pw), lambda i, c: (i, c, 0)),
            vec(pw), vec(RWKV_WIDTH),
            _layer_spec((RWKV_LORA, RWKV_WIDTH), layer),
            vec(RWKV_WIDTH),
            _layer_spec((RWKV_LORA, RWKV_WIDTH), layer),
            _layer_spec((RWKV_LORA, RWKV_WIDTH), layer),
            vec(RWKV_WIDTH), vec(RWKV_WIDTH), vec(RWKV_WIDTH), vec(RWKV_WIDTH), vec(RWKV_WIDTH),
        ],
        out_specs=pl.BlockSpec((n_seq, tt, RWKV_WIDTH), lambda i, c: (i, c, 0)),
        out_shape=jax.ShapeDtypeStruct((b, t, RWKV_WIDTH), BF16),
        scratch_shapes=[pltpu.VMEM((n_seq, 8, pw), F32), pltpu.VMEM((n_seq, HEAD_DIM, RWKV_WIDTH), F32)],
        compiler_params=_params("arbitrary", "arbitrary"),
        name="rwkv7_chunked",
    )(p_rwkv, mu, w0, ww2_pad, a0, wa2_pad, wg2, k_k, k_a, r_k, gn_w, gn_b)


def kernel(x, c, w_ada, b_ada, g_ffn1, w_ffn1_in, w_ffn1_out, g_mix, w_mix_in, w_mix_out, att_q_gain, att_k_gain, att_sinks, rwkv_mu, rwkv_w0, rwkv_w_w2, rwkv_a0, rwkv_a_w2, rwkv_g_w2, rwkv_k_k, rwkv_k_a, rwkv_r_k, rwkv_gn_w, rwkv_gn_b, conv_w, g_ffn2, w_ffn2_in, w_ffn2_out):
    n_layers, d = g_ffn1.shape
    bsz = x.shape[0]
    row3 = lambda z: z.astype(F32).reshape(n_layers, 1, -1)
    bf = lambda z: z.astype(BF16)

    mod = _modulation(c, w_ada, b_ada).reshape(n_layers, bsz, N_MOD, d)

    half = RWKV_LORA // 2
    zeros = jnp.zeros((n_layers, half, RWKV_WIDTH), F32)
    ww2_pad = bf(jnp.concatenate([rwkv_w_w2, zeros], axis=1))
    wa2_pad = bf(jnp.concatenate([zeros, rwkv_a_w2], axis=1))
    wg2 = bf(rwkv_g_w2)
    g1, gm, g2 = row3(g_ffn1), row3(g_mix), row3(g_ffn2)
    mu, w0, a0 = row3(rwkv_mu), row3(rwkv_w0), row3(rwkv_a0)
    k_k, k_a, r_k = row3(rwkv_k_k), row3(rwkv_k_a), row3(rwkv_r_k)
    gn_w, gn_b = row3(rwkv_gn_w), row3(rwkv_gn_b)

    first = lambda l: [(w_ffn1_in, l), (w_ffn1_out, l), (w_mix_in, l)]
    second = lambda l: [(w_ffn2_in, l), (w_ffn2_out, l), (w_mix_out, l)]
    w1_in, w1_out, w_mix_in_b = bf(w_ffn1_in[0]), bf(w_ffn1_out[0]), bf(w_mix_in[0])

    h = x
    for l in range(n_layers):
        (h, p_att, p_rwkv, y_conv), (w2_in, w2_out, w_mix_out_b) = _ffn_mixin(
            h, mod[l], g1, gm, w1_in, w1_out, w_mix_in_b, conv_w, l, second(l))
        y_att = _attention(p_att, att_q_gain[l], att_k_gain[l], att_sinks[l])
        y_rwkv = _rwkv(p_rwkv, mu, w0, ww2_pad, a0, wa2_pad, wg2, k_k, k_a, r_k, gn_w, gn_b, l)
        h, nxt = _mixout_ffn(h, mod[l], g2, y_att, y_rwkv, y_conv, w_mix_out_b, w2_in, w2_out, l,
                             first(l + 1) if l + 1 < n_layers else [])
        if nxt:
            w1_in, w1_out, w_mix_in_b = nxt
    return h
```

```python
import functools
import math

import jax
import jax.numpy as jnp
from jax import lax
from jax.experimental import pallas as pl
from jax.experimental.pallas import tpu as pltpu

F32 = jnp.float32
BF16 = jnp.bfloat16

HEAD_DIM = 64
ATT_Q_HEADS = 8
ATT_KV_HEADS = 2
ATT_WIDTH = ATT_Q_HEADS * HEAD_DIM
ATT_KV_WIDTH = ATT_KV_HEADS * HEAD_DIM
ATT_PROJ_WIDTH = ATT_WIDTH + 2 * ATT_KV_WIDTH
ATT_BLOCK = 128
ATT_TILE = 2048
RWKV_HEADS = 4
RWKV_WIDTH = RWKV_HEADS * HEAD_DIM
RWKV_LORA = 128
RWKV_PROJ_WIDTH = 3 * RWKV_WIDTH + 2 * RWKV_LORA
RWKV_GN_EPS = 64e-5
RWKV_CHUNK = 64
RWKV_TILE = 512
RWKV_SEQS = 4
CONV_WIDTH = 256
CONV_K = 3
N_MOD = 9
EPS = 1e-6
NEG_BIG = -1e30
EXP_M05 = math.exp(-0.5)

MXU_WIDTH = 256
BF16_SUBLANES = 16
TOKEN_TILE = 512
VMEM_LIMIT = 56 * 1024 * 1024


def _dot(a, b):
    return jnp.dot(a, b, preferred_element_type=F32)


def _dot_nt(a, b):
    return lax.dot_general(a, b, (((1,), (1,)), ((), ())), preferred_element_type=F32)


def _block_ones(n, blk):
    r = lax.broadcasted_iota(jnp.int32, (n, n), 0) // blk
    c = lax.broadcasted_iota(jnp.int32, (n, n), 1) // blk
    return jnp.where(r == c, 1.0, 0.0).astype(BF16)


def _const_spec(shape):
    nd = len(shape)
    return pl.BlockSpec(shape, lambda *_: (0,) * nd, pipeline_mode=pl.Buffered(1))


def _layer_spec(shape, layer):
    nd = len(shape)
    return pl.BlockSpec((None,) + tuple(shape), lambda *_: (layer,) + (0,) * nd,
                        pipeline_mode=pl.Buffered(1))


def _params(*sem):
    return pltpu.CompilerParams(dimension_semantics=sem, vmem_limit_bytes=VMEM_LIMIT)


def _modulated_norm(x, gain, shift, scale):
    ms = jnp.mean(x * x, axis=-1, keepdims=True)
    return (x * lax.rsqrt(ms + EPS) * gain) * (1.0 + scale) + shift


def _mod_kernel(c_ref, w_ref, b_ref, o_ref):
    c = c_ref[...]
    act = (c * jax.nn.sigmoid(c)).astype(BF16)
    o_ref[...] = _dot(act, w_ref[...].astype(BF16)) + b_ref[...]


def _modulation(c, w_ada, b_ada):
    n_layers, d, n = w_ada.shape
    b = c.shape[0]
    tn = d
    return pl.pallas_call(
        _mod_kernel,
        grid=(n_layers, n // tn),
        in_specs=[
            pl.BlockSpec((b, d), lambda l, j: (0, 0)),
            pl.BlockSpec((None, d, tn), lambda l, j: (l, 0, j)),
            pl.BlockSpec((None, 1, tn), lambda l, j: (l, 0, j)),
        ],
        out_specs=pl.BlockSpec((None, b, tn), lambda l, j: (l, 0, j)),
        out_shape=jax.ShapeDtypeStruct((n_layers, b, n), F32),
        compiler_params=_params("arbitrary", "arbitrary"),
        name="adaln_mod",
    )(c, w_ada, b_ada.reshape(n_layers, 1, n))


def _swiglu_residual(x, shift, scale, gate, gain, wg_ref, wu_ref, wo_ref, n_chunks):
    hn = _modulated_norm(x, gain, shift, scale).astype(BF16)
    d_ff = wg_ref.shape[1]
    n_tiles = -(-d_ff // MXU_WIDTH)
    edges = [min(d_ff, MXU_WIDTH * ((n_tiles * j + n_chunks - 1) // n_chunks)) for j in range(n_chunks + 1)]
    acc = None
    for j in range(n_chunks):
        sl = slice(edges[j], edges[j + 1])
        g = _dot(hn, wg_ref[:, sl])
        up = _dot(hn, wu_ref[:, sl])
        act = (g * jax.nn.sigmoid(g) * up).astype(BF16)
        part = _dot(act, wo_ref[sl, :])
        acc = part if acc is None else acc + part
    return x + (0.5 * (1.0 + gate)) * acc


def _cast_jobs(sources, n_steps, grid_cols):
    in_specs, out_specs, out_shapes = [], [], []
    for arr, layer in sources:
        _, rows, cols = arr.shape
        rb = min(r for r in range(BF16_SUBLANES, rows + 1, BF16_SUBLANES) if rows % r == 0 and rows // r <= n_steps)
        nblk = rows // rb
        blk = lambda i, j, nblk=nblk: jnp.minimum(i * grid_cols + j, nblk - 1)
        in_specs.append(pl.BlockSpec((None, rb, cols), lambda i, j, blk=blk, layer=layer: (layer, blk(i, j), 0)))
        out_specs.append(pl.BlockSpec((rb, cols), lambda i, j, blk=blk: (blk(i, j), 0)))
        out_shapes.append(jax.ShapeDtypeStruct((rows, cols), BF16))
    return in_specs, out_specs, out_shapes


def _run_casts(src_refs, dst_refs):
    for src, dst in zip(src_refs, dst_refs):
        dst[...] = src[...].astype(dst.dtype)


def _ffn_mixin_kernel(*refs, n_chunks, n_cast):
    (mod_ref, g1_ref, gm_ref, x_ref, wg_ref, wu_ref, wo_ref, wmix_ref, cw_ref) = refs[:9]
    cast_src = refs[9:9 + n_cast]
    h_ref, patt_ref, prwkv_ref, yconv_ref = refs[9 + n_cast:13 + n_cast]
    cast_dst = refs[13 + n_cast:13 + 2 * n_cast]
    carry_ref = refs[13 + 2 * n_cast]

    @pl.when(pl.program_id(1) == 0)
    def _():
        carry_ref[...] = jnp.zeros_like(carry_ref)

    h = _swiglu_residual(x_ref[...], mod_ref[0:1, :], mod_ref[1:2, :], mod_ref[2:3, :], g1_ref[...],
                         wg_ref, wu_ref, wo_ref, n_chunks)
    h_ref[...] = h
    hn = _modulated_norm(h, gm_ref[...], mod_ref[3:4, :], mod_ref[4:5, :]).astype(BF16)
    a_end = ATT_PROJ_WIDTH
    r_end = a_end + RWKV_PROJ_WIDTH
    patt_ref[...] = _dot(hn, wmix_ref[:, 0:a_end]).astype(patt_ref.dtype)
    prwkv_ref[...] = _dot(hn, wmix_ref[:, a_end:r_end])
    pc = _dot(hn, wmix_ref[:, r_end:])
    cwid = yconv_ref.shape[-1]
    b_gate = pc[:, 0:cwid]
    u = pc[:, cwid:2 * cwid] * pc[:, 2 * cwid:3 * cwid]
    tm = u.shape[0]
    row = lax.broadcasted_iota(jnp.int32, (tm, 1), 0)
    prev1 = carry_ref[1:2, :]
    prev2 = carry_ref[0:1, :]
    u1 = jnp.where(row == 0, prev1, pltpu.roll(u, 1, 0))
    u2 = jnp.where(row == 0, prev2, jnp.where(row == 1, prev1, pltpu.roll(u, 2, 0)))
    y = cw_ref[0:1, :] * u2 + cw_ref[1:2, :] * u1 + cw_ref[2:3, :] * u
    yconv_ref[...] = (b_gate * y).astype(yconv_ref.dtype)
    carry_ref[0:2, :] = u[tm - 2:tm, :]
    _run_casts(cast_src, cast_dst)


def _ffn_chunks(d_ff):
    return 2 if d_ff % 256 == 0 else 1


def _ffn_weight_specs(d, d_ff):
    gate = pl.BlockSpec((d, d_ff), lambda i, j: (0, 0), pipeline_mode=pl.Buffered(1))
    up = pl.BlockSpec((d, d_ff), lambda i, j: (0, 1), pipeline_mode=pl.Buffered(1))
    return [gate, up, _const_spec((d_ff, d))]


def _ffn_mixin(h, mod9, g_ffn, g_mix, w_in, w_out, w_mix_in, conv_w, layer, cast_sources):
    b, t, d = h.shape
    d_ff = w_out.shape[0]
    tm = min(TOKEN_TILE, t)
    nt = t // tm
    tok = lambda width: pl.BlockSpec((None, tm, width), lambda i, j: (i, j, 0))
    c_in, c_out, c_shapes = _cast_jobs(cast_sources, b * nt, nt)
    outs = pl.pallas_call(
        functools.partial(_ffn_mixin_kernel, n_chunks=_ffn_chunks(d_ff), n_cast=len(cast_sources)),
        grid=(b, nt),
        in_specs=[
            pl.BlockSpec((None, N_MOD, d), lambda i, j: (i, 0, 0)),
            _layer_spec((1, d), layer),
            _layer_spec((1, d), layer),
            tok(d),
            *_ffn_weight_specs(d, d_ff),
            _const_spec(w_mix_in.shape),
            _layer_spec((CONV_K, CONV_WIDTH), layer),
            *c_in,
        ],
        out_specs=[tok(d), tok(ATT_PROJ_WIDTH), tok(RWKV_PROJ_WIDTH), tok(CONV_WIDTH), *c_out],
        out_shape=[
            jax.ShapeDtypeStruct((b, t, d), F32),
            jax.ShapeDtypeStruct((b, t, ATT_PROJ_WIDTH), BF16),
            jax.ShapeDtypeStruct((b, t, RWKV_PROJ_WIDTH), F32),
            jax.ShapeDtypeStruct((b, t, CONV_WIDTH), BF16),
            *c_shapes,
        ],
        scratch_shapes=[pltpu.VMEM((8, CONV_WIDTH), F32)],
        compiler_params=_params("arbitrary", "arbitrary"),
        name="ffn1_mix_in",
    )(mod9, g_ffn, g_mix, h, w_in, w_in, w_out, w_mix_in, conv_w, *[a for a, _ in cast_sources])
    return outs[:4], outs[4:]


def _mixout_ffn_kernel(*refs, n_chunks, n_cast):
    (mod_ref, g2_ref, h_ref, ya_ref, yr_ref, yc_ref, wmix_ref, wg_ref, wu_ref, wo_ref) = refs[:10]
    cast_src = refs[10:10 + n_cast]
    o_ref = refs[10 + n_cast]
    cast_dst = refs[11 + n_cast:11 + 2 * n_cast]
    wa = ATT_WIDTH
    wr = wa + RWKV_WIDTH
    mixed = (_dot(ya_ref[...], wmix_ref[0:wa, :]) + _dot(yr_ref[...], wmix_ref[wa:wr, :])
             + _dot(yc_ref[...], wmix_ref[wr:, :]))
    h = h_ref[...] + (1.0 + mod_ref[5:6, :]) * mixed
    o_ref[...] = _swiglu_residual(h, mod_ref[6:7, :], mod_ref[7:8, :], mod_ref[8:9, :], g2_ref[...],
                                  wg_ref, wu_ref, wo_ref, n_chunks)
    _run_casts(cast_src, cast_dst)


def _mixout_ffn(h, mod9, g_ffn, y_att, y_rwkv, y_conv, w_mix_out, w_in, w_out, layer, cast_sources):
    b, t, d = h.shape
    d_ff = w_out.shape[0]
    tm = min(TOKEN_TILE, t)
    nt = t // tm
    tok = lambda width: pl.BlockSpec((None, tm, width), lambda i, j: (i, j, 0))
    c_in, c_out, c_shapes = _cast_jobs(cast_sources, b * nt, nt)
    outs = pl.pallas_call(
        functools.partial(_mixout_ffn_kernel, n_chunks=_ffn_chunks(d_ff), n_cast=len(cast_sources)),
        grid=(b, nt),
        in_specs=[
            pl.BlockSpec((None, N_MOD, d), lambda i, j: (i, 0, 0)),
            _layer_spec((1, d), layer),
            tok(d), tok(ATT_WIDTH), tok(RWKV_WIDTH), tok(CONV_WIDTH),
            _const_spec(w_mix_out.shape),
            *_ffn_weight_specs(d, d_ff),
            *c_in,
        ],
        out_specs=[tok(d), *c_out],
        out_shape=[jax.ShapeDtypeStruct((b, t, d), F32), *c_shapes],
        compiler_params=_params("arbitrary", "arbitrary"),
        name="mix_out_ffn2",
    )(mod9, g_ffn, h, y_att, y_rwkv, y_conv, w_mix_out, w_in, w_in, w_out, *[a for a, _ in cast_sources])
    return outs[0], outs[1:]


def _head_rms(x, ones_blk, gain):
    ss = _dot((x * x).astype(BF16), ones_blk)
    return x * lax.rsqrt(ss * (1.0 / HEAD_DIM) + EPS) * gain


def _attn_kernel(sink_ref, qg_ref, kg_ref, bias_ref, onesq_ref, onesk_ref, densel_ref,
                 q_ref, kvc_ref, kvp_ref, o_ref):
    n = pl.program_id(1)
    blk = ATT_BLOCK
    n_sub = q_ref.shape[0] // blk
    rows = 2 * blk
    low = lax.broadcasted_iota(jnp.int32, (1, 2 * HEAD_DIM), 1) < HEAD_DIM
    top = lax.broadcasted_iota(jnp.int32, (rows, 1), 0) < blk

    q = q_ref[...].astype(F32)
    qn = _head_rms(q, onesq_ref[...], qg_ref[...]).astype(BF16)
    kv = jnp.concatenate([kvp_ref[...], kvc_ref[...]], axis=0).astype(F32)
    k = kv[:, 0:ATT_KV_WIDTH]
    v = kv[:, ATT_KV_WIDTH:2 * ATT_KV_WIDTH]
    kn = _head_rms(k, onesk_ref[...], kg_ref[...])
    kr = pltpu.roll(kn, HEAD_DIM, 1)
    vr = pltpu.roll(v, HEAD_DIM, 1)
    zero = jnp.zeros_like(kn)
    bf = lambda z: z.astype(BF16)
    k_low = [bf(jnp.where(low, kn, zero)), bf(jnp.where(low, kr, zero))]
    k_high = [bf(jnp.where(low, zero, kr)), bf(jnp.where(low, zero, kn))]
    v_low = [bf(jnp.where(low, v, zero)), bf(jnp.where(low, vr, zero))]
    v_high = [bf(jnp.where(low, zero, vr)), bf(jnp.where(low, zero, v))]
    den_sel = densel_ref[...]

    for j in range(n_sub):
        bias = bias_ref[jnp.minimum(n, 1)] if j == 0 else bias_ref[1]
        keys = slice(j * blk, (j + 2) * blk)
        for g in range(ATT_KV_HEADS):
            qj = qn[j * blk:(j + 1) * blk]
            qg = jnp.concatenate([qj[:, 256 * g:256 * g + 128], qj[:, 256 * g + 128:256 * g + 256]], axis=0)
            kcat = jnp.concatenate([k_low[g][keys], k_high[g][keys]], axis=0)
            vcat = jnp.concatenate([v_low[g][keys], v_high[g][keys]], axis=0)
            s = _dot_nt(qg, kcat) + bias
            h0 = 4 * g
            sink_e = jnp.where(top, sink_ref[h0], sink_ref[h0 + 2])
            sink_o = jnp.where(top, sink_ref[h0 + 1], sink_ref[h0 + 3])
            s_e = s[:, 0:2 * blk]
            s_o = s[:, 2 * blk:4 * blk]
            m_e = jnp.maximum(jnp.max(s_e, axis=-1, keepdims=True), sink_e)
            m_o = jnp.maximum(jnp.max(s_o, axis=-1, keepdims=True), sink_o)
            p = bf(jnp.concatenate([jnp.exp2(s_e - m_e), jnp.exp2(s_o - m_o)], axis=1))
            nd = _dot(p, jnp.concatenate([vcat, den_sel], axis=1))
            den = nd[:, 2 * HEAD_DIM:] + jnp.where(low, jnp.exp2(sink_e - m_e), jnp.exp2(sink_o - m_o))
            out = (nd[:, 0:2 * HEAD_DIM] / den).astype(o_ref.dtype)
            o_ref[j * blk:(j + 1) * blk, 256 * g:256 * g + 128] = out[0:blk]
            o_ref[j * blk:(j + 1) * blk, 256 * g + 128:256 * g + 256] = out[blk:rows]


def _attention_constants():
    blk = ATT_BLOCK
    ri = jnp.arange(2 * blk)[:, None] % blk
    cj = jnp.arange(4 * blk)[None, :] % (2 * blk)
    band = (cj > ri) & (cj <= ri + blk)
    bias = jnp.stack([jnp.where(band & (cj >= blk), 0.0, NEG_BIG), jnp.where(band, 0.0, NEG_BIG)]).astype(F32)
    ones = lambda n: (jnp.arange(n)[:, None] // HEAD_DIM == jnp.arange(n)[None, :] // HEAD_DIM).astype(BF16)
    den_sel = (jnp.arange(4 * blk)[:, None] // (2 * blk) == jnp.arange(2 * HEAD_DIM)[None, :] // HEAD_DIM)
    return bias, ones(ATT_WIDTH), ones(ATT_KV_WIDTH), den_sel.astype(BF16)


def _attention(p_att, q_gain, k_gain, sinks):
    b, t, _ = p_att.shape
    blk = ATT_BLOCK
    tq = min(ATT_TILE, t)
    n_sub = tq // blk
    tile_gain = lambda g, reps: jnp.tile(g.astype(F32), reps).reshape(1, reps * HEAD_DIM)
    log2e = math.log2(math.e)
    bias, ones_q, ones_k, den_sel = _attention_constants()
    kv_col = ATT_WIDTH // (2 * ATT_KV_WIDTH)
    return pl.pallas_call(
        _attn_kernel,
        grid=(b, t // tq),
        in_specs=[
            pl.BlockSpec(memory_space=pltpu.SMEM),
            _const_spec((1, ATT_WIDTH)),
            _const_spec((1, ATT_KV_WIDTH)),
            _const_spec(bias.shape),
            _const_spec(ones_q.shape),
            _const_spec(ones_k.shape),
            _const_spec(den_sel.shape),
            pl.BlockSpec((None, tq, ATT_WIDTH), lambda i, n: (i, n, 0)),
            pl.BlockSpec((None, tq, 2 * ATT_KV_WIDTH), lambda i, n: (i, n, kv_col)),
            pl.BlockSpec((None, blk, 2 * ATT_KV_WIDTH), lambda i, n: (i, jnp.maximum(n * n_sub - 1, 0), kv_col)),
        ],
        out_specs=pl.BlockSpec((None, tq, ATT_WIDTH), lambda i, n: (i, n, 0)),
        out_shape=jax.ShapeDtypeStruct((b, t, ATT_WIDTH), BF16),
        compiler_params=_params("arbitrary", "arbitrary"),
        name="swa_sink_attention",
    )(sinks.astype(F32) * log2e, tile_gain(q_gain, ATT_Q_HEADS) * (HEAD_DIM ** -0.5 * log2e),
      tile_gain(k_gain, ATT_KV_HEADS), bias, ones_q, ones_k, den_sel, p_att, p_att, p_att)


def _rwkv_kernel(p_ref, mu_ref, w0_ref, ww2_ref, a0_ref, wa2_ref, wg2_ref, kk_ref, ka_ref, rk_ref,
                 gnw_ref, gnb_ref, o_ref, prev_ref, state_ref, *, c_len):
    n_seq, tt, _ = p_ref.shape
    width = RWKV_WIDTH
    n_heads = RWKV_HEADS
    n_ch = tt // c_len
    bf = lambda z: z.astype(BF16)
    each = lambda f, *cols: [f(*args) for args in zip(*cols)]

    @pl.when(pl.program_id(1) == 0)
    def _():
        prev_ref[...] = jnp.zeros_like(prev_ref)
        state_ref[...] = jnp.zeros_like(state_ref)

    ones_head = _block_ones(width, HEAD_DIM)

    def head_sum(z):
        return _dot(bf(z), ones_head)

    tri_r = lax.broadcasted_iota(jnp.int32, (tt, tt), 0)
    tri_c = lax.broadcasted_iota(jnp.int32, (tt, tt), 1)
    tri = jnp.where((tri_c <= tri_r) & (tri_c // c_len == tri_r // c_len), 1.0, 0.0).astype(BF16)
    tri2 = jnp.concatenate([tri, tri], axis=1)
    trow = lax.broadcasted_iota(jnp.int32, (tt, 1), 0)
    lane_head = lax.broadcasted_iota(jnp.int32, (1, width), 1) // HEAD_DIM

    def stack(z):
        zero = jnp.zeros_like(z)
        return jnp.concatenate([jnp.where(lane_head == h, z, zero) for h in range(n_heads)], axis=0)

    def head_transpose(z_bd):
        zt = jnp.transpose(z_bd.astype(F32))
        out = zt[0:HEAD_DIM]
        for h in range(1, n_heads):
            out = out + zt[h * HEAD_DIM:(h + 1) * HEAD_DIM]
        return bf(out)

    sl = [slice(j * c_len, (j + 1) * c_len) for j in range(n_ch)]
    cut = lambda z: [z[s] for s in sl]

    def token_features(s):
        p = p_ref[s]
        p_prev = jnp.where(trow == 0, prev_ref[s, 0:1, :], pltpu.roll(p, 1, 0))
        prev_ref[s, 0:1, :] = p[tt - 1:tt, :]
        xs = p + mu_ref[...] * (p_prev - p)
        r = xs[:, 0:width]
        k = xs[:, width:2 * width]
        v = xs[:, 2 * width:3 * width]
        lora = xs[:, 3 * width:3 * width + RWKV_LORA]
        gate_in = xs[:, 3 * width + RWKV_LORA:]
        dw = _dot(bf(jnp.tanh(lora)), ww2_ref[...])
        da = _dot(bf(lora), wa2_ref[...])
        g = _dot(bf(jax.nn.sigmoid(gate_in)), wg2_ref[...])
        lw = -EXP_M05 * jax.nn.sigmoid(w0_ref[...] + dw)
        a = jax.nn.sigmoid(a0_ref[...] + da)
        kk_raw = k * kk_ref[...]
        kk = kk_raw * lax.rsqrt(jnp.maximum(head_sum(kk_raw * kk_raw), 1e-24))
        kmod = k * (1.0 + (a - 1.0) * ka_ref[...])
        b = kk * a
        lw_hi = bf(lw)
        lw_lo = bf(lw - lw_hi.astype(F32))
        lc = _dot(tri2, jnp.concatenate([lw_hi, lw_lo], axis=0))
        w_inv = jnp.exp(-lc)
        lc_c = cut(lc)
        ltot = [z[c_len - 1:c_len, :] for z in lc_c]
        w_end = each(lambda lt, lcj: jnp.exp(lt - lcj), ltot, lc_c)
        chunks = dict(
            ltot=ltot,
            at=cut(bf(kk * jnp.exp(lc - lw))),
            rt=cut(r * jnp.exp(lc)),
            bt=cut(bf(b * w_inv)),
            kt=cut(bf(kmod * w_inv)),
            v=cut(bf(v)),
            bh=each(lambda z, w: bf(z * w), cut(b), w_end),
            kh=each(lambda z, w: bf(z * w), cut(kmod), w_end),
        )
        return chunks, (r, kmod, v, g)

    feats = [token_features(s) for s in range(n_seq)]
    col = lambda name: [z for chunks, _ in feats for z in chunks[name]]
    ltot, at_b, rt_c = col("ltot"), col("at"), col("rt")
    rt_b = each(bf, rt_c)
    bt_bd, kt_bd, v_bd, at_bd = (each(stack, col(nm)) for nm in ("bt", "kt", "v", "at"))
    bh_bd, kh_bd = each(stack, col("bh")), each(stack, col("kh"))

    mi = lax.broadcasted_iota(jnp.int32, (c_len, width), 0)
    mj = lax.broadcasted_iota(jnp.int32, (c_len, width), 1) & (c_len - 1)
    strict = mj < mi
    incl = mj <= mi
    eye = mj == mi

    rows2 = lambda x, y: jnp.concatenate([x, y], axis=0)
    top, bot = (lambda z: z[0:c_len]), (lambda z: z[c_len:2 * c_len])
    ar_b = each(rows2, at_b, rt_b)
    g_b = each(_dot_nt, ar_b, bt_bd)
    g_k = each(_dot_nt, ar_b, kt_bd)
    a_ab = each(lambda z: jnp.where(strict, top(z), 0.0), g_b)
    a_rb = each(lambda z: bf(jnp.where(incl, bot(z), 0.0)), g_b)
    a_ak = each(lambda z: bf(jnp.where(strict, top(z), 0.0)), g_k)
    a_rk = each(lambda z: bf(jnp.where(incl, bot(z), 0.0)), g_k)
    bh_t = each(head_transpose, bh_bd)
    kh_t = each(head_transpose, kh_bd)
    on_v = each(lambda x, y, z, w: _dot(jnp.concatenate([x, y, z], axis=0), w), a_ak, a_rk, kh_t, v_bd)
    y1_bd = each(lambda z: stack(bf(top(z))), on_v)

    t_inv = each(lambda z: jnp.where(eye, 1.0, 0.0) - z, a_ab)
    pw = each(bf, a_ab)
    pw = each(lambda x: bf(_dot(x, stack(x))), pw)
    n_sq = int(math.log2(c_len)) - 1
    for lvl in range(n_sq):
        pw_bd = each(stack, pw)
        if lvl + 1 < n_sq:
            res = each(lambda t, x, y: _dot(rows2(bf(t), x), y), t_inv, pw, pw_bd)
            t_inv = each(lambda t, z: t + top(z), t_inv, res)
            pw = each(lambda z: bf(bot(z)), res)
        else:
            t_inv = each(lambda t, y: t + _dot(bf(t), y), t_inv, pw_bd)
    t_b = each(bf, t_inv)

    at2_bd = each(lambda x, y: stack(bf(_dot(x, y))), t_b, at_bd)
    u2_bd = each(lambda x, y: stack(bf(_dot(x, y))), t_b, y1_bd)
    ab_t = each(rows2, a_rb, bh_t)
    on_at2 = each(_dot, ab_t, at2_bd)
    on_u2 = each(_dot, ab_t, u2_bd)
    r2 = each(lambda z, m: bf(z - top(m)), rt_c, on_at2)
    p_t = each(lambda lt, m: bf(jnp.where(eye, jnp.exp(lt), 0.0) - bot(m)), ltot, on_at2)
    o2 = each(lambda z, m: z[c_len:2 * c_len] - top(m), on_v, on_u2)
    q_t = each(lambda z, m: z[2 * c_len:3 * c_len] - bot(m), on_v, on_u2)
    rp = each(rows2, r2, p_t)

    states = [state_ref[s] for s in range(n_seq)]
    ys = [[] for _ in range(n_seq)]
    for j in range(n_ch):
        for s in range(n_seq):
            i = s * n_ch + j
            res = _dot(rp[i], stack(bf(states[s])))
            ys[s].append(top(res) + o2[i])
            states[s] = bot(res) + q_t[i]

    for s in range(n_seq):
        state_ref[s] = states[s]
        r, kmod, v, g = feats[s][1]
        y = jnp.concatenate(ys[s], axis=0) if n_ch > 1 else ys[s][0]
        mean = head_sum(y) * (1.0 / HEAD_DIM)
        dev = y - mean
        var = head_sum(dev * dev) * (1.0 / HEAD_DIM)
        yn = dev * lax.rsqrt(var + RWKV_GN_EPS) * gnw_ref[...] + gnb_ref[...]
        bonus = head_sum(r * kmod * rk_ref[...]) * v
        o_ref[s] = ((yn + bonus) * g).astype(o_ref.dtype)


def _rwkv(p_rwkv, mu, w0, ww2_pad, a0, wa2_pad, wg2, k_k, k_a, r_k, gn_w, gn_b, layer):
    b, t, pw = p_rwkv.shape
    tt = min(RWKV_TILE, t)
    c_len = min(RWKV_CHUNK, tt)
    n_seq = RWKV_SEQS if b % RWKV_SEQS == 0 else 1
    vec = lambda n: _layer_spec((1, n), layer)
    return pl.pallas_call(
        functools.partial(_rwkv_kernel, c_len=c_len),
        grid=(b // n_seq, t // tt),
        in_specs=[
            pl.BlockSpec((n_seq, tt, pw), lambda i, c: (i, c, 0)),
            vec(pw), vec(RWKV_WIDTH),
            _layer_spec((RWKV_LORA, RWKV_WIDTH), layer),
            vec(RWKV_WIDTH),
            _layer_spec((RWKV_LORA, RWKV_WIDTH), layer),
            _layer_spec((RWKV_LORA, RWKV_WIDTH), layer),
            vec(RWKV_WIDTH), vec(RWKV_WIDTH), vec(RWKV_WIDTH), vec(RWKV_WIDTH), vec(RWKV_WIDTH),
        ],
        out_specs=pl.BlockSpec((n_seq, tt, RWKV_WIDTH), lambda i, c: (i, c, 0)),
        out_shape=jax.ShapeDtypeStruct((b, t, RWKV_WIDTH), BF16),
        scratch_shapes=[pltpu.VMEM((n_seq, 8, pw), F32), pltpu.VMEM((n_seq, HEAD_DIM, RWKV_WIDTH), F32)],
        compiler_params=_params("arbitrary", "arbitrary"),
        name="rwkv7_chunked",
    )(p_rwkv, mu, w0, ww2_pad, a0, wa2_pad, wg2, k_k, k_a, r_k, gn_w, gn_b)


def kernel(x, c, w_ada, b_ada, g_ffn1, w_ffn1_in, w_ffn1_out, g_mix, w_mix_in, w_mix_out, att_q_gain, att_k_gain, att_sinks, rwkv_mu, rwkv_w0, rwkv_w_w2, rwkv_a0, rwkv_a_w2, rwkv_g_w2, rwkv_k_k, rwkv_k_a, rwkv_r_k, rwkv_gn_w, rwkv_gn_b, conv_w, g_ffn2, w_ffn2_in, w_ffn2_out):
    n_layers, d = g_ffn1.shape
    bsz = x.shape[0]
    row3 = lambda z: z.astype(F32).reshape(n_layers, 1, -1)
    bf = lambda z: z.astype(BF16)

    mod = _modulation(c, w_ada, b_ada).reshape(n_layers, bsz, N_MOD, d)

    half = RWKV_LORA // 2
    zeros = jnp.zeros((n_layers, half, RWKV_WIDTH), F32)
    ww2_pad = bf(jnp.concatenate([rwkv_w_w2, zeros], axis=1))
    wa2_pad = bf(jnp.concatenate([zeros, rwkv_a_w2], axis=1))
    wg2 = bf(rwkv_g_w2)
    g1, gm, g2 = row3(g_ffn1), row3(g_mix), row3(g_ffn2)
    mu, w0, a0 = row3(rwkv_mu), row3(rwkv_w0), row3(rwkv_a0)
    k_k, k_a, r_k = row3(rwkv_k_k), row3(rwkv_k_a), row3(rwkv_r_k)
    gn_w, gn_b = row3(rwkv_gn_w), row3(rwkv_gn_b)

    first = lambda l: [(w_ffn1_in, l), (w_ffn1_out, l), (w_mix_in, l)]
    second = lambda l: [(w_ffn2_in, l), (w_ffn2_out, l), (w_mix_out, l)]
    w1_in, w1_out, w_mix_in_b = bf(w_ffn1_in[0]), bf(w_ffn1_out[0]), bf(w_mix_in[0])

    h = x
    for l in range(n_layers):
        (h, p_att, p_rwkv, y_conv), (w2_in, w2_out, w_mix_out_b) = _ffn_mixin(
            h, mod[l], g1, gm, w1_in, w1_out, w_mix_in_b, conv_w, l, second(l))
        y_att = _attention(p_att, att_q_gain[l], att_k_gain[l], att_sinks[l])
        y_rwkv = _rwkv(p_rwkv, mu, w0, ww2_pad, a0, wa2_pad, wg2, k_k, k_a, r_k, gn_w, gn_b, l)
        h, nxt = _mixout_ffn(h, mod[l], g2, y_att, y_rwkv, y_conv, w_mix_out_b, w2_in, w2_out, l,
                             first(l + 1) if l + 1 < n_layers else [])
        if nxt:
            w1_in, w1_out, w_mix_in_b = nxt
    return h
```

```python
import functools
import math

import jax
import jax.numpy as jnp
from jax import lax
from jax.experimental import pallas as pl
from jax.experimental.pallas import tpu as pltpu

F32 = jnp.float32
BF16 = jnp.bfloat16

HEAD_DIM = 64
ATT_Q_HEADS = 8
ATT_KV_HEADS = 2
ATT_WIDTH = ATT_Q_HEADS * HEAD_DIM
ATT_KV_WIDTH = ATT_KV_HEADS * HEAD_DIM
ATT_PROJ_WIDTH = ATT_WIDTH + 2 * ATT_KV_WIDTH
ATT_BLOCK = 128
ATT_TILE = 2048
RWKV_HEADS = 4
RWKV_WIDTH = RWKV_HEADS * HEAD_DIM
RWKV_LORA = 128
RWKV_PROJ_WIDTH = 3 * RWKV_WIDTH + 2 * RWKV_LORA
RWKV_GN_EPS = 64e-5
RWKV_CHUNK = 64
RWKV_TILE = 256
RWKV_SEQS = 4
CONV_WIDTH = 256
CONV_K = 3
N_MOD = 9
EPS = 1e-6
NEG_BIG = -1e30
EXP_M05 = math.exp(-0.5)

MXU_WIDTH = 256
BF16_SUBLANES = 16
TOKEN_TILE = 512
VMEM_LIMIT = 56 * 1024 * 1024


def _dot(a, b):
    return jnp.dot(a, b, preferred_element_type=F32)


def _dot_nt(a, b):
    return lax.dot_general(a, b, (((1,), (1,)), ((), ())), preferred_element_type=F32)


def _block_ones(n, blk):
    r = lax.broadcasted_iota(jnp.int32, (n, n), 0) // blk
    c = lax.broadcasted_iota(jnp.int32, (n, n), 1) // blk
    return jnp.where(r == c, 1.0, 0.0).astype(BF16)


def _const_spec(shape):
    nd = len(shape)
    return pl.BlockSpec(shape, lambda *_: (0,) * nd, pipeline_mode=pl.Buffered(1))


def _layer_spec(shape, layer):
    nd = len(shape)
    return pl.BlockSpec((None,) + tuple(shape), lambda *_: (layer,) + (0,) * nd,
                        pipeline_mode=pl.Buffered(1))


def _params(*sem):
    return pltpu.CompilerParams(dimension_semantics=sem, vmem_limit_bytes=VMEM_LIMIT)


def _modulated_norm(x, gain, shift, scale):
    ms = jnp.mean(x * x, axis=-1, keepdims=True)
    return (x * lax.rsqrt(ms + EPS) * gain) * (1.0 + scale) + shift


def _mod_kernel(c_ref, w_ref, b_ref, o_ref):
    c = c_ref[...]
    act = (c * jax.nn.sigmoid(c)).astype(BF16)
    o_ref[...] = _dot(act, w_ref[...].astype(BF16)) + b_ref[...]


def _modulation(c, w_ada, b_ada):
    n_layers, d, n = w_ada.shape
    b = c.shape[0]
    tn = d
    return pl.pallas_call(
        _mod_kernel,
        grid=(n_layers, n // tn),
        in_specs=[
            pl.BlockSpec((b, d), lambda l, j: (0, 0)),
            pl.BlockSpec((None, d, tn), lambda l, j: (l, 0, j)),
            pl.BlockSpec((None, 1, tn), lambda l, j: (l, 0, j)),
        ],
        out_specs=pl.BlockSpec((None, b, tn), lambda l, j: (l, 0, j)),
        out_shape=jax.ShapeDtypeStruct((n_layers, b, n), F32),
        compiler_params=_params("arbitrary", "arbitrary"),
        name="adaln_mod",
    )(c, w_ada, b_ada.reshape(n_layers, 1, n))


def _swiglu_residual(x, shift, scale, gate, gain, wg_ref, wu_ref, wo_ref, n_chunks):
    hn = _modulated_norm(x, gain, shift, scale).astype(BF16)
    d_ff = wg_ref.shape[1]
    n_tiles = -(-d_ff // MXU_WIDTH)
    edges = [min(d_ff, MXU_WIDTH * ((n_tiles * j + n_chunks - 1) // n_chunks)) for j in range(n_chunks + 1)]
    acc = None
    for j in range(n_chunks):
        sl = slice(edges[j], edges[j + 1])
        g = _dot(hn, wg_ref[:, sl])
        up = _dot(hn, wu_ref[:, sl])
        act = (g * jax.nn.sigmoid(g) * up).astype(BF16)
        part = _dot(act, wo_ref[sl, :])
        acc = part if acc is None else acc + part
    return x + (0.5 * (1.0 + gate)) * acc


def _cast_jobs(sources, n_steps, grid_cols):
    in_specs, out_specs, out_shapes = [], [], []
    for arr, layer in sources:
        _, rows, cols = arr.shape
        rb = min(r for r in range(BF16_SUBLANES, rows + 1, BF16_SUBLANES) if rows % r == 0 and rows // r <= n_steps)
        nblk = rows // rb
        blk = lambda i, j, nblk=nblk: jnp.minimum(i * grid_cols + j, nblk - 1)
        in_specs.append(pl.BlockSpec((None, rb, cols), lambda i, j, blk=blk, layer=layer: (layer, blk(i, j), 0)))
        out_specs.append(pl.BlockSpec((rb, cols), lambda i, j, blk=blk: (blk(i, j), 0)))
        out_shapes.append(jax.ShapeDtypeStruct((rows, cols), BF16))
    return in_specs, out_specs, out_shapes


def _run_casts(src_refs, dst_refs):
    for src, dst in zip(src_refs, dst_refs):
        dst[...] = src[...].astype(dst.dtype)


def _ffn_mixin_kernel(*refs, n_chunks, n_cast):
    (mod_ref, g1_ref, gm_ref, x_ref, wg_ref, wu_ref, wo_ref, wmix_ref, cw_ref) = refs[:9]
    cast_src = refs[9:9 + n_cast]
    h_ref, patt_ref, prwkv_ref, yconv_ref = refs[9 + n_cast:13 + n_cast]
    cast_dst = refs[13 + n_cast:13 + 2 * n_cast]
    carry_ref = refs[13 + 2 * n_cast]

    @pl.when(pl.program_id(1) == 0)
    def _():
        carry_ref[...] = jnp.zeros_like(carry_ref)

    h = _swiglu_residual(x_ref[...], mod_ref[0:1, :], mod_ref[1:2, :], mod_ref[2:3, :], g1_ref[...],
                         wg_ref, wu_ref, wo_ref, n_chunks)
    h_ref[...] = h
    hn = _modulated_norm(h, gm_ref[...], mod_ref[3:4, :], mod_ref[4:5, :]).astype(BF16)
    a_end = ATT_PROJ_WIDTH
    r_end = a_end + RWKV_PROJ_WIDTH
    patt_ref[...] = _dot(hn, wmix_ref[:, 0:a_end]).astype(patt_ref.dtype)
    prwkv_ref[...] = _dot(hn, wmix_ref[:, a_end:r_end])
    pc = _dot(hn, wmix_ref[:, r_end:])
    cwid = yconv_ref.shape[-1]
    b_gate = pc[:, 0:cwid]
    u = pc[:, cwid:2 * cwid] * pc[:, 2 * cwid:3 * cwid]
    tm = u.shape[0]
    row = lax.broadcasted_iota(jnp.int32, (tm, 1), 0)
    prev1 = carry_ref[1:2, :]
    prev2 = carry_ref[0:1, :]
    u1 = jnp.where(row == 0, prev1, pltpu.roll(u, 1, 0))
    u2 = jnp.where(row == 0, prev2, jnp.where(row == 1, prev1, pltpu.roll(u, 2, 0)))
    y = cw_ref[0:1, :] * u2 + cw_ref[1:2, :] * u1 + cw_ref[2:3, :] * u
    yconv_ref[...] = (b_gate * y).astype(yconv_ref.dtype)
    carry_ref[0:2, :] = u[tm - 2:tm, :]
    _run_casts(cast_src, cast_dst)


def _ffn_chunks(d_ff):
    return 2 if d_ff % 256 == 0 else 1


def _ffn_weight_specs(d, d_ff):
    gate = pl.BlockSpec((d, d_ff), lambda i, j: (0, 0), pipeline_mode=pl.Buffered(1))
    up = pl.BlockSpec((d, d_ff), lambda i, j: (0, 1), pipeline_mode=pl.Buffered(1))
    return [gate, up, _const_spec((d_ff, d))]


def _ffn_mixin(h, mod9, g_ffn, g_mix, w_in, w_out, w_mix_in, conv_w, layer, cast_sources):
    b, t, d = h.shape
    d_ff = w_out.shape[0]
    tm = min(TOKEN_TILE, t)
    nt = t // tm
    tok = lambda width: pl.BlockSpec((None, tm, width), lambda i, j: (i, j, 0))
    c_in, c_out, c_shapes = _cast_jobs(cast_sources, b * nt, nt)
    outs = pl.pallas_call(
        functools.partial(_ffn_mixin_kernel, n_chunks=_ffn_chunks(d_ff), n_cast=len(cast_sources)),
        grid=(b, nt),
        in_specs=[
            pl.BlockSpec((None, N_MOD, d), lambda i, j: (i, 0, 0)),
            _layer_spec((1, d), layer),
            _layer_spec((1, d), layer),
            tok(d),
            *_ffn_weight_specs(d, d_ff),
            _const_spec(w_mix_in.shape),
            _layer_spec((CONV_K, CONV_WIDTH), layer),
            *c_in,
        ],
        out_specs=[tok(d), tok(ATT_PROJ_WIDTH), tok(RWKV_PROJ_WIDTH), tok(CONV_WIDTH), *c_out],
        out_shape=[
            jax.ShapeDtypeStruct((b, t, d), F32),
            jax.ShapeDtypeStruct((b, t, ATT_PROJ_WIDTH), BF16),
            jax.ShapeDtypeStruct((b, t, RWKV_PROJ_WIDTH), F32),
            jax.ShapeDtypeStruct((b, t, CONV_WIDTH), BF16),
            *c_shapes,
        ],
        scratch_shapes=[pltpu.VMEM((8, CONV_WIDTH), F32)],
        compiler_params=_params("arbitrary", "arbitrary"),
        name="ffn1_mix_in",
    )(mod9, g_ffn, g_mix, h, w_in, w_in, w_out, w_mix_in, conv_w, *[a for a, _ in cast_sources])
    return outs[:4], outs[4:]


def _mixout_ffn_kernel(*refs, n_chunks, n_cast):
    (mod_ref, g2_ref, h_ref, ya_ref, yr_ref, yc_ref, wmix_ref, wg_ref, wu_ref, wo_ref) = refs[:10]
    cast_src = refs[10:10 + n_cast]
    o_ref = refs[10 + n_cast]
    cast_dst = refs[11 + n_cast:11 + 2 * n_cast]
    wa = ATT_WIDTH
    wr = wa + RWKV_WIDTH
    mixed = (_dot(ya_ref[...], wmix_ref[0:wa, :]) + _dot(yr_ref[...], wmix_ref[wa:wr, :])
             + _dot(yc_ref[...], wmix_ref[wr:, :]))
    h = h_ref[...] + (1.0 + mod_ref[5:6, :]) * mixed
    o_ref[...] = _swiglu_residual(h, mod_ref[6:7, :], mod_ref[7:8, :], mod_ref[8:9, :], g2_ref[...],
                                  wg_ref, wu_ref, wo_ref, n_chunks)
    _run_casts(cast_src, cast_dst)


def _mixout_ffn(h, mod9, g_ffn, y_att, y_rwkv, y_conv, w_mix_out, w_in, w_out, layer, cast_sources):
    b, t, d = h.shape
    d_ff = w_out.shape[0]
    tm = min(TOKEN_TILE, t)
    nt = t // tm
    tok = lambda width: pl.BlockSpec((None, tm, width), lambda i, j: (i, j, 0))
    c_in, c_out, c_shapes = _cast_jobs(cast_sources, b * nt, nt)
    outs = pl.pallas_call(
        functools.partial(_mixout_ffn_kernel, n_chunks=_ffn_chunks(d_ff), n_cast=len(cast_sources)),
        grid=(b, nt),
        in_specs=[
            pl.BlockSpec((None, N_MOD, d), lambda i, j: (i, 0, 0)),
            _layer_spec((1, d), layer),
            tok(d), tok(ATT_WIDTH), tok(RWKV_WIDTH), tok(CONV_WIDTH),
            _const_spec(w_mix_out.shape),
            *_ffn_weight_specs(d, d_ff),
            *c_in,
        ],
        out_specs=[tok(d), *c_out],
        out_shape=[jax.ShapeDtypeStruct((b, t, d), F32), *c_shapes],
        compiler_params=_params("arbitrary", "arbitrary"),
        name="mix_out_ffn2",
    )(mod9, g_ffn, h, y_att, y_rwkv, y_conv, w_mix_out, w_in, w_in, w_out, *[a for a, _ in cast_sources])
    return outs[0], outs[1:]


def _head_rms(x, ones_blk, gain):
    ss = _dot((x * x).astype(BF16), ones_blk)
    return x * lax.rsqrt(ss * (1.0 / HEAD_DIM) + EPS) * gain


def _attn_kernel(sink_ref, qg_ref, kg_ref, bias_ref, onesq_ref, onesk_ref, densel_ref,
                 q_ref, kvc_ref, kvp_ref, o_ref):
    n = pl.program_id(1)
    blk = ATT_BLOCK
    n_sub = q_ref.shape[0] // blk
    rows = 2 * blk
    low = lax.broadcasted_iota(jnp.int32, (1, 2 * HEAD_DIM), 1) < HEAD_DIM
    top = lax.broadcasted_iota(jnp.int32, (rows, 1), 0) < blk

    q = q_ref[...].astype(F32)
    qn = _head_rms(q, onesq_ref[...], qg_ref[...]).astype(BF16)
    kv = jnp.concatenate([kvp_ref[...], kvc_ref[...]], axis=0).astype(F32)
    k = kv[:, 0:ATT_KV_WIDTH]
    v = kv[:, ATT_KV_WIDTH:2 * ATT_KV_WIDTH]
    kn = _head_rms(k, onesk_ref[...], kg_ref[...])
    kr = pltpu.roll(kn, HEAD_DIM, 1)
    vr = pltpu.roll(v, HEAD_DIM, 1)
    zero = jnp.zeros_like(kn)
    bf = lambda z: z.astype(BF16)
    k_low = [bf(jnp.where(low, kn, zero)), bf(jnp.where(low, kr, zero))]
    k_high = [bf(jnp.where(low, zero, kr)), bf(jnp.where(low, zero, kn))]
    v_low = [bf(jnp.where(low, v, zero)), bf(jnp.where(low, vr, zero))]
    v_high = [bf(jnp.where(low, zero, vr)), bf(jnp.where(low, zero, v))]
    den_sel = densel_ref[...]

    for j in range(n_sub):
        bias = bias_ref[jnp.minimum(n, 1)] if j == 0 else bias_ref[1]
        keys = slice(j * blk, (j + 2) * blk)
        for g in range(ATT_KV_HEADS):
            qj = qn[j * blk:(j + 1) * blk]
            qg = jnp.concatenate([qj[:, 256 * g:256 * g + 128], qj[:, 256 * g + 128:256 * g + 256]], axis=0)
            kcat = jnp.concatenate([k_low[g][keys], k_high[g][keys]], axis=0)
            vcat = jnp.concatenate([v_low[g][keys], v_high[g][keys]], axis=0)
            s = _dot_nt(qg, kcat) + bias
            h0 = 4 * g
            sink_e = jnp.where(top, sink_ref[h0], sink_ref[h0 + 2])
            sink_o = jnp.where(top, sink_ref[h0 + 1], sink_ref[h0 + 3])
            s_e = s[:, 0:2 * blk]
            s_o = s[:, 2 * blk:4 * blk]
            m_e = jnp.maximum(jnp.max(s_e, axis=-1, keepdims=True), sink_e)
            m_o = jnp.maximum(jnp.max(s_o, axis=-1, keepdims=True), sink_o)
            p = bf(jnp.concatenate([jnp.exp2(s_e - m_e), jnp.exp2(s_o - m_o)], axis=1))
            nd = _dot(p, jnp.concatenate([vcat, den_sel], axis=1))
            den = nd[:, 2 * HEAD_DIM:] + jnp.where(low, jnp.exp2(sink_e - m_e), jnp.exp2(sink_o - m_o))
            out = (nd[:, 0:2 * HEAD_DIM] / den).astype(o_ref.dtype)
            o_ref[j * blk:(j + 1) * blk, 256 * g:256 * g + 128] = out[0:blk]
            o_ref[j * blk:(j + 1) * blk, 256 * g + 128:256 * g + 256] = out[blk:rows]


def _attention_constants():
    blk = ATT_BLOCK
    ri = jnp.arange(2 * blk)[:, None] % blk
    cj = jnp.arange(4 * blk)[None, :] % (2 * blk)
    band = (cj > ri) & (cj <= ri + blk)
    bias = jnp.stack([jnp.where(band & (cj >= blk), 0.0, NEG_BIG), jnp.where(band, 0.0, NEG_BIG)]).astype(F32)
    ones = lambda n: (jnp.arange(n)[:, None] // HEAD_DIM == jnp.arange(n)[None, :] // HEAD_DIM).astype(BF16)
    den_sel = (jnp.arange(4 * blk)[:, None] // (2 * blk) == jnp.arange(2 * HEAD_DIM)[None, :] // HEAD_DIM)
    return bias, ones(ATT_WIDTH), ones(ATT_KV_WIDTH), den_sel.astype(BF16)


def _attention(p_att, q_gain, k_gain, sinks):
    b, t, _ = p_att.shape
    blk = ATT_BLOCK
    tq = min(ATT_TILE, t)
    n_sub = tq // blk
    tile_gain = lambda g, reps: jnp.tile(g.astype(F32), reps).reshape(1, reps * HEAD_DIM)
    log2e = math.log2(math.e)
    bias, ones_q, ones_k, den_sel = _attention_constants()
    kv_col = ATT_WIDTH // (2 * ATT_KV_WIDTH)
    return pl.pallas_call(
        _attn_kernel,
        grid=(b, t // tq),
        in_specs=[
            pl.BlockSpec(memory_space=pltpu.SMEM),
            _const_spec((1, ATT_WIDTH)),
            _const_spec((1, ATT_KV_WIDTH)),
            _const_spec(bias.shape),
            _const_spec(ones_q.shape),
            _const_spec(ones_k.shape),
            _const_spec(den_sel.shape),
            pl.BlockSpec((None, tq, ATT_WIDTH), lambda i, n: (i, n, 0)),
            pl.BlockSpec((None, tq, 2 * ATT_KV_WIDTH), lambda i, n: (i, n, kv_col)),
            pl.BlockSpec((None, blk, 2 * ATT_KV_WIDTH), lambda i, n: (i, jnp.maximum(n * n_sub - 1, 0), kv_col)),
        ],
        out_specs=pl.BlockSpec((None, tq, ATT_WIDTH), lambda i, n: (i, n, 0)),
        out_shape=jax.ShapeDtypeStruct((b, t, ATT_WIDTH), BF16),
        compiler_params=_params("arbitrary", "arbitrary"),
        name="swa_sink_attention",
    )(sinks.astype(F32) * log2e, tile_gain(q_gain, ATT_Q_HEADS) * (HEAD_DIM ** -0.5 * log2e),
      tile_gain(k_gain, ATT_KV_HEADS), bias, ones_q, ones_k, den_sel, p_att, p_att, p_att)


def _rwkv_kernel(p_ref, mu_ref, w0_ref, ww2_ref, a0_ref, wa2_ref, wg2_ref, kk_ref, ka_ref, rk_ref,
                 gnw_ref, gnb_ref, o_ref, prev_ref, state_ref, *, c_len):
    n_seq, tt, _ = p_ref.shape
    width = RWKV_WIDTH
    n_heads = RWKV_HEADS
    n_ch = tt // c_len
    bf = lambda z: z.astype(BF16)
    each = lambda f, *cols: [f(*args) for args in zip(*cols)]

    @pl.when(pl.program_id(1) == 0)
    def _():
        prev_ref[...] = jnp.zeros_like(prev_ref)
        state_ref[...] = jnp.zeros_like(state_ref)

    ones_head = _block_ones(width, HEAD_DIM)

    def head_sum(z):
        return _dot(bf(z), ones_head)

    tri_r = lax.broadcasted_iota(jnp.int32, (tt, tt), 0)
    tri_c = lax.broadcasted_iota(jnp.int32, (tt, tt), 1)
    tri = jnp.where((tri_c <= tri_r) & (tri_c // c_len == tri_r // c_len), 1.0, 0.0).astype(BF16)
    tri2 = jnp.concatenate([tri, tri], axis=1)
    trow = lax.broadcasted_iota(jnp.int32, (tt, 1), 0)
    lane_head = lax.broadcasted_iota(jnp.int32, (1, width), 1) // HEAD_DIM

    def stack(z):
        zero = jnp.zeros_like(z)
        return jnp.concatenate([jnp.where(lane_head == h, z, zero) for h in range(n_heads)], axis=0)

    def head_transpose(z_bd):
        zt = jnp.transpose(z_bd.astype(F32))
        out = zt[0:HEAD_DIM]
        for h in range(1, n_heads):
            out = out + zt[h * HEAD_DIM:(h + 1) * HEAD_DIM]
        return bf(out)

    sl = [slice(j * c_len, (j + 1) * c_len) for j in range(n_ch)]
    cut = lambda z: [z[s] for s in sl]

    def token_features(s):
        p = p_ref[s]
        p_prev = jnp.where(trow == 0, prev_ref[s, 0:1, :], pltpu.roll(p, 1, 0))
        prev_ref[s, 0:1, :] = p[tt - 1:tt, :]
        xs = p + mu_ref[...] * (p_prev - p)
        r = xs[:, 0:width]
        k = xs[:, width:2 * width]
        v = xs[:, 2 * width:3 * width]
        lora = xs[:, 3 * width:3 * width + RWKV_LORA]
        gate_in = xs[:, 3 * width + RWKV_LORA:]
        dw = _dot(bf(jnp.tanh(lora)), ww2_ref[...])
        da = _dot(bf(lora), wa2_ref[...])
        g = _dot(bf(jax.nn.sigmoid(gate_in)), wg2_ref[...])
        lw = -EXP_M05 * jax.nn.sigmoid(w0_ref[...] + dw)
        a = jax.nn.sigmoid(a0_ref[...] + da)
        kk_raw = k * kk_ref[...]
        kk = kk_raw * lax.rsqrt(jnp.maximum(head_sum(kk_raw * kk_raw), 1e-24))
        kmod = k * (1.0 + (a - 1.0) * ka_ref[...])
        b = kk * a
        lw_hi = bf(lw)
        lw_lo = bf(lw - lw_hi.astype(F32))
        lc = _dot(tri2, jnp.concatenate([lw_hi, lw_lo], axis=0))
        w_inv = jnp.exp(-lc)
        lc_c = cut(lc)
        ltot = [z[c_len - 1:c_len, :] for z in lc_c]
        w_end = each(lambda lt, lcj: jnp.exp(lt - lcj), ltot, lc_c)
        chunks = dict(
            ltot=ltot,
            at=cut(bf(kk * jnp.exp(lc - lw))),
            rt=cut(r * jnp.exp(lc)),
            bt=cut(bf(b * w_inv)),
            kt=cut(bf(kmod * w_inv)),
            v=cut(bf(v)),
            bh=each(lambda z, w: bf(z * w), cut(b), w_end),
            kh=each(lambda z, w: bf(z * w), cut(kmod), w_end),
        )
        return chunks, (r, kmod, v, g)

    feats = [token_features(s) for s in range(n_seq)]
    col = lambda name: [z for chunks, _ in feats for z in chunks[name]]
    ltot, at_b, rt_c = col("ltot"), col("at"), col("rt")
    rt_b = each(bf, rt_c)
    bt_bd, kt_bd, v_bd, at_bd = (each(stack, col(nm)) for nm in ("bt", "kt", "v", "at"))
    bh_bd, kh_bd = each(stack, col("bh")), each(stack, col("kh"))

    mi = lax.broadcasted_iota(jnp.int32, (c_len, width), 0)
    mj = lax.broadcasted_iota(jnp.int32, (c_len, width), 1) & (c_len - 1)
    strict = mj < mi
    incl = mj <= mi
    eye = mj == mi

    rows2 = lambda x, y: jnp.concatenate([x, y], axis=0)
    top, bot = (lambda z: z[0:c_len]), (lambda z: z[c_len:2 * c_len])
    ar_b = each(rows2, at_b, rt_b)
    g_b = each(_dot_nt, ar_b, bt_bd)
    g_k = each(_dot_nt, ar_b, kt_bd)
    a_ab = each(lambda z: jnp.where(strict, top(z), 0.0), g_b)
    a_rb = each(lambda z: bf(jnp.where(incl, bot(z), 0.0)), g_b)
    a_ak = each(lambda z: bf(jnp.where(strict, top(z), 0.0)), g_k)
    a_rk = each(lambda z: bf(jnp.where(incl, bot(z), 0.0)), g_k)
    bh_t = each(head_transpose, bh_bd)
    kh_t = each(head_transpose, kh_bd)
    on_v = each(lambda x, y, z, w: _dot(jnp.concatenate([x, y, z], axis=0), w), a_ak, a_rk, kh_t, v_bd)
    y1_bd = each(lambda z: stack(bf(top(z))), on_v)

    t_inv = each(lambda z: jnp.where(eye, 1.0, 0.0) - z, a_ab)
    pw = each(bf, a_ab)
    pw = each(lambda x: bf(_dot(x, stack(x))), pw)
    n_sq = int(math.log2(c_len)) - 1
    for lvl in range(n_sq):
        pw_bd = each(stack, pw)
        if lvl + 1 < n_sq:
            res = each(lambda t, x, y: _dot(rows2(bf(t), x), y), t_inv, pw, pw_bd)
            t_inv = each(lambda t, z: t + top(z), t_inv, res)
            pw = each(lambda z: bf(bot(z)), res)
        else:
            t_inv = each(lambda t, y: t + _dot(bf(t), y), t_inv, pw_bd)
    t_b = each(bf, t_inv)

    abt_t = each(lambda x, y, t: bf(_dot(rows2(x, y), stack(t))), a_rb, bh_t, t_b)
    on_at2 = each(_dot, abt_t, at_bd)
    on_u2 = each(_dot, abt_t, y1_bd)
    r2 = each(lambda z, m: bf(z - top(m)), rt_c, on_at2)
    p_t = each(lambda lt, m: bf(jnp.where(eye, jnp.exp(lt), 0.0) - bot(m)), ltot, on_at2)
    o2 = each(lambda z, m: z[c_len:2 * c_len] - top(m), on_v, on_u2)
    q_t = each(lambda z, m: z[2 * c_len:3 * c_len] - bot(m), on_v, on_u2)
    rp = each(rows2, r2, p_t)

    states = [state_ref[s] for s in range(n_seq)]
    ys = [[] for _ in range(n_seq)]
    for j in range(n_ch):
        for s in range(n_seq):
            i = s * n_ch + j
            res = _dot(rp[i], stack(bf(states[s])))
            ys[s].append(top(res) + o2[i])
            states[s] = bot(res) + q_t[i]

    for s in range(n_seq):
        state_ref[s] = states[s]
        r, kmod, v, g = feats[s][1]
        y = jnp.concatenate(ys[s], axis=0) if n_ch > 1 else ys[s][0]
        mean = head_sum(y) * (1.0 / HEAD_DIM)
        dev = y - mean
        var = head_sum(dev * dev) * (1.0 / HEAD_DIM)
        yn = dev * lax.rsqrt(var + RWKV_GN_EPS) * gnw_ref[...] + gnb_ref[...]
        bonus = head_sum(r * kmod * rk_ref[...]) * v
        o_ref[s] = ((yn + bonus) * g).astype(o_ref.dtype)


def _rwkv(p_rwkv, mu, w0, ww2_pad, a0, wa2_pad, wg2, k_k, k_a, r_k, gn_w, gn_b, layer):
    b, t, pw = p_rwkv.shape
    tt = min(RWKV_TILE, t)
    c_len = min(RWKV_CHUNK, tt)
    n_seq = RWKV_SEQS if b % RWKV_SEQS == 0 else 1
    vec = lambda n: _layer_spec((1, n), layer)
    return pl.pallas_call(
        functools.partial(_rwkv_kernel, c_len=c_len),
        grid=(b // n_seq, t // tt),
        in_specs=[
            pl.BlockSpec((n_seq, tt, pw), lambda i, c: (i, c, 0)),
            vec(pw), vec(RWKV_WIDTH),
            _layer_spec((RWKV_LORA, RWKV_WIDTH), layer),
            vec(RWKV_WIDTH),
            _layer_spec((RWKV_LORA, RWKV_WIDTH), layer),
            _layer_spec((RWKV_LORA, RWKV_WIDTH), layer),
            vec(RWKV_WIDTH), vec(RWKV_WIDTH), vec(RWKV_WIDTH), vec(RWKV_WIDTH), vec(RWKV_WIDTH),
        ],
        out_specs=pl.BlockSpec((n_seq, tt, RWKV_WIDTH), lambda i, c: (i, c, 0)),
        out_shape=jax.ShapeDtypeStruct((b, t, RWKV_WIDTH), BF16),
        scratch_shapes=[pltpu.VMEM((n_seq, 8, pw), F32), pltpu.VMEM((n_seq, HEAD_DIM, RWKV_WIDTH), F32)],
        compiler_params=_params("arbitrary", "arbitrary"),
        name="rwkv7_chunked",
    )(p_rwkv, mu, w0, ww2_pad, a0, wa2_pad, wg2, k_k, k_a, r_k, gn_w, gn_b)


def kernel(x, c, w_ada, b_ada, g_ffn1, w_ffn1_in, w_ffn1_out, g_mix, w_mix_in, w_mix_out, att_q_gain, att_k_gain, att_sinks, rwkv_mu, rwkv_w0, rwkv_w_w2, rwkv_a0, rwkv_a_w2, rwkv_g_w2, rwkv_k_k, rwkv_k_a, rwkv_r_k, rwkv_gn_w, rwkv_gn_b, conv_w, g_ffn2, w_ffn2_in, w_ffn2_out):
    n_layers, d = g_ffn1.shape
    bsz = x.shape[0]
    row3 = lambda z: z.astype(F32).reshape(n_layers, 1, -1)
    bf = lambda z: z.astype(BF16)

    mod = _modulation(c, w_ada, b_ada).reshape(n_layers, bsz, N_MOD, d)

    half = RWKV_LORA // 2
    zeros = jnp.zeros((n_layers, half, RWKV_WIDTH), F32)
    ww2_pad = bf(jnp.concatenate([rwkv_w_w2, zeros], axis=1))
    wa2_pad = bf(jnp.concatenate([zeros, rwkv_a_w2], axis=1))
    wg2 = bf(rwkv_g_w2)
    g1, gm, g2 = row3(g_ffn1), row3(g_mix), row3(g_ffn2)
    mu, w0, a0 = row3(rwkv_mu), row3(rwkv_w0), row3(rwkv_a0)
    k_k, k_a, r_k = row3(rwkv_k_k), row3(rwkv_k_a), row3(rwkv_r_k)
    gn_w, gn_b = row3(rwkv_gn_w), row3(rwkv_gn_b)

    first = lambda l: [(w_ffn1_in, l), (w_ffn1_out, l), (w_mix_in, l)]
    second = lambda l: [(w_ffn2_in, l), (w_ffn2_out, l), (w_mix_out, l)]
    w1_in, w1_out, w_mix_in_b = bf(w_ffn1_in[0]), bf(w_ffn1_out[0]), bf(w_mix_in[0])

    h = x
    for l in range(n_layers):
        (h, p_att, p_rwkv, y_conv), (w2_in, w2_out, w_mix_out_b) = _ffn_mixin(
            h, mod[l], g1, gm, w1_in, w1_out, w_mix_in_b, conv_w, l, second(l))
        y_att = _attention(p_att, att_q_gain[l], att_k_gain[l], att_sinks[l])
        y_rwkv = _rwkv(p_rwkv, mu, w0, ww2_pad, a0, wa2_pad, wg2, k_k, k_a, r_k, gn_w, gn_b, l)
        h, nxt = _mixout_ffn(h, mod[l], g2, y_att, y_rwkv, y_conv, w_mix_out_b, w2_in, w2_out, l,
                             first(l + 1) if l + 1 < n_layers else [])
        if nxt:
            w1_in, w1_out, w_mix_in_b = nxt
    return h
```

```python
import functools
import math

import jax
import jax.numpy as jnp
from jax import lax
from jax.experimental import pallas as pl
from jax.experimental.pallas import tpu as pltpu

F32 = jnp.float32
BF16 = jnp.bfloat16

HEAD_DIM = 64
ATT_Q_HEADS = 8
ATT_KV_HEADS = 2
ATT_WIDTH = ATT_Q_HEADS * HEAD_DIM
ATT_KV_WIDTH = ATT_KV_HEADS * HEAD_DIM
ATT_PROJ_WIDTH = ATT_WIDTH + 2 * ATT_KV_WIDTH
ATT_BLOCK = 128
ATT_TILE = 2048
RWKV_HEADS = 4
RWKV_WIDTH = RWKV_HEADS * HEAD_DIM
RWKV_LORA = 128
RWKV_PROJ_WIDTH = 3 * RWKV_WIDTH + 2 * RWKV_LORA
RWKV_GN_EPS = 64e-5
RWKV_CHUNK = 64
RWKV_TILE = 256
RWKV_SEQS = 4
CONV_WIDTH = 256
CONV_K = 3
N_MOD = 9
EPS = 1e-6
NEG_BIG = -1e30
EXP_M05 = math.exp(-0.5)

MXU_WIDTH = 256
BF16_SUBLANES = 16
TOKEN_TILE = 512
VMEM_LIMIT = 56 * 1024 * 1024


def _dot(a, b):
    return jnp.dot(a, b, preferred_element_type=F32)


def _dot_nt(a, b):
    return lax.dot_general(a, b, (((1,), (1,)), ((), ())), preferred_element_type=F32)


def _block_ones(n, blk):
    r = lax.broadcasted_iota(jnp.int32, (n, n), 0) // blk
    c = lax.broadcasted_iota(jnp.int32, (n, n), 1) // blk
    return jnp.where(r == c, 1.0, 0.0).astype(BF16)


def _const_spec(shape):
    nd = len(shape)
    return pl.BlockSpec(shape, lambda *_: (0,) * nd, pipeline_mode=pl.Buffered(1))


def _layer_spec(shape, layer):
    nd = len(shape)
    return pl.BlockSpec((None,) + tuple(shape), lambda *_: (layer,) + (0,) * nd,
                        pipeline_mode=pl.Buffered(1))


def _params(*sem):
    return pltpu.CompilerParams(dimension_semantics=sem, vmem_limit_bytes=VMEM_LIMIT)


def _modulated_norm(x, gain, shift, scale):
    ms = jnp.mean(x * x, axis=-1, keepdims=True)
    return (x * lax.rsqrt(ms + EPS) * gain) * (1.0 + scale) + shift


def _mod_kernel(c_ref, w_ref, b_ref, o_ref):
    c = c_ref[...]
    act = (c * jax.nn.sigmoid(c)).astype(BF16)
    o_ref[...] = _dot(act, w_ref[...].astype(BF16)) + b_ref[...]


def _modulation(c, w_ada, b_ada):
    n_layers, d, n = w_ada.shape
    b = c.shape[0]
    tn = d
    return pl.pallas_call(
        _mod_kernel,
        grid=(n_layers, n // tn),
        in_specs=[
            pl.BlockSpec((b, d), lambda l, j: (0, 0)),
            pl.BlockSpec((None, d, tn), lambda l, j: (l, 0, j)),
            pl.BlockSpec((None, 1, tn), lambda l, j: (l, 0, j)),
        ],
        out_specs=pl.BlockSpec((None, b, tn), lambda l, j: (l, 0, j)),
        out_shape=jax.ShapeDtypeStruct((n_layers, b, n), F32),
        compiler_params=_params("arbitrary", "arbitrary"),
        name="adaln_mod",
    )(c, w_ada, b_ada.reshape(n_layers, 1, n))


def _swiglu_residual(x, shift, scale, gate, gain, wg_ref, wu_ref, wo_ref, n_chunks):
    hn = _modulated_norm(x, gain, shift, scale).astype(BF16)
    d_ff = wg_ref.shape[1]
    n_tiles = -(-d_ff // MXU_WIDTH)
    edges = [min(d_ff, MXU_WIDTH * ((n_tiles * j + n_chunks - 1) // n_chunks)) for j in range(n_chunks + 1)]
    acc = None
    for j in range(n_chunks):
        sl = slice(edges[j], edges[j + 1])
        g = _dot(hn, wg_ref[:, sl])
        up = _dot(hn, wu_ref[:, sl])
        act = (g * jax.nn.sigmoid(g) * up).astype(BF16)
        part = _dot(act, wo_ref[sl, :])
        acc = part if acc is None else acc + part
    return x + (0.5 * (1.0 + gate)) * acc


def _cast_jobs(sources, n_steps, grid_cols):
    in_specs, out_specs, out_shapes = [], [], []
    for arr, layer in sources:
        _, rows, cols = arr.shape
        rb = min(r for r in range(BF16_SUBLANES, rows + 1, BF16_SUBLANES) if rows % r == 0 and rows // r <= n_steps)
        nblk = rows // rb
        blk = lambda i, j, nblk=nblk: jnp.minimum(i * grid_cols + j, nblk - 1)
        in_specs.append(pl.BlockSpec((None, rb, cols), lambda i, j, blk=blk, layer=layer: (layer, blk(i, j), 0)))
        out_specs.append(pl.BlockSpec((rb, cols), lambda i, j, blk=blk: (blk(i, j), 0)))
        out_shapes.append(jax.ShapeDtypeStruct((rows, cols), BF16))
    return in_specs, out_specs, out_shapes


def _run_casts(src_refs, dst_refs):
    for src, dst in zip(src_refs, dst_refs):
        dst[...] = src[...].astype(dst.dtype)


N_FEAT_BF16 = 6
N_FEAT_F32 = 4


def _rwkv_features(p, prev_ref, tri2, ones_head, par, c_len):
    mu, w0, ww2, a0, wa2, wg2, k_k, k_a, r_k = par
    tm = p.shape[0]
    width = RWKV_WIDTH
    bf = lambda z: z.astype(BF16)
    head_sum = lambda z: _dot(bf(z), ones_head)
    trow = lax.broadcasted_iota(jnp.int32, (tm, 1), 0)
    p_prev = jnp.where(trow == 0, prev_ref[0:1, :], pltpu.roll(p, 1, 0))
    prev_ref[0:1, :] = p[tm - 1:tm, :]
    xs = p + mu * (p_prev - p)
    r = xs[:, 0:width]
    k = xs[:, width:2 * width]
    v = xs[:, 2 * width:3 * width]
    lora = xs[:, 3 * width:3 * width + RWKV_LORA]
    gate_in = xs[:, 3 * width + RWKV_LORA:]
    lora_t, lora_b, gate_b = bf(jnp.tanh(lora)), bf(lora), bf(jax.nn.sigmoid(gate_in))
    kk_raw = k * k_k
    kk_sq = bf(kk_raw * kk_raw)
    yield
    dw = _dot(lora_t, ww2)
    da = _dot(lora_b, wa2)
    g = _dot(gate_b, wg2)
    kk_norm2 = _dot(kk_sq, ones_head)
    lw = -EXP_M05 * jax.nn.sigmoid(w0 + dw)
    a = jax.nn.sigmoid(a0 + da)
    kk = kk_raw * lax.rsqrt(jnp.maximum(kk_norm2, 1e-24))
    kmod = k * (1.0 + (a - 1.0) * k_a)
    b = kk * a
    lw_hi = bf(lw)
    lw_lo = bf(lw - lw_hi.astype(F32))
    lw_split = jnp.concatenate([lw_hi, lw_lo], axis=0)
    rk_prod = bf(r * kmod * r_k)
    yield
    lc = _dot(tri2, lw_split)
    bonus = _dot(rk_prod, ones_head) * v
    ltot = jnp.concatenate([jnp.broadcast_to(lc[j + c_len - 1:j + c_len, :], (c_len, width))
                            for j in range(0, tm, c_len)], axis=0)
    w_inv = jnp.exp(-lc)
    w_end = jnp.exp(ltot - lc)
    feats_bf = [bf(kk * jnp.exp(lc - lw)), bf(b * w_inv), bf(kmod * w_inv), bf(v), bf(b * w_end), bf(kmod * w_end)]
    feats_f32 = [r * jnp.exp(lc), bonus, g, jnp.exp(ltot)]
    yield feats_bf, feats_f32


def _ffn_mixin_kernel(*refs, n_chunks, n_cast, c_len):
    (mod_ref, g1_ref, gm_ref, x_ref, wg_ref, wu_ref, wo_ref, wmix_ref, cw_ref, tri_ref, ones_ref) = refs[:11]
    rwkv_par = refs[11:20]
    cast_src = refs[20:20 + n_cast]
    h_ref, patt_ref, fbf_ref, ff32_ref, yconv_ref = refs[20 + n_cast:25 + n_cast]
    cast_dst = refs[25 + n_cast:25 + 2 * n_cast]
    carry_ref, prev_ref = refs[25 + 2 * n_cast:27 + 2 * n_cast]

    @pl.when(pl.program_id(1) == 0)
    def _():
        carry_ref[...] = jnp.zeros_like(carry_ref)
        prev_ref[...] = jnp.zeros_like(prev_ref)

    h = _swiglu_residual(x_ref[...], mod_ref[0:1, :], mod_ref[1:2, :], mod_ref[2:3, :], g1_ref[...],
                         wg_ref, wu_ref, wo_ref, n_chunks)
    h_ref[...] = h
    hn = _modulated_norm(h, gm_ref[...], mod_ref[3:4, :], mod_ref[4:5, :]).astype(BF16)
    a_end = ATT_PROJ_WIDTH
    r_end = a_end + RWKV_PROJ_WIDTH
    cwid = yconv_ref.shape[-1]
    p_rwkv = _dot(hn, wmix_ref[:, a_end:r_end])
    feats = _rwkv_features(p_rwkv, prev_ref, tri_ref[...], ones_ref[...], [z[...] for z in rwkv_par], c_len)
    next(feats)
    patt_ref[...] = _dot(hn, wmix_ref[:, 0:a_end]).astype(patt_ref.dtype)
    next(feats)
    b_gate = _dot(hn, wmix_ref[:, r_end:r_end + cwid])
    feats_bf, feats_f32 = next(feats)
    pc = _dot(hn, wmix_ref[:, r_end + cwid:])
    for i, z in enumerate(feats_bf):
        fbf_ref[:, i * RWKV_WIDTH:(i + 1) * RWKV_WIDTH] = z
    for i, z in enumerate(feats_f32):
        ff32_ref[:, i * RWKV_WIDTH:(i + 1) * RWKV_WIDTH] = z
    u = pc[:, 0:cwid] * pc[:, cwid:2 * cwid]
    tm = u.shape[0]
    row = lax.broadcasted_iota(jnp.int32, (tm, 1), 0)
    prev1 = carry_ref[1:2, :]
    prev2 = carry_ref[0:1, :]
    u1 = jnp.where(row == 0, prev1, pltpu.roll(u, 1, 0))
    u2 = jnp.where(row == 0, prev2, jnp.where(row == 1, prev1, pltpu.roll(u, 2, 0)))
    y = cw_ref[0:1, :] * u2 + cw_ref[1:2, :] * u1 + cw_ref[2:3, :] * u
    yconv_ref[...] = (b_gate * y).astype(yconv_ref.dtype)
    carry_ref[0:2, :] = u[tm - 2:tm, :]
    _run_casts(cast_src, cast_dst)


def _ffn_chunks(d_ff):
    return 2 if d_ff % 256 == 0 else 1


def _ffn_weight_specs(d, d_ff):
    gate = pl.BlockSpec((d, d_ff), lambda i, j: (0, 0), pipeline_mode=pl.Buffered(1))
    up = pl.BlockSpec((d, d_ff), lambda i, j: (0, 1), pipeline_mode=pl.Buffered(1))
    return [gate, up, _const_spec((d_ff, d))]


def _rwkv_constants(tm, c_len):
    t = jnp.arange(tm)
    tri = ((t[None, :] <= t[:, None]) & (t[None, :] // c_len == t[:, None] // c_len)).astype(BF16)
    lane = jnp.arange(RWKV_WIDTH)
    ones_head = (lane[:, None] // HEAD_DIM == lane[None, :] // HEAD_DIM).astype(BF16)
    return jnp.concatenate([tri, tri], axis=1), ones_head


def _ffn_mixin(h, mod9, g_ffn, g_mix, w_in, w_out, w_mix_in, conv_w, rwkv_params, layer, cast_sources):
    b, t, d = h.shape
    d_ff = w_out.shape[0]
    tm = min(TOKEN_TILE, t)
    nt = t // tm
    c_len = min(RWKV_CHUNK, tm)
    tok = lambda width: pl.BlockSpec((None, tm, width), lambda i, j: (i, j, 0))
    c_in, c_out, c_shapes = _cast_jobs(cast_sources, b * nt, nt)
    tri2, ones_head = _rwkv_constants(tm, c_len)
    outs = pl.pallas_call(
        functools.partial(_ffn_mixin_kernel, n_chunks=_ffn_chunks(d_ff), n_cast=len(cast_sources), c_len=c_len),
        grid=(b, nt),
        in_specs=[
            pl.BlockSpec((None, N_MOD, d), lambda i, j: (i, 0, 0)),
            _layer_spec((1, d), layer),
            _layer_spec((1, d), layer),
            tok(d),
            *_ffn_weight_specs(d, d_ff),
            _const_spec(w_mix_in.shape),
            _layer_spec((CONV_K, CONV_WIDTH), layer),
            _const_spec(tri2.shape),
            _const_spec(ones_head.shape),
            *[_layer_spec(z.shape[1:], layer) for z in rwkv_params],
            *c_in,
        ],
        out_specs=[tok(d), tok(ATT_PROJ_WIDTH), tok(N_FEAT_BF16 * RWKV_WIDTH), tok(N_FEAT_F32 * RWKV_WIDTH),
                   tok(CONV_WIDTH), *c_out],
        out_shape=[
            jax.ShapeDtypeStruct((b, t, d), F32),
            jax.ShapeDtypeStruct((b, t, ATT_PROJ_WIDTH), BF16),
            jax.ShapeDtypeStruct((b, t, N_FEAT_BF16 * RWKV_WIDTH), BF16),
            jax.ShapeDtypeStruct((b, t, N_FEAT_F32 * RWKV_WIDTH), F32),
            jax.ShapeDtypeStruct((b, t, CONV_WIDTH), BF16),
            *c_shapes,
        ],
        scratch_shapes=[pltpu.VMEM((8, CONV_WIDTH), F32), pltpu.VMEM((8, RWKV_PROJ_WIDTH), F32)],
        compiler_params=_params("arbitrary", "arbitrary"),
        name="ffn1_mix_in",
    )(mod9, g_ffn, g_mix, h, w_in, w_in, w_out, w_mix_in, conv_w, tri2, ones_head, *rwkv_params,
      *[a for a, _ in cast_sources])
    return outs[:5], outs[5:]


def _mixout_ffn_kernel(*refs, n_chunks, n_cast):
    (mod_ref, g2_ref, h_ref, ya_ref, yr_ref, yc_ref, wmix_ref, wg_ref, wu_ref, wo_ref) = refs[:10]
    cast_src = refs[10:10 + n_cast]
    o_ref = refs[10 + n_cast]
    cast_dst = refs[11 + n_cast:11 + 2 * n_cast]
    wa = ATT_WIDTH
    wr = wa + RWKV_WIDTH
    mixed = (_dot(ya_ref[...], wmix_ref[0:wa, :]) + _dot(yr_ref[...], wmix_ref[wa:wr, :])
             + _dot(yc_ref[...], wmix_ref[wr:, :]))
    h = h_ref[...] + (1.0 + mod_ref[5:6, :]) * mixed
    o_ref[...] = _swiglu_residual(h, mod_ref[6:7, :], mod_ref[7:8, :], mod_ref[8:9, :], g2_ref[...],
                                  wg_ref, wu_ref, wo_ref, n_chunks)
    _run_casts(cast_src, cast_dst)


def _mixout_ffn(h, mod9, g_ffn, y_att, y_rwkv, y_conv, w_mix_out, w_in, w_out, layer, cast_sources):
    b, t, d = h.shape
    d_ff = w_out.shape[0]
    tm = min(TOKEN_TILE, t)
    nt = t // tm
    tok = lambda width: pl.BlockSpec((None, tm, width), lambda i, j: (i, j, 0))
    c_in, c_out, c_shapes = _cast_jobs(cast_sources, b * nt, nt)
    outs = pl.pallas_call(
        functools.partial(_mixout_ffn_kernel, n_chunks=_ffn_chunks(d_ff), n_cast=len(cast_sources)),
        grid=(b, nt),
        in_specs=[
            pl.BlockSpec((None, N_MOD, d), lambda i, j: (i, 0, 0)),
            _layer_spec((1, d), layer),
            tok(d), tok(ATT_WIDTH), tok(RWKV_WIDTH), tok(CONV_WIDTH),
            _const_spec(w_mix_out.shape),
            *_ffn_weight_specs(d, d_ff),
            *c_in,
        ],
        out_specs=[tok(d), *c_out],
        out_shape=[jax.ShapeDtypeStruct((b, t, d), F32), *c_shapes],
        compiler_params=_params("arbitrary", "arbitrary"),
        name="mix_out_ffn2",
    )(mod9, g_ffn, h, y_att, y_rwkv, y_conv, w_mix_out, w_in, w_in, w_out, *[a for a, _ in cast_sources])
    return outs[0], outs[1:]


def _head_rms(x, ones_blk, gain):
    ss = _dot((x * x).astype(BF16), ones_blk)
    return x * lax.rsqrt(ss * (1.0 / HEAD_DIM) + EPS) * gain


def _attn_kernel(sink_ref, qg_ref, kg_ref, bias_ref, onesq_ref, onesk_ref, densel_ref,
                 q_ref, kvc_ref, kvp_ref, o_ref):
    n = pl.program_id(1)
    blk = ATT_BLOCK
    n_sub = q_ref.shape[0] // blk
    rows = 2 * blk
    low = lax.broadcasted_iota(jnp.int32, (1, 2 * HEAD_DIM), 1) < HEAD_DIM
    top = lax.broadcasted_iota(jnp.int32, (rows, 1), 0) < blk

    q = q_ref[...].astype(F32)
    qn = _head_rms(q, onesq_ref[...], qg_ref[...]).astype(BF16)
    kv = jnp.concatenate([kvp_ref[...], kvc_ref[...]], axis=0).astype(F32)
    k = kv[:, 0:ATT_KV_WIDTH]
    v = kv[:, ATT_KV_WIDTH:2 * ATT_KV_WIDTH]
    kn = _head_rms(k, onesk_ref[...], kg_ref[...])
    kr = pltpu.roll(kn, HEAD_DIM, 1)
    vr = pltpu.roll(v, HEAD_DIM, 1)
    zero = jnp.zeros_like(kn)
    bf = lambda z: z.astype(BF16)
    k_low = [bf(jnp.where(low, kn, zero)), bf(jnp.where(low, kr, zero))]
    k_high = [bf(jnp.where(low, zero, kr)), bf(jnp.where(low, zero, kn))]
    v_low = [bf(jnp.where(low, v, zero)), bf(jnp.where(low, vr, zero))]
    v_high = [bf(jnp.where(low, zero, vr)), bf(jnp.where(low, zero, v))]
    den_sel = densel_ref[...]

    for j in range(n_sub):
        bias = bias_ref[jnp.minimum(n, 1)] if j == 0 else bias_ref[1]
        keys = slice(j * blk, (j + 2) * blk)
        for g in range(ATT_KV_HEADS):
            qj = qn[j * blk:(j + 1) * blk]
            qg = jnp.concatenate([qj[:, 256 * g:256 * g + 128], qj[:, 256 * g + 128:256 * g + 256]], axis=0)
            kcat = jnp.concatenate([k_low[g][keys], k_high[g][keys]], axis=0)
            vcat = jnp.concatenate([v_low[g][keys], v_high[g][keys]], axis=0)
            s = _dot_nt(qg, kcat) + bias
            h0 = 4 * g
            sink_e = jnp.where(top, sink_ref[h0], sink_ref[h0 + 2])
            sink_o = jnp.where(top, sink_ref[h0 + 1], sink_ref[h0 + 3])
            s_e = s[:, 0:2 * blk]
            s_o = s[:, 2 * blk:4 * blk]
            m_e = jnp.maximum(jnp.max(s_e, axis=-1, keepdims=True), sink_e)
            m_o = jnp.maximum(jnp.max(s_o, axis=-1, keepdims=True), sink_o)
            p = bf(jnp.concatenate([jnp.exp2(s_e - m_e), jnp.exp2(s_o - m_o)], axis=1))
            nd = _dot(p, jnp.concatenate([vcat, den_sel], axis=1))
            den = nd[:, 2 * HEAD_DIM:] + jnp.where(low, jnp.exp2(sink_e - m_e), jnp.exp2(sink_o - m_o))
            out = (nd[:, 0:2 * HEAD_DIM] / den).astype(o_ref.dtype)
            o_ref[j * blk:(j + 1) * blk, 256 * g:256 * g + 128] = out[0:blk]
            o_ref[j * blk:(j + 1) * blk, 256 * g + 128:256 * g + 256] = out[blk:rows]


def _attention_constants():
    blk = ATT_BLOCK
    ri = jnp.arange(2 * blk)[:, None] % blk
    cj = jnp.arange(4 * blk)[None, :] % (2 * blk)
    band = (cj > ri) & (cj <= ri + blk)
    bias = jnp.stack([jnp.where(band & (cj >= blk), 0.0, NEG_BIG), jnp.where(band, 0.0, NEG_BIG)]).astype(F32)
    ones = lambda n: (jnp.arange(n)[:, None] // HEAD_DIM == jnp.arange(n)[None, :] // HEAD_DIM).astype(BF16)
    den_sel = (jnp.arange(4 * blk)[:, None] // (2 * blk) == jnp.arange(2 * HEAD_DIM)[None, :] // HEAD_DIM)
    return bias, ones(ATT_WIDTH), ones(ATT_KV_WIDTH), den_sel.astype(BF16)


def _attention(p_att, q_gain, k_gain, sinks):
    b, t, _ = p_att.shape
    blk = ATT_BLOCK
    tq = min(ATT_TILE, t)
    n_sub = tq // blk
    tile_gain = lambda g, reps: jnp.tile(g.astype(F32), reps).reshape(1, reps * HEAD_DIM)
    log2e = math.log2(math.e)
    bias, ones_q, ones_k, den_sel = _attention_constants()
    kv_col = ATT_WIDTH // (2 * ATT_KV_WIDTH)
    return pl.pallas_call(
        _attn_kernel,
        grid=(b, t // tq),
        in_specs=[
            pl.BlockSpec(memory_space=pltpu.SMEM),
            _const_spec((1, ATT_WIDTH)),
            _const_spec((1, ATT_KV_WIDTH)),
            _const_spec(bias.shape),
            _const_spec(ones_q.shape),
            _const_spec(ones_k.shape),
            _const_spec(den_sel.shape),
            pl.BlockSpec((None, tq, ATT_WIDTH), lambda i, n: (i, n, 0)),
            pl.BlockSpec((None, tq, 2 * ATT_KV_WIDTH), lambda i, n: (i, n, kv_col)),
            pl.BlockSpec((None, blk, 2 * ATT_KV_WIDTH), lambda i, n: (i, jnp.maximum(n * n_sub - 1, 0), kv_col)),
        ],
        out_specs=pl.BlockSpec((None, tq, ATT_WIDTH), lambda i, n: (i, n, 0)),
        out_shape=jax.ShapeDtypeStruct((b, t, ATT_WIDTH), BF16),
        compiler_params=_params("arbitrary", "arbitrary"),
        name="swa_sink_attention",
    )(sinks.astype(F32) * log2e, tile_gain(q_gain, ATT_Q_HEADS) * (HEAD_DIM ** -0.5 * log2e),
      tile_gain(k_gain, ATT_KV_HEADS), bias, ones_q, ones_k, den_sel, p_att, p_att, p_att)


def _rwkv_kernel(fbf_ref, ff32_ref, gnw_ref, gnb_ref, o_ref, state_ref, *, c_len):
    n_seq, tt, _ = fbf_ref.shape
    width = RWKV_WIDTH
    n_heads = RWKV_HEADS
    n_ch = tt // c_len
    bf = lambda z: z.astype(BF16)
    each = lambda f, *cols: [f(*args) for args in zip(*cols)]

    @pl.when(pl.program_id(1) == 0)
    def _():
        state_ref[...] = jnp.zeros_like(state_ref)

    ones_head = _block_ones(width, HEAD_DIM)
    head_sum = lambda z: _dot(bf(z), ones_head)
    lane_head = lax.broadcasted_iota(jnp.int32, (1, width), 1) // HEAD_DIM

    def stack(z):
        zero = jnp.zeros_like(z)
        return jnp.concatenate([jnp.where(lane_head == h, z, zero) for h in range(n_heads)], axis=0)

    def head_transpose(z_bd):
        zt = jnp.transpose(z_bd.astype(F32))
        out = zt[0:HEAD_DIM]
        for h in range(1, n_heads):
            out = out + zt[h * HEAD_DIM:(h + 1) * HEAD_DIM]
        return bf(out)

    lanes = lambda i: slice(i * width, (i + 1) * width)
    chunk = lambda ref, i: [ref[s, j * c_len:(j + 1) * c_len, lanes(i)] for s in range(n_seq) for j in range(n_ch)]
    at_b, bt_b, kt_b, v_b, bh_b, kh_b = (chunk(fbf_ref, i) for i in range(N_FEAT_BF16))
    rt_c = chunk(ff32_ref, 0)
    w_tot = [ff32_ref[s, j * c_len:j * c_len + 1, lanes(3)] for s in range(n_seq) for j in range(n_ch)]
    rt_b = each(bf, rt_c)
    bt_bd, kt_bd, v_bd, at_bd = each(stack, bt_b), each(stack, kt_b), each(stack, v_b), each(stack, at_b)
    bh_bd, kh_bd = each(stack, bh_b), each(stack, kh_b)

    mi = lax.broadcasted_iota(jnp.int32, (c_len, width), 0)
    mj = lax.broadcasted_iota(jnp.int32, (c_len, width), 1) & (c_len - 1)
    strict = mj < mi
    incl = mj <= mi
    eye = mj == mi

    rows2 = lambda x, y: jnp.concatenate([x, y], axis=0)
    top, bot = (lambda z: z[0:c_len]), (lambda z: z[c_len:2 * c_len])
    ar_b = each(rows2, at_b, rt_b)
    g_b = each(_dot_nt, ar_b, bt_bd)
    g_k = each(_dot_nt, ar_b, kt_bd)
    a_ab = each(lambda z: jnp.where(strict, top(z), 0.0), g_b)
    a_rb = each(lambda z: bf(jnp.where(incl, bot(z), 0.0)), g_b)
    a_ak = each(lambda z: bf(jnp.where(strict, top(z), 0.0)), g_k)
    a_rk = each(lambda z: bf(jnp.where(incl, bot(z), 0.0)), g_k)
    bh_t = each(head_transpose, bh_bd)
    kh_t = each(head_transpose, kh_bd)
    on_v = each(lambda x, y, z, w: _dot(jnp.concatenate([x, y, z], axis=0), w), a_ak, a_rk, kh_t, v_bd)
    y1_bd = each(lambda z: stack(bf(top(z))), on_v)

    t_inv = each(lambda z: jnp.where(eye, 1.0, 0.0) - z, a_ab)
    pw = each(bf, a_ab)
    pw = each(lambda x: bf(_dot(x, stack(x))), pw)
    n_sq = int(math.log2(c_len)) - 1
    for lvl in range(n_sq):
        pw_bd = each(stack, pw)
        if lvl + 1 < n_sq:
            res = each(lambda t, x, y: _dot(rows2(bf(t), x), y), t_inv, pw, pw_bd)
            t_inv = each(lambda t, z: t + top(z), t_inv, res)
            pw = each(lambda z: bf(bot(z)), res)
        else:
            t_inv = each(lambda t, y: t + _dot(bf(t), y), t_inv, pw_bd)
    t_b = each(bf, t_inv)

    abt_t = each(lambda x, y, t: bf(_dot(rows2(x, y), stack(t))), a_rb, bh_t, t_b)
    on_at2 = each(_dot, abt_t, at_bd)
    on_u2 = each(_dot, abt_t, y1_bd)
    r2 = each(lambda z, m: bf(z - top(m)), rt_c, on_at2)
    p_t = each(lambda wt, m: bf(jnp.where(eye, wt, 0.0) - bot(m)), w_tot, on_at2)
    o2 = each(lambda z, m: z[c_len:2 * c_len] - top(m), on_v, on_u2)
    q_t = each(lambda z, m: z[2 * c_len:3 * c_len] - bot(m), on_v, on_u2)
    rp = each(rows2, r2, p_t)

    states = [state_ref[s] for s in range(n_seq)]
    ys = [[] for _ in range(n_seq)]
    for j in range(n_ch):
        for s in range(n_seq):
            i = s * n_ch + j
            res = _dot(rp[i], stack(bf(states[s])))
            ys[s].append(top(res) + o2[i])
            states[s] = bot(res) + q_t[i]

    for s in range(n_seq):
        state_ref[s] = states[s]
        y = jnp.concatenate(ys[s], axis=0) if n_ch > 1 else ys[s][0]
        mean = head_sum(y) * (1.0 / HEAD_DIM)
        dev = y - mean
        var = head_sum(dev * dev) * (1.0 / HEAD_DIM)
        yn = dev * lax.rsqrt(var + RWKV_GN_EPS) * gnw_ref[...] + gnb_ref[...]
        o_ref[s] = ((yn + ff32_ref[s, :, lanes(1)]) * ff32_ref[s, :, lanes(2)]).astype(o_ref.dtype)


def _rwkv(feat_bf, feat_f32, gn_w, gn_b, layer):
    b, t, _ = feat_bf.shape
    tt = min(RWKV_TILE, t)
    c_len = min(RWKV_CHUNK, tt)
    n_seq = RWKV_SEQS if b % RWKV_SEQS == 0 else 1
    return pl.pallas_call(
        functools.partial(_rwkv_kernel, c_len=c_len),
        grid=(b // n_seq, t // tt),
        in_specs=[
            pl.BlockSpec((n_seq, tt, N_FEAT_BF16 * RWKV_WIDTH), lambda i, c: (i, c, 0)),
            pl.BlockSpec((n_seq, tt, N_FEAT_F32 * RWKV_WIDTH), lambda i, c: (i, c, 0)),
            _layer_spec((1, RWKV_WIDTH), layer),
            _layer_spec((1, RWKV_WIDTH), layer),
        ],
        out_specs=pl.BlockSpec((n_seq, tt, RWKV_WIDTH), lambda i, c: (i, c, 0)),
        out_shape=jax.ShapeDtypeStruct((b, t, RWKV_WIDTH), BF16),
        scratch_shapes=[pltpu.VMEM((n_seq, HEAD_DIM, RWKV_WIDTH), F32)],
        compiler_params=_params("arbitrary", "arbitrary"),
        name="rwkv7_chunked",
    )(feat_bf, feat_f32, gn_w, gn_b)


def kernel(x, c, w_ada, b_ada, g_ffn1, w_ffn1_in, w_ffn1_out, g_mix, w_mix_in, w_mix_out, att_q_gain, att_k_gain, att_sinks, rwkv_mu, rwkv_w0, rwkv_w_w2, rwkv_a0, rwkv_a_w2, rwkv_g_w2, rwkv_k_k, rwkv_k_a, rwkv_r_k, rwkv_gn_w, rwkv_gn_b, conv_w, g_ffn2, w_ffn2_in, w_ffn2_out):
    n_layers, d = g_ffn1.shape
    bsz = x.shape[0]
    row3 = lambda z: z.astype(F32).reshape(n_layers, 1, -1)
    bf = lambda z: z.astype(BF16)

    mod = _modulation(c, w_ada, b_ada).reshape(n_layers, bsz, N_MOD, d)

    half = RWKV_LORA // 2
    zeros = jnp.zeros((n_layers, half, RWKV_WIDTH), F32)
    ww2_pad = bf(jnp.concatenate([rwkv_w_w2, zeros], axis=1))
    wa2_pad = bf(jnp.concatenate([zeros, rwkv_a_w2], axis=1))
    wg2 = bf(rwkv_g_w2)
    g1, gm, g2 = row3(g_ffn1), row3(g_mix), row3(g_ffn2)
    mu, w0, a0 = row3(rwkv_mu), row3(rwkv_w0), row3(rwkv_a0)
    k_k, k_a, r_k = row3(rwkv_k_k), row3(rwkv_k_a), row3(rwkv_r_k)
    gn_w, gn_b = row3(rwkv_gn_w), row3(rwkv_gn_b)
    rwkv_params = [mu, w0, ww2_pad, a0, wa2_pad, wg2, k_k, k_a, r_k]

    first = lambda l: [(w_ffn1_in, l), (w_ffn1_out, l), (w_mix_in, l)]
    second = lambda l: [(w_ffn2_in, l), (w_ffn2_out, l), (w_mix_out, l)]
    w1_in, w1_out, w_mix_in_b = bf(w_ffn1_in[0]), bf(w_ffn1_out[0]), bf(w_mix_in[0])

    h = x
    for l in range(n_layers):
        (h, p_att, feat_bf, feat_f32, y_conv), (w2_in, w2_out, w_mix_out_b) = _ffn_mixin(
            h, mod[l], g1, gm, w1_in, w1_out, w_mix_in_b, conv_w, rwkv_params, l, second(l))
        y_att = _attention(p_att, att_q_gain[l], att_k_gain[l], att_sinks[l])
        y_rwkv = _rwkv(feat_bf, feat_f32, gn_w, gn_b, l)
        h, nxt = _mixout_ffn(h, mod[l], g2, y_att, y_rwkv, y_conv, w_mix_out_b, w2_in, w2_out, l,
                             first(l + 1) if l + 1 < n_layers else [])
        if nxt:
            w1_in, w1_out, w_mix_in_b = nxt
    return h
```

```python
import functools
import math

import jax
import jax.numpy as jnp
from jax import lax
from jax.experimental import pallas as pl
from jax.experimental.pallas import tpu as pltpu

F32 = jnp.float32
BF16 = jnp.bfloat16

HEAD_DIM = 64
ATT_Q_HEADS = 8
ATT_KV_HEADS = 2
ATT_WIDTH = ATT_Q_HEADS * HEAD_DIM
ATT_KV_WIDTH = ATT_KV_HEADS * HEAD_DIM
ATT_PROJ_WIDTH = ATT_WIDTH + 2 * ATT_KV_WIDTH
ATT_BLOCK = 128
ATT_TILE = 2048
RWKV_HEADS = 4
RWKV_WIDTH = RWKV_HEADS * HEAD_DIM
RWKV_LORA = 128
RWKV_PROJ_WIDTH = 3 * RWKV_WIDTH + 2 * RWKV_LORA
RWKV_GN_EPS = 64e-5
RWKV_CHUNK = 64
RWKV_INV_BLOCK = 8
RWKV_TILE = 256
RWKV_SEQS = 4
CONV_WIDTH = 256
CONV_K = 3
N_MOD = 9
EPS = 1e-6
NEG_BIG = -1e30
EXP_M05 = math.exp(-0.5)

MXU_WIDTH = 256
BF16_SUBLANES = 16
TOKEN_TILE = 512
VMEM_LIMIT = 56 * 1024 * 1024


def _dot(a, b):
    return jnp.dot(a, b, preferred_element_type=F32)


def _dot_nt(a, b):
    return lax.dot_general(a, b, (((1,), (1,)), ((), ())), preferred_element_type=F32)


def _block_ones(n, blk):
    r = lax.broadcasted_iota(jnp.int32, (n, n), 0) // blk
    c = lax.broadcasted_iota(jnp.int32, (n, n), 1) // blk
    return jnp.where(r == c, 1.0, 0.0).astype(BF16)


def _const_spec(shape):
    nd = len(shape)
    return pl.BlockSpec(shape, lambda *_: (0,) * nd, pipeline_mode=pl.Buffered(1))


def _layer_spec(shape, layer):
    nd = len(shape)
    return pl.BlockSpec((None,) + tuple(shape), lambda *_: (layer,) + (0,) * nd,
                        pipeline_mode=pl.Buffered(1))


def _params(*sem):
    return pltpu.CompilerParams(dimension_semantics=sem, vmem_limit_bytes=VMEM_LIMIT)


def _modulated_norm(x, gain, shift, scale):
    ms = jnp.mean(x * x, axis=-1, keepdims=True)
    return (x * lax.rsqrt(ms + EPS) * gain) * (1.0 + scale) + shift


def _mod_kernel(c_ref, w_ref, b_ref, o_ref):
    c = c_ref[...]
    act = (c * jax.nn.sigmoid(c)).astype(BF16)
    o_ref[...] = _dot(act, w_ref[...].astype(BF16)) + b_ref[...]


def _modulation(c, w_ada, b_ada):
    n_layers, d, n = w_ada.shape
    b = c.shape[0]
    tn = d
    return pl.pallas_call(
        _mod_kernel,
        grid=(n_layers, n // tn),
        in_specs=[
            pl.BlockSpec((b, d), lambda l, j: (0, 0)),
            pl.BlockSpec((None, d, tn), lambda l, j: (l, 0, j)),
            pl.BlockSpec((None, 1, tn), lambda l, j: (l, 0, j)),
        ],
        out_specs=pl.BlockSpec((None, b, tn), lambda l, j: (l, 0, j)),
        out_shape=jax.ShapeDtypeStruct((n_layers, b, n), F32),
        compiler_params=_params("arbitrary", "arbitrary"),
        name="adaln_mod",
    )(c, w_ada, b_ada.reshape(n_layers, 1, n))


def _swiglu_residual(x, shift, scale, gate, gain, wg_ref, wu_ref, wo_ref, n_chunks):
    hn = _modulated_norm(x, gain, shift, scale).astype(BF16)
    d_ff = wg_ref.shape[1]
    n_tiles = -(-d_ff // MXU_WIDTH)
    edges = [min(d_ff, MXU_WIDTH * ((n_tiles * j + n_chunks - 1) // n_chunks)) for j in range(n_chunks + 1)]
    acc = None
    for j in range(n_chunks):
        sl = slice(edges[j], edges[j + 1])
        g = _dot(hn, wg_ref[:, sl])
        up = _dot(hn, wu_ref[:, sl])
        act = (g * jax.nn.sigmoid(g) * up).astype(BF16)
        part = _dot(act, wo_ref[sl, :])
        acc = part if acc is None else acc + part
    return x + (0.5 * (1.0 + gate)) * acc


def _cast_jobs(sources, n_steps, grid_cols):
    in_specs, out_specs, out_shapes = [], [], []
    for arr, layer in sources:
        _, rows, cols = arr.shape
        rb = min(r for r in range(BF16_SUBLANES, rows + 1, BF16_SUBLANES) if rows % r == 0 and rows // r <= n_steps)
        nblk = rows // rb
        blk = lambda i, j, nblk=nblk: jnp.minimum(i * grid_cols + j, nblk - 1)
        in_specs.append(pl.BlockSpec((None, rb, cols), lambda i, j, blk=blk, layer=layer: (layer, blk(i, j), 0)))
        out_specs.append(pl.BlockSpec((rb, cols), lambda i, j, blk=blk: (blk(i, j), 0)))
        out_shapes.append(jax.ShapeDtypeStruct((rows, cols), BF16))
    return in_specs, out_specs, out_shapes


def _run_casts(src_refs, dst_refs):
    for src, dst in zip(src_refs, dst_refs):
        dst[...] = src[...].astype(dst.dtype)


def _ffn_mixin_kernel(*refs, n_chunks, n_cast):
    (mod_ref, g1_ref, gm_ref, x_ref, wg_ref, wu_ref, wo_ref, wmix_ref, cw_ref) = refs[:9]
    cast_src = refs[9:9 + n_cast]
    h_ref, patt_ref, prwkv_ref, yconv_ref = refs[9 + n_cast:13 + n_cast]
    cast_dst = refs[13 + n_cast:13 + 2 * n_cast]
    carry_ref = refs[13 + 2 * n_cast]

    @pl.when(pl.program_id(1) == 0)
    def _():
        carry_ref[...] = jnp.zeros_like(carry_ref)

    h = _swiglu_residual(x_ref[...], mod_ref[0:1, :], mod_ref[1:2, :], mod_ref[2:3, :], g1_ref[...],
                         wg_ref, wu_ref, wo_ref, n_chunks)
    h_ref[...] = h
    hn = _modulated_norm(h, gm_ref[...], mod_ref[3:4, :], mod_ref[4:5, :]).astype(BF16)
    a_end = ATT_PROJ_WIDTH
    r_end = a_end + RWKV_PROJ_WIDTH
    patt_ref[...] = _dot(hn, wmix_ref[:, 0:a_end]).astype(patt_ref.dtype)
    prwkv_ref[...] = _dot(hn, wmix_ref[:, a_end:r_end])
    pc = _dot(hn, wmix_ref[:, r_end:])
    cwid = yconv_ref.shape[-1]
    b_gate = pc[:, 0:cwid]
    u = pc[:, cwid:2 * cwid] * pc[:, 2 * cwid:3 * cwid]
    tm = u.shape[0]
    row = lax.broadcasted_iota(jnp.int32, (tm, 1), 0)
    prev1 = carry_ref[1:2, :]
    prev2 = carry_ref[0:1, :]
    u1 = jnp.where(row == 0, prev1, pltpu.roll(u, 1, 0))
    u2 = jnp.where(row == 0, prev2, jnp.where(row == 1, prev1, pltpu.roll(u, 2, 0)))
    y = cw_ref[0:1, :] * u2 + cw_ref[1:2, :] * u1 + cw_ref[2:3, :] * u
    yconv_ref[...] = (b_gate * y).astype(yconv_ref.dtype)
    carry_ref[0:2, :] = u[tm - 2:tm, :]
    _run_casts(cast_src, cast_dst)


def _ffn_chunks(d_ff):
    return 2 if d_ff % 256 == 0 else 1


def _ffn_weight_specs(d, d_ff):
    gate = pl.BlockSpec((d, d_ff), lambda i, j: (0, 0), pipeline_mode=pl.Buffered(1))
    up = pl.BlockSpec((d, d_ff), lambda i, j: (0, 1), pipeline_mode=pl.Buffered(1))
    return [gate, up, _const_spec((d_ff, d))]


def _ffn_mixin(h, mod9, g_ffn, g_mix, w_in, w_out, w_mix_in, conv_w, layer, cast_sources):
    b, t, d = h.shape
    d_ff = w_out.shape[0]
    tm = min(TOKEN_TILE, t)
    nt = t // tm
    tok = lambda width: pl.BlockSpec((None, tm, width), lambda i, j: (i, j, 0))
    c_in, c_out, c_shapes = _cast_jobs(cast_sources, b * nt, nt)
    outs = pl.pallas_call(
        functools.partial(_ffn_mixin_kernel, n_chunks=_ffn_chunks(d_ff), n_cast=len(cast_sources)),
        grid=(b, nt),
        in_specs=[
            pl.BlockSpec((None, N_MOD, d), lambda i, j: (i, 0, 0)),
            _layer_spec((1, d), layer),
            _layer_spec((1, d), layer),
            tok(d),
            *_ffn_weight_specs(d, d_ff),
            _const_spec(w_mix_in.shape),
            _layer_spec((CONV_K, CONV_WIDTH), layer),
            *c_in,
        ],
        out_specs=[tok(d), tok(ATT_PROJ_WIDTH), tok(RWKV_PROJ_WIDTH), tok(CONV_WIDTH), *c_out],
        out_shape=[
            jax.ShapeDtypeStruct((b, t, d), F32),
            jax.ShapeDtypeStruct((b, t, ATT_PROJ_WIDTH), BF16),
            jax.ShapeDtypeStruct((b, t, RWKV_PROJ_WIDTH), F32),
            jax.ShapeDtypeStruct((b, t, CONV_WIDTH), BF16),
            *c_shapes,
        ],
        scratch_shapes=[pltpu.VMEM((8, CONV_WIDTH), F32)],
        compiler_params=_params("arbitrary", "arbitrary"),
        name="ffn1_mix_in",
    )(mod9, g_ffn, g_mix, h, w_in, w_in, w_out, w_mix_in, conv_w, *[a for a, _ in cast_sources])
    return outs[:4], outs[4:]


def _mixout_ffn_kernel(*refs, n_chunks, n_cast):
    (mod_ref, g2_ref, h_ref, ya_ref, yr_ref, yc_ref, wmix_ref, wg_ref, wu_ref, wo_ref) = refs[:10]
    cast_src = refs[10:10 + n_cast]
    o_ref = refs[10 + n_cast]
    cast_dst = refs[11 + n_cast:11 + 2 * n_cast]
    wa = ATT_WIDTH
    wr = wa + RWKV_WIDTH
    mixed = (_dot(ya_ref[...], wmix_ref[0:wa, :]) + _dot(yr_ref[...], wmix_ref[wa:wr, :])
             + _dot(yc_ref[...], wmix_ref[wr:, :]))
    h = h_ref[...] + (1.0 + mod_ref[5:6, :]) * mixed
    o_ref[...] = _swiglu_residual(h, mod_ref[6:7, :], mod_ref[7:8, :], mod_ref[8:9, :], g2_ref[...],
                                  wg_ref, wu_ref, wo_ref, n_chunks)
    _run_casts(cast_src, cast_dst)


def _mixout_ffn(h, mod9, g_ffn, y_att, y_rwkv, y_conv, w_mix_out, w_in, w_out, layer, cast_sources):
    b, t, d = h.shape
    d_ff = w_out.shape[0]
    tm = min(TOKEN_TILE, t)
    nt = t // tm
    tok = lambda width: pl.BlockSpec((None, tm, width), lambda i, j: (i, j, 0))
    c_in, c_out, c_shapes = _cast_jobs(cast_sources, b * nt, nt)
    outs = pl.pallas_call(
        functools.partial(_mixout_ffn_kernel, n_chunks=_ffn_chunks(d_ff), n_cast=len(cast_sources)),
        grid=(b, nt),
        in_specs=[
            pl.BlockSpec((None, N_MOD, d), lambda i, j: (i, 0, 0)),
            _layer_spec((1, d), layer),
            tok(d), tok(ATT_WIDTH), tok(RWKV_WIDTH), tok(CONV_WIDTH),
            _const_spec(w_mix_out.shape),
            *_ffn_weight_specs(d, d_ff),
            *c_in,
        ],
        out_specs=[tok(d), *c_out],
        out_shape=[jax.ShapeDtypeStruct((b, t, d), F32), *c_shapes],
        compiler_params=_params("arbitrary", "arbitrary"),
        name="mix_out_ffn2",
    )(mod9, g_ffn, h, y_att, y_rwkv, y_conv, w_mix_out, w_in, w_in, w_out, *[a for a, _ in cast_sources])
    return outs[0], outs[1:]


def _head_rms(x, ones_blk, gain):
    ss = _dot((x * x).astype(BF16), ones_blk)
    return x * lax.rsqrt(ss * (1.0 / HEAD_DIM) + EPS) * gain


def _attn_kernel(sink_ref, qg_ref, kg_ref, bias_ref, onesq_ref, onesk_ref, densel_ref,
                 q_ref, kvc_ref, kvp_ref, o_ref):
    n = pl.program_id(1)
    blk = ATT_BLOCK
    n_sub = q_ref.shape[0] // blk
    rows = 2 * blk
    low = lax.broadcasted_iota(jnp.int32, (1, 2 * HEAD_DIM), 1) < HEAD_DIM
    top = lax.broadcasted_iota(jnp.int32, (rows, 1), 0) < blk

    q = q_ref[...].astype(F32)
    qn = _head_rms(q, onesq_ref[...], qg_ref[...]).astype(BF16)
    kv = jnp.concatenate([kvp_ref[...], kvc_ref[...]], axis=0).astype(F32)
    k = kv[:, 0:ATT_KV_WIDTH]
    v = kv[:, ATT_KV_WIDTH:2 * ATT_KV_WIDTH]
    kn = _head_rms(k, onesk_ref[...], kg_ref[...])
    kr = pltpu.roll(kn, HEAD_DIM, 1)
    vr = pltpu.roll(v, HEAD_DIM, 1)
    zero = jnp.zeros_like(kn)
    bf = lambda z: z.astype(BF16)
    k_low = [bf(jnp.where(low, kn, zero)), bf(jnp.where(low, kr, zero))]
    k_high = [bf(jnp.where(low, zero, kr)), bf(jnp.where(low, zero, kn))]
    v_low = [bf(jnp.where(low, v, zero)), bf(jnp.where(low, vr, zero))]
    v_high = [bf(jnp.where(low, zero, vr)), bf(jnp.where(low, zero, v))]
    den_sel = densel_ref[...]

    for j in range(n_sub):
        bias = bias_ref[jnp.minimum(n, 1)] if j == 0 else bias_ref[1]
        keys = slice(j * blk, (j + 2) * blk)
        for g in range(ATT_KV_HEADS):
            qj = qn[j * blk:(j + 1) * blk]
            qg = jnp.concatenate([qj[:, 256 * g:256 * g + 128], qj[:, 256 * g + 128:256 * g + 256]], axis=0)
            kcat = jnp.concatenate([k_low[g][keys], k_high[g][keys]], axis=0)
            vcat = jnp.concatenate([v_low[g][keys], v_high[g][keys]], axis=0)
            s = _dot_nt(qg, kcat) + bias
            h0 = 4 * g
            sink_e = jnp.where(top, sink_ref[h0], sink_ref[h0 + 2])
            sink_o = jnp.where(top, sink_ref[h0 + 1], sink_ref[h0 + 3])
            s_e = s[:, 0:2 * blk]
            s_o = s[:, 2 * blk:4 * blk]
            m_e = jnp.maximum(jnp.max(s_e, axis=-1, keepdims=True), sink_e)
            m_o = jnp.maximum(jnp.max(s_o, axis=-1, keepdims=True), sink_o)
            p = bf(jnp.concatenate([jnp.exp2(s_e - m_e), jnp.exp2(s_o - m_o)], axis=1))
            nd = _dot(p, jnp.concatenate([vcat, den_sel], axis=1))
            den = nd[:, 2 * HEAD_DIM:] + jnp.where(low, jnp.exp2(sink_e - m_e), jnp.exp2(sink_o - m_o))
            out = (nd[:, 0:2 * HEAD_DIM] / den).astype(o_ref.dtype)
            o_ref[j * blk:(j + 1) * blk, 256 * g:256 * g + 128] = out[0:blk]
            o_ref[j * blk:(j + 1) * blk, 256 * g + 128:256 * g + 256] = out[blk:rows]


def _attention_constants():
    blk = ATT_BLOCK
    ri = jnp.arange(2 * blk)[:, None] % blk
    cj = jnp.arange(4 * blk)[None, :] % (2 * blk)
    band = (cj > ri) & (cj <= ri + blk)
    bias = jnp.stack([jnp.where(band & (cj >= blk), 0.0, NEG_BIG), jnp.where(band, 0.0, NEG_BIG)]).astype(F32)
    ones = lambda n: (jnp.arange(n)[:, None] // HEAD_DIM == jnp.arange(n)[None, :] // HEAD_DIM).astype(BF16)
    den_sel = (jnp.arange(4 * blk)[:, None] // (2 * blk) == jnp.arange(2 * HEAD_DIM)[None, :] // HEAD_DIM)
    return bias, ones(ATT_WIDTH), ones(ATT_KV_WIDTH), den_sel.astype(BF16)


def _attention(p_att, q_gain, k_gain, sinks):
    b, t, _ = p_att.shape
    blk = ATT_BLOCK
    tq = min(ATT_TILE, t)
    n_sub = tq // blk
    tile_gain = lambda g, reps: jnp.tile(g.astype(F32), reps).reshape(1, reps * HEAD_DIM)
    log2e = math.log2(math.e)
    bias, ones_q, ones_k, den_sel = _attention_constants()
    kv_col = ATT_WIDTH // (2 * ATT_KV_WIDTH)
    return pl.pallas_call(
        _attn_kernel,
        grid=(b, t // tq),
        in_specs=[
            pl.BlockSpec(memory_space=pltpu.SMEM),
            _const_spec((1, ATT_WIDTH)),
            _const_spec((1, ATT_KV_WIDTH)),
            _const_spec(bias.shape),
            _const_spec(ones_q.shape),
            _const_spec(ones_k.shape),
            _const_spec(den_sel.shape),
            pl.BlockSpec((None, tq, ATT_WIDTH), lambda i, n: (i, n, 0)),
            pl.BlockSpec((None, tq, 2 * ATT_KV_WIDTH), lambda i, n: (i, n, kv_col)),
            pl.BlockSpec((None, blk, 2 * ATT_KV_WIDTH), lambda i, n: (i, jnp.maximum(n * n_sub - 1, 0), kv_col)),
        ],
        out_specs=pl.BlockSpec((None, tq, ATT_WIDTH), lambda i, n: (i, n, 0)),
        out_shape=jax.ShapeDtypeStruct((b, t, ATT_WIDTH), BF16),
        compiler_params=_params("arbitrary", "arbitrary"),
        name="swa_sink_attention",
    )(sinks.astype(F32) * log2e, tile_gain(q_gain, ATT_Q_HEADS) * (HEAD_DIM ** -0.5 * log2e),
      tile_gain(k_gain, ATT_KV_HEADS), bias, ones_q, ones_k, den_sel, p_att, p_att, p_att)


def _rwkv_kernel(p_ref, mu_ref, w0_ref, ww2_ref, a0_ref, wa2_ref, wg2_ref, kk_ref, ka_ref, rk_ref,
                 gnw_ref, gnb_ref, o_ref, prev_ref, state_ref, *, c_len):
    n_seq, tt, _ = p_ref.shape
    width = RWKV_WIDTH
    n_heads = RWKV_HEADS
    n_ch = tt // c_len
    bf = lambda z: z.astype(BF16)
    each = lambda f, *cols: [f(*args) for args in zip(*cols)]

    @pl.when(pl.program_id(1) == 0)
    def _():
        prev_ref[...] = jnp.zeros_like(prev_ref)
        state_ref[...] = jnp.zeros_like(state_ref)

    ones_head = _block_ones(width, HEAD_DIM)

    def head_sum(z):
        return _dot(bf(z), ones_head)

    tri_r = lax.broadcasted_iota(jnp.int32, (tt, tt), 0)
    tri_c = lax.broadcasted_iota(jnp.int32, (tt, tt), 1)
    tri = jnp.where((tri_c <= tri_r) & (tri_c // c_len == tri_r // c_len), 1.0, 0.0).astype(BF16)
    tri2 = jnp.concatenate([tri, tri], axis=1)
    trow = lax.broadcasted_iota(jnp.int32, (tt, 1), 0)
    lane_head = lax.broadcasted_iota(jnp.int32, (1, width), 1) // HEAD_DIM

    def stack(z):
        zero = jnp.zeros_like(z)
        return jnp.concatenate([jnp.where(lane_head == h, z, zero) for h in range(n_heads)], axis=0)

    def head_transpose(z_bd):
        zt = jnp.transpose(z_bd.astype(F32))
        out = zt[0:HEAD_DIM]
        for h in range(1, n_heads):
            out = out + zt[h * HEAD_DIM:(h + 1) * HEAD_DIM]
        return bf(out)

    sl = [slice(j * c_len, (j + 1) * c_len) for j in range(n_ch)]
    cut = lambda z: [z[s] for s in sl]

    def token_features(s):
        p = p_ref[s]
        p_prev = jnp.where(trow == 0, prev_ref[s, 0:1, :], pltpu.roll(p, 1, 0))
        prev_ref[s, 0:1, :] = p[tt - 1:tt, :]
        xs = p + mu_ref[...] * (p_prev - p)
        r = xs[:, 0:width]
        k = xs[:, width:2 * width]
        v = xs[:, 2 * width:3 * width]
        lora = xs[:, 3 * width:3 * width + RWKV_LORA]
        gate_in = xs[:, 3 * width + RWKV_LORA:]
        dw = _dot(bf(jnp.tanh(lora)), ww2_ref[...])
        da = _dot(bf(lora), wa2_ref[...])
        g = _dot(bf(jax.nn.sigmoid(gate_in)), wg2_ref[...])
        lw = -EXP_M05 * jax.nn.sigmoid(w0_ref[...] + dw)
        a = jax.nn.sigmoid(a0_ref[...] + da)
        kk_raw = k * kk_ref[...]
        kk = kk_raw * lax.rsqrt(jnp.maximum(head_sum(kk_raw * kk_raw), 1e-24))
        kmod = k * (1.0 + (a - 1.0) * ka_ref[...])
        b = kk * a
        lw_hi = bf(lw)
        lw_lo = bf(lw - lw_hi.astype(F32))
        lc = _dot(tri2, jnp.concatenate([lw_hi, lw_lo], axis=0))
        w_inv = jnp.exp(-lc)
        lc_c = cut(lc)
        ltot = [z[c_len - 1:c_len, :] for z in lc_c]
        w_end = each(lambda lt, lcj: jnp.exp(lt - lcj), ltot, lc_c)
        chunks = dict(
            ltot=ltot,
            at=cut(bf(kk * jnp.exp(lc - lw))),
            rt=cut(r * jnp.exp(lc)),
            bt=cut(bf(b * w_inv)),
            kt=cut(bf(kmod * w_inv)),
            v=cut(bf(v)),
            bh=each(lambda z, w: bf(z * w), cut(b), w_end),
            kh=each(lambda z, w: bf(z * w), cut(kmod), w_end),
        )
        return chunks, (r, kmod, v, g)

    feats = [token_features(s) for s in range(n_seq)]
    col = lambda name: [z for chunks, _ in feats for z in chunks[name]]
    ltot, at_b, rt_c = col("ltot"), col("at"), col("rt")
    rt_b = each(bf, rt_c)
    bt_bd, kt_bd, v_bd, at_bd = (each(stack, col(nm)) for nm in ("bt", "kt", "v", "at"))
    bh_bd, kh_bd = each(stack, col("bh")), each(stack, col("kh"))

    mi = lax.broadcasted_iota(jnp.int32, (c_len, width), 0)
    mj = lax.broadcasted_iota(jnp.int32, (c_len, width), 1) & (c_len - 1)
    strict = mj < mi
    incl = mj <= mi
    eye = mj == mi

    rows2 = lambda x, y: jnp.concatenate([x, y], axis=0)
    top, bot = (lambda z: z[0:c_len]), (lambda z: z[c_len:2 * c_len])
    ar_b = each(rows2, at_b, rt_b)
    g_b = each(_dot_nt, ar_b, bt_bd)
    g_k = each(_dot_nt, ar_b, kt_bd)
    a_ab = each(lambda z: jnp.where(strict, top(z), 0.0), g_b)
    a_rb = each(lambda z: bf(jnp.where(incl, bot(z), 0.0)), g_b)
    a_ak = each(lambda z: bf(jnp.where(strict, top(z), 0.0)), g_k)
    a_rk = each(lambda z: bf(jnp.where(incl, bot(z), 0.0)), g_k)
    bh_t = each(head_transpose, bh_bd)
    kh_t = each(head_transpose, kh_bd)
    on_v = each(lambda x, y, z, w: _dot(jnp.concatenate([x, y, z], axis=0), w), a_ak, a_rk, kh_t, v_bd)
    y1_bd = each(lambda z: stack(bf(top(z))), on_v)

    blk_i = lambda b: (mi // b, mj // b)
    ti0, tj0 = blk_i(RWKV_INV_BLOCK)
    diag = each(lambda z: jnp.where(ti0 == tj0, z, 0.0), a_ab)
    t_inv = each(lambda z: jnp.where(eye, 1.0, 0.0) - z, diag)
    pw = each(bf, diag)
    n_sq = int(math.log2(RWKV_INV_BLOCK)) - 1
    for lvl in range(n_sq + 1):
        pw_bd = each(stack, pw)
        if lvl == 0:
            pw = each(lambda x, y: bf(_dot(x, y)), pw, pw_bd)
        elif lvl < n_sq:
            res = each(lambda t, x, y: _dot(rows2(bf(t), x), y), t_inv, pw, pw_bd)
            t_inv = each(lambda t, z: t + top(z), t_inv, res)
            pw = each(lambda z: bf(bot(z)), res)
        else:
            t_inv = each(lambda t, y: t + _dot(bf(t), y), t_inv, pw_bd)
    b = RWKV_INV_BLOCK
    while b < c_len:
        ti, tj = blk_i(b)
        corner = (ti // 2 == tj // 2) & (ti % 2 == 1) & (tj % 2 == 0)
        low = each(lambda z: bf(jnp.where(corner, z, 0.0)), a_ab)
        t_b = each(bf, t_inv)
        x = each(lambda t, l: bf(_dot(t, stack(l))), t_b, low)
        t_inv = each(lambda t, xx, tb: t - _dot(xx, stack(tb)), t_inv, x, t_b)
        b *= 2
    t_b = each(bf, t_inv)

    abt_t = each(lambda x, y, t: bf(_dot(rows2(x, y), stack(t))), a_rb, bh_t, t_b)
    on_at2 = each(_dot, abt_t, at_bd)
    on_u2 = each(_dot, abt_t, y1_bd)
    r2 = each(lambda z, m: bf(z - top(m)), rt_c, on_at2)
    p_t = each(lambda lt, m: bf(jnp.where(eye, jnp.exp(lt), 0.0) - bot(m)), ltot, on_at2)
    o2 = each(lambda z, m: z[c_len:2 * c_len] - top(m), on_v, on_u2)
    q_t = each(lambda z, m: z[2 * c_len:3 * c_len] - bot(m), on_v, on_u2)
    rp = each(rows2, r2, p_t)

    states = [state_ref[s] for s in range(n_seq)]
    ys = [[] for _ in range(n_seq)]
    for j in range(n_ch):
        for s in range(n_seq):
            i = s * n_ch + j
            res = _dot(rp[i], stack(bf(states[s])))
            ys[s].append(top(res) + o2[i])
            states[s] = bot(res) + q_t[i]

    for s in range(n_seq):
        state_ref[s] = states[s]
        r, kmod, v, g = feats[s][1]
        y = jnp.concatenate(ys[s], axis=0) if n_ch > 1 else ys[s][0]
        mean = head_sum(y) * (1.0 / HEAD_DIM)
        dev = y - mean
        var = head_sum(dev * dev) * (1.0 / HEAD_DIM)
        yn = dev * lax.rsqrt(var + RWKV_GN_EPS) * gnw_ref[...] + gnb_ref[...]
        bonus = head_sum(r * kmod * rk_ref[...]) * v
        o_ref[s] = ((yn + bonus) * g).astype(o_ref.dtype)


def _rwkv(p_rwkv, mu, w0, ww2_pad, a0, wa2_pad, wg2, k_k, k_a, r_k, gn_w, gn_b, layer):
    b, t, pw = p_rwkv.shape
    tt = min(RWKV_TILE, t)
    c_len = min(RWKV_CHUNK, tt)
    n_seq = RWKV_SEQS if b % RWKV_SEQS == 0 else 1
    vec = lambda n: _layer_spec((1, n), layer)
    return pl.pallas_call(
        functools.partial(_rwkv_kernel, c_len=c_len),
        grid=(b // n_seq, t // tt),
        in_specs=[
            pl.BlockSpec((n_seq, tt, pw), lambda i, c: (i, c, 0)),
            vec(pw), vec(RWKV_WIDTH),
            _layer_spec((RWKV_LORA, RWKV_WIDTH), layer),
            vec(RWKV_WIDTH),
            _layer_spec((RWKV_LORA, RWKV_WIDTH), layer),
            _layer_spec((RWKV_LORA, RWKV_WIDTH), layer),
            vec(RWKV_WIDTH), vec(RWKV_WIDTH), vec(RWKV_WIDTH), vec(RWKV_WIDTH), vec(RWKV_WIDTH),
        ],
        out_specs=pl.BlockSpec((n_seq, tt, RWKV_WIDTH), lambda i, c: (i, c, 0)),
        out_shape=jax.ShapeDtypeStruct((b, t, RWKV_WIDTH), BF16),
        scratch_shapes=[pltpu.VMEM((n_seq, 8, pw), F32), pltpu.VMEM((n_seq, HEAD_DIM, RWKV_WIDTH), F32)],
        compiler_params=_params("arbitrary", "arbitrary"),
        name="rwkv7_chunked",
    )(p_rwkv, mu, w0, ww2_pad, a0, wa2_pad, wg2, k_k, k_a, r_k, gn_w, gn_b)


def kernel(x, c, w_ada, b_ada, g_ffn1, w_ffn1_in, w_ffn1_out, g_mix, w_mix_in, w_mix_out, att_q_gain, att_k_gain, att_sinks, rwkv_mu, rwkv_w0, rwkv_w_w2, rwkv_a0, rwkv_a_w2, rwkv_g_w2, rwkv_k_k, rwkv_k_a, rwkv_r_k, rwkv_gn_w, rwkv_gn_b, conv_w, g_ffn2, w_ffn2_in, w_ffn2_out):
    n_layers, d = g_ffn1.shape
    bsz = x.shape[0]
    row3 = lambda z: z.astype(F32).reshape(n_layers, 1, -1)
    bf = lambda z: z.astype(BF16)

    mod = _modulation(c, w_ada, b_ada).reshape(n_layers, bsz, N_MOD, d)

    half = RWKV_LORA // 2
    zeros = jnp.zeros((n_layers, half, RWKV_WIDTH), F32)
    ww2_pad = bf(jnp.concatenate([rwkv_w_w2, zeros], axis=1))
    wa2_pad = bf(jnp.concatenate([zeros, rwkv_a_w2], axis=1))
    wg2 = bf(rwkv_g_w2)
    g1, gm, g2 = row3(g_ffn1), row3(g_mix), row3(g_ffn2)
    mu, w0, a0 = row3(rwkv_mu), row3(rwkv_w0), row3(rwkv_a0)
    k_k, k_a, r_k = row3(rwkv_k_k), row3(rwkv_k_a), row3(rwkv_r_k)
    gn_w, gn_b = row3(rwkv_gn_w), row3(rwkv_gn_b)

    first = lambda l: [(w_ffn1_in, l), (w_ffn1_out, l), (w_mix_in, l)]
    second = lambda l: [(w_ffn2_in, l), (w_ffn2_out, l), (w_mix_out, l)]
    w1_in, w1_out, w_mix_in_b = bf(w_ffn1_in[0]), bf(w_ffn1_out[0]), bf(w_mix_in[0])

    h = x
    for l in range(n_layers):
        (h, p_att, p_rwkv, y_conv), (w2_in, w2_out, w_mix_out_b) = _ffn_mixin(
            h, mod[l], g1, gm, w1_in, w1_out, w_mix_in_b, conv_w, l, second(l))
        y_att = _attention(p_att, att_q_gain[l], att_k_gain[l], att_sinks[l])
        y_rwkv = _rwkv(p_rwkv, mu, w0, ww2_pad, a0, wa2_pad, wg2, k_k, k_a, r_k, gn_w, gn_b, l)
        h, nxt = _mixout_ffn(h, mod[l], g2, y_att, y_rwkv, y_conv, w_mix_out_b, w2_in, w2_out, l,
                             first(l + 1) if l + 1 < n_layers else [])
        if nxt:
            w1_in, w1_out, w_mix_in_b = nxt
    return h
```
